```python
import jax, jax.numpy as jnp
from jax import lax
import numpy as np

D_MODEL = 1024
BATCH = 8
SEQ = 2048
DEPTH = 2

GRID_W = 64
CTX_LEN = 256
EPS = 1e-6
ROPE_BASE = 10000.0
Q_BLOCK = 128
NEG_INF = -1e30

A_HEADS = 4
A_NOPE = 128
A_ROPE = 64
A_V = 128
A_Q_RANK = 256
A_KV_RANK = 128
B_HEADS = 8
B_KV_HEADS = 2
B_HD = 64
WINDOW = 128
C_HEADS = 4
C_DK = 64
C_DV = 128
C_DECAY_RANK = 16
C_DECAY_TEMP = 16.0
C_CHUNK = 64
D_HEADS = 4
D_KV_HEADS = 2
D_HD = 128
N_BRANCH = 4
BRANCH_W = 512
D_FF = 4 * D_MODEL

A_COLS = A_Q_RANK + A_KV_RANK + A_ROPE
B_COLS = (B_HEADS + 2 * B_KV_HEADS) * B_HD
C_COLS = 2 * C_HEADS * C_DK + 2 * C_HEADS * C_DV + 2 * C_DECAY_RANK
D_COLS = (D_HEADS + 2 * D_KV_HEADS) * D_HD
GATE_COLS = N_BRANCH * D_MODEL
IN_SPLITS = (A_COLS, B_COLS, C_COLS, D_COLS, GATE_COLS)
IN_COLS = A_COLS + B_COLS + C_COLS + D_COLS + GATE_COLS

kernel_name = 'hybrid_parallel_mixer_flow_block'


def rmsnorm(x, w):
    xf = x.astype(jnp.float32)
    y = xf * lax.rsqrt(jnp.mean(xf * xf, axis=-1, keepdims=True) + EPS)
    return (y * w.astype(jnp.float32)).astype(x.dtype)


def modulate(h, shift, scale):
    return h * (1 + scale) + shift


def split_cols(t, sizes):
    return jnp.split(t, [int(s) for s in np.cumsum(sizes)[:-1]], axis=-1)


def to_heads(t, n, d):
    b, l, _ = t.shape
    return t.reshape(b, l, n, d).transpose(0, 2, 1, 3)


def from_heads(t):
    b, h, l, d = t.shape
    return t.transpose(0, 2, 1, 3).reshape(b, l, h * d)


def rope_1d(x, pos):
    half = x.shape[-1] // 2
    inv = ROPE_BASE ** (-jnp.arange(half, dtype=jnp.float32) / half)
    ang = pos.astype(jnp.float32)[:, None] * inv[None, :]
    cos = jnp.cos(ang).astype(x.dtype)
    sin = jnp.sin(ang).astype(x.dtype)
    x1, x2 = x[..., :half], x[..., half:]
    return jnp.concatenate([x1 * cos - x2 * sin, x1 * sin + x2 * cos], axis=-1)


def rope_2d(x, rows, cols):
    half = x.shape[-1] // 2
    return jnp.concatenate([rope_1d(x[..., :half], rows), rope_1d(x[..., half:], cols)], axis=-1)


def dense_block_attention(q, k, v, scale):
    b, hk, g, l, dk = q.shape
    nb = l // Q_BLOCK
    qb = q.reshape(b, hk, g, nb, Q_BLOCK, dk).transpose(3, 0, 1, 2, 4, 5)

    def one_block(qi):
        s = jnp.einsum('bhgqd,bhkd->bhgqk', qi, k).astype(jnp.float32) * scale
        p = jax.nn.softmax(s, axis=-1).astype(v.dtype)
        return jnp.einsum('bhgqk,bhkd->bhgqd', p, v)

    o = lax.map(one_block, qb)
    return o.transpose(1, 2, 3, 0, 4, 5).reshape(b, hk, g, l, v.shape[-1])


def mla_mixer(p_lat, p_ctx, q_norm_w, w_uq, kv_norm_w, w_ukv, rows, cols, need_ctx):
    def project(p, rope):
        c_q, c_kv, k_rope = split_cols(p, [A_Q_RANK, A_KV_RANK, A_ROPE])
        q = to_heads(rmsnorm(c_q, q_norm_w) @ w_uq, A_HEADS, A_NOPE + A_ROPE)
        kv = to_heads(rmsnorm(c_kv, kv_norm_w) @ w_ukv, A_HEADS, A_NOPE + A_V)
        q_nope, q_rope = q[..., :A_NOPE], q[..., A_NOPE:]
        k_nope, v = kv[..., :A_NOPE], kv[..., A_NOPE:]
        k_rope = k_rope[:, None]
        if rope:
            q_rope = rope_2d(q_rope, rows, cols)
            k_rope = rope_2d(k_rope, rows, cols)
        k_rope = jnp.broadcast_to(k_rope, k_nope.shape[:-1] + (A_ROPE,))
        q = jnp.concatenate([q_nope, q_rope], axis=-1)[:, :, None]
        k = jnp.concatenate([k_nope, k_rope], axis=-1)
        return q, k, v

    scale = (A_NOPE + A_ROPE) ** -0.5
    q, k, v = project(p_lat, True)
    qc, kc, vc = project(p_ctx, False)
    y = dense_block_attention(q, jnp.concatenate([kc, k], axis=2), jnp.concatenate([vc, v], axis=2), scale)
    y_lat = from_heads(y[:, :, 0])
    y_ctx = from_heads(dense_block_attention(qc, kc, vc, scale)[:, :, 0]) if need_ctx else None
    return y_lat, y_ctx


def window_sink_mixer(p_lat, p_ctx, sink, rows, cols, need_ctx):
    g = B_HEADS // B_KV_HEADS
    scale = B_HD ** -0.5

    def project(p, rope):
        q, k, v = split_cols(p, [B_HEADS * B_HD, B_KV_HEADS * B_HD, B_KV_HEADS * B_HD])
        q, k, v = to_heads(q, B_HEADS, B_HD), to_heads(k, B_KV_HEADS, B_HD), to_heads(v, B_KV_HEADS, B_HD)
        if rope:
            q, k = rope_2d(q, rows, cols), rope_2d(k, rows, cols)
        b, _, l, _ = q.shape
        return q.reshape(b, B_KV_HEADS, g, l, B_HD), k, v

    q, k, v = project(p_lat, True)
    qc, kc, vc = project(p_ctx, False)
    sink_f = sink.astype(jnp.float32).reshape(B_KV_HEADS, g)
    n_ctx = kc.shape[2]
    b, hk, _, t, d = q.shape
    nb = t // Q_BLOCK
    qb = q.reshape(b, hk, g, nb, Q_BLOCK, d)

    def bands(z):
        zp = jnp.pad(z, ((0, 0), (0, 0), (Q_BLOCK, Q_BLOCK), (0, 0)))
        return jnp.concatenate(
            [zp[:, :, j * Q_BLOCK: j * Q_BLOCK + t].reshape(b, hk, nb, Q_BLOCK, d) for j in range(3)], axis=3)

    kb, vb = bands(k), bands(v)
    qpos = jnp.arange(nb)[:, None] * Q_BLOCK + jnp.arange(Q_BLOCK)[None, :]
    kpos = jnp.arange(nb)[:, None] * Q_BLOCK - Q_BLOCK + jnp.arange(3 * Q_BLOCK)[None, :]
    valid = ((kpos[:, None, :] >= 0) & (kpos[:, None, :] < t)
             & (jnp.abs(qpos[:, :, None] - kpos[:, None, :]) <= WINDOW))
    s_loc = jnp.where(valid, jnp.einsum('bhgnqd,bhnkd->bhgnqk', qb, kb).astype(jnp.float32) * scale, NEG_INF)
    s_ctx = jnp.einsum('bhgnqd,bhcd->bhgnqc', qb, kc).astype(jnp.float32) * scale
    s_sink = jnp.broadcast_to(sink_f[None, :, :, None, None, None], s_loc.shape[:-1] + (1,))
    pr = jax.nn.softmax(jnp.concatenate([s_ctx, s_loc, s_sink], axis=-1), axis=-1).astype(v.dtype)
    o = (jnp.einsum('bhgnqc,bhcd->bhgnqd', pr[..., :n_ctx], vc)
         + jnp.einsum('bhgnqk,bhnkd->bhgnqd', pr[..., n_ctx:n_ctx + 3 * Q_BLOCK], vb))
    y_lat = from_heads(o.reshape(b, hk * g, t, d))
    if need_ctx:
        s = jnp.einsum('bhgqd,bhcd->bhgqc', qc, kc).astype(jnp.float32) * scale
        s_sink_c = jnp.broadcast_to(sink_f[None, :, :, None, None], s.shape[:-1] + (1,))
        pc = jax.nn.softmax(jnp.concatenate([s, s_sink_c], axis=-1), axis=-1)[..., :-1].astype(vc.dtype)
        oc = jnp.einsum('bhgqc,bhcd->bhgqd', pc, vc)
        y_ctx = from_heads(oc.reshape(b, hk * g, n_ctx, d))
    else:
        y_ctx = None
    return y_lat, y_ctx


def gla_chunk_scan(q, k, v, log_a, s0, with_out):
    b, h, t, dk = q.shape
    dv = v.shape[-1]
    n = t // C_CHUNK

    def chunks(z):
        return z.astype(jnp.float32).reshape(b, h, n, C_CHUNK, z.shape[-1]).transpose(2, 0, 1, 3, 4)

    lower_tri = jnp.tril(jnp.ones((C_CHUNK, C_CHUNK), dtype=bool))[:, :, None]

    def body(state, inp):
        qi, ki, vi, lai = inp
        cum = jnp.cumsum(lai, axis=2)
        last = cum[:, :, -1:, :]
        new_state = (jnp.exp(last[:, :, 0, :])[..., None] * state
                     + jnp.einsum('bhsd,bhse->bhde', ki * jnp.exp(last - cum), vi))
        if not with_out:
            return new_state, None
        o_inter = jnp.einsum('bhtd,bhde->bhte', qi * jnp.exp(cum), state)
        diff = cum[:, :, :, None, :] - cum[:, :, None, :, :]
        decay = jnp.exp(jnp.where(lower_tri, diff, -jnp.inf))
        scores = jnp.einsum('bhtd,bhsd,bhtsd->bhts', qi, ki, decay)
        return new_state, o_inter + jnp.einsum('bhts,bhse->bhte', scores, vi)

    state, o = lax.scan(body, s0, (chunks(q), chunks(k), chunks(v), chunks(log_a)))
    if with_out:
        o = o.transpose(1, 2, 0, 3, 4).reshape(b, h, t, dv)
    return state, o


def gla_mixer(p_lat, p_ctx, w_decay, b_decay, norm_w, need_ctx):
    def project(p):
        q, k, v, r, lr_f, lr_b = split_cols(
            p, [C_HEADS * C_DK, C_HEADS * C_DK, C_HEADS * C_DV, C_HEADS * C_DV, C_DECAY_RANK, C_DECAY_RANK])
        la_f = to_heads(jax.nn.log_sigmoid((lr_f @ w_decay[0] + b_decay[0]).astype(jnp.float32)) / C_DECAY_TEMP,
                        C_HEADS, C_DK)
        la_b = to_heads(jax.nn.log_sigmoid((lr_b @ w_decay[1] + b_decay[1]).astype(jnp.float32)) / C_DECAY_TEMP,
                        C_HEADS, C_DK)
        return (to_heads(q, C_HEADS, C_DK) * C_DK ** -0.5, to_heads(k, C_HEADS, C_DK),
                to_heads(v, C_HEADS, C_DV), r, la_f, la_b)

    q, k, v, r, la_f, la_b = project(p_lat)
    qc, kc, vc, rc, lac_f, lac_b = project(p_ctx)

    def flip(z):
        return z[:, :, ::-1]

    s0 = jnp.zeros((q.shape[0], C_HEADS, C_DK, C_DV), jnp.float32)
    s_cf, o_cf = gla_chunk_scan(qc, kc, vc, lac_f, s0, need_ctx)
    s_cb, o_cb = gla_chunk_scan(flip(qc), flip(kc), flip(vc), flip(lac_b), s0, need_ctx)
    _, o_f = gla_chunk_scan(q, k, v, la_f, s_cf, True)
    _, o_b = gla_chunk_scan(flip(q), flip(k), flip(v), flip(la_b), s_cb, True)

    def readout(o, gate):
        return from_heads(rmsnorm(o.astype(gate.dtype), norm_w)) * jax.nn.silu(gate)

    y_lat = readout(o_f + flip(o_b), r)
    y_ctx = readout(o_cf + flip(o_cb), rc) if need_ctx else None
    return y_lat, y_ctx


def qknorm_gqa_mixer(p_lat, p_ctx, q_norm_w, k_norm_w, rows, cols, need_ctx):
    g = D_HEADS // D_KV_HEADS
    scale = D_HD ** -0.5

    def project(p, rope):
        q, k, v = split_cols(p, [D_HEADS * D_HD, D_KV_HEADS * D_HD, D_KV_HEADS * D_HD])
        q = rmsnorm(to_heads(q, D_HEADS, D_HD), q_norm_w)
        k = rmsnorm(to_heads(k, D_KV_HEADS, D_HD), k_norm_w)
        v = to_heads(v, D_KV_HEADS, D_HD)
        if rope:
            q, k = rope_2d(q, rows, cols), rope_2d(k, rows, cols)
        b, _, l, _ = q.shape
        return q.reshape(b, D_KV_HEADS, g, l, D_HD), k, v

    q, k, v = project(p_lat, True)
    qc, kc, vc = project(p_ctx, False)
    b, _, _, t, _ = q.shape
    y = dense_block_attention(q, jnp.concatenate([kc, k], axis=2), jnp.concatenate([vc, v], axis=2), scale)
    y_lat = from_heads(y.reshape(b, D_HEADS, t, D_HD))
    if need_ctx:
        yc = dense_block_attention(qc, kc, vc, scale)
        y_ctx = from_heads(yc.reshape(b, D_HEADS, kc.shape[2], D_HD))
    else:
        y_ctx = None
    return y_lat, y_ctx


def gated_merge(ys, gate_logits, w_branch, w_o):
    b, l, _ = gate_logits.shape
    gates = jax.nn.sigmoid(gate_logits).reshape(b, l, N_BRANCH, D_MODEL)
    z = jnp.einsum('blnw,nwd->blnd', jnp.stack(ys, axis=2), w_branch)
    return jnp.einsum('blnd,blnd->bld', gates, z) @ w_o


def sqrelu_mlp(h, w1, w2):
    return jnp.square(jax.nn.relu(h @ w1)) @ w2


def setup_inputs(seed: int = 0) -> dict:
    key = jax.random.key(seed)
    ks = jax.random.split(key, 24)
    f32 = jnp.float32
    L, D = DEPTH, D_MODEL

    def nrm(k, shape, scale):
        return jax.random.normal(k, shape, f32) * scale

    def gain(k, shape):
        return 1.0 + 0.05 * jax.random.normal(k, shape, f32)

    return {
        'x': nrm(ks[0], (BATCH, SEQ, D), 1.0),
        'c': nrm(ks[1], (BATCH, D), 1.0),
        'ctx': nrm(ks[2], (BATCH, CTX_LEN, D), 1.0),
        'c_ctx': nrm(ks[3], (D,), 1.0),
        'ada_w': nrm(ks[4], (L, D, 6 * D), 0.5 * D ** -0.5),
        'ada_b': nrm(ks[5], (L, 6 * D), 0.01),
        'norm1_w': gain(ks[6], (L, D)),
        'norm2_w': gain(ks[7], (L, D)),
        'w_in': nrm(ks[8], (L, D, IN_COLS), D ** -0.5),
        'mla_q_norm_w': gain(ks[9], (L, A_Q_RANK)),
        'mla_w_uq': nrm(ks[10], (L, A_Q_RANK, A_HEADS * (A_NOPE + A_ROPE)), A_Q_RANK ** -0.5),
        'mla_kv_norm_w': gain(ks[11], (L, A_KV_RANK)),
        'mla_w_ukv': nrm(ks[12], (L, A_KV_RANK, A_HEADS * (A_NOPE + A_V)), A_KV_RANK ** -0.5),
        'swa_sink': nrm(ks[13], (L, B_HEADS), 1.0),
        'gla_w_decay': nrm(ks[14], (L, 2, C_DECAY_RANK, C_HEADS * C_DK), C_DECAY_RANK ** -0.5),
        'gla_b_decay': nrm(ks[15], (L, 2, C_HEADS * C_DK), 0.01),
        'gla_norm_w': gain(ks[16], (L, C_DV)),
        'gqa_q_norm_w': gain(ks[17], (L, D_HD)),
        'gqa_k_norm_w': gain(ks[18], (L, D_HD)),
        'w_branch': nrm(ks[19], (L, N_BRANCH, BRANCH_W, D), BRANCH_W ** -0.5),
        'w_o': nrm(ks[20], (L, D, D), D ** -0.5),
        'mlp_w1': nrm(ks[21], (L, D, D_FF), D ** -0.5),
        'mlp_w2': nrm(ks[22], (L, D_FF, D), D_FF ** -0.5),
        'final_norm_w': gain(ks[23], (D,)),
    }


def reference(x, c, ctx, c_ctx, ada_w, ada_b, norm1_w, norm2_w, w_in,
              mla_q_norm_w, mla_w_uq, mla_kv_norm_w, mla_w_ukv, swa_sink,
              gla_w_decay, gla_b_decay, gla_norm_w, gqa_q_norm_w, gqa_k_norm_w,
              w_branch, w_o, mlp_w1, mlp_w2, final_norm_w):
    t = x.shape[1]
    ROWS = t // GRID_W
    rows = jnp.repeat(jnp.arange(ROWS), GRID_W)
    cols = jnp.tile(jnp.arange(GRID_W), ROWS)
    xc = ctx
    sc = jax.nn.silu(c)
    scc = jax.nn.silu(c_ctx)
    for l in range(DEPTH):
        need_ctx = l < DEPTH - 1
        mod = jnp.split((sc @ ada_w[l] + ada_b[l])[:, None, :], 6, axis=-1)
        mod_c = jnp.split(scc @ ada_w[l] + ada_b[l], 6, axis=-1)
        h = modulate(rmsnorm(x, norm1_w[l]), mod[0], mod[1])
        hc = modulate(rmsnorm(xc, norm1_w[l]), mod_c[0], mod_c[1])
        p = split_cols(h @ w_in[l], IN_SPLITS)
        pc = split_cols(hc @ w_in[l], IN_SPLITS)
        ya, yca = mla_mixer(p[0], pc[0], mla_q_norm_w[l], mla_w_uq[l], mla_kv_norm_w[l], mla_w_ukv[l],
                            rows, cols, need_ctx)
        yb, ycb = window_sink_mixer(p[1], pc[1], swa_sink[l], rows, cols, need_ctx)
        yg, ycg = gla_mixer(p[2], pc[2], gla_w_decay[l], gla_b_decay[l], gla_norm_w[l], need_ctx)
        yd, ycd = qknorm_gqa_mixer(p[3], pc[3], gqa_q_norm_w[l], gqa_k_norm_w[l], rows, cols, need_ctx)
        x = x + mod[2] * gated_merge([ya, yb, yg, yd], p[4], w_branch[l], w_o[l])
        x = x + mod[5] * sqrelu_mlp(modulate(rmsnorm(x, norm2_w[l]), mod[3], mod[4]), mlp_w1[l], mlp_w2[l])
        if need_ctx:
            xc = xc + mod_c[2] * gated_merge([yca, ycb, ycg, ycd], pc[4], w_branch[l], w_o[l])
            xc = xc + mod_c[5] * sqrelu_mlp(modulate(rmsnorm(xc, norm2_w[l]), mod_c[3], mod_c[4]),
                                            mlp_w1[l], mlp_w2[l])
    return rmsnorm(x, final_norm_w)
```

```python
import functools

import numpy as np
import jax
import jax.numpy as jnp
from jax import lax
from jax.experimental import pallas as pl
from jax.experimental.pallas import tpu as pltpu

F32 = jnp.float32
BF16 = jnp.bfloat16

GRID_W = 64
EPS = 1e-6
ROPE_BASE = 10000.0
NEG_INF = -1e30

A_HEADS, A_NOPE, A_ROPE, A_V, A_Q_RANK, A_KV_RANK = 4, 128, 64, 128, 256, 128
A_HEAD_PAD = 256
B_HEADS, B_KV_HEADS, B_HD, WINDOW, B_BLOCK = 8, 2, 64, 128, 128
C_HEADS, C_DK, C_DV, C_DECAY_RANK, C_DECAY_TEMP, C_CHUNK = 4, 64, 128, 16, 16.0, 64
D_HEADS, D_KV_HEADS, D_HD = 4, 2, 128
N_BRANCH, BRANCH_W = 4, 512

LANE = 128
TOKEN_TILE = 256
VMEM_LIMIT = 56 * 1024 * 1024

P_COLS = 8192
SEG_GATE = (0, 4096)
SEG_DQ, SEG_DK, SEG_DV = (4096, 512), (4608, 256), (4864, 256)
SEG_BQ, SEG_BK, SEG_BV = (5120, 512), (5632, 128), (5760, 128)
SEG_ACQ, SEG_ACKV, SEG_AKR = (5888, 256), (6144, 128), (6272, 128)
SEG_CQ, SEG_CV, SEG_CR, SEG_CK, SEG_CLR = (6400, 256), (6656, 512), (7168, 512), (7680, 256), (7936, 128)


def _cparams(sem):
    return pltpu.CompilerParams(dimension_semantics=sem, vmem_limit_bytes=VMEM_LIMIT)


def _rms(xf, w):
    return xf * lax.rsqrt(jnp.mean(xf * xf, axis=-1, keepdims=True) + EPS) * w


def _dot(a, b):
    return jnp.dot(a, b, preferred_element_type=F32)


def _dot_nt(a, b):
    return lax.dot_general(a, b, (((1,), (1,)), ((), ())), preferred_element_type=F32)


def _dot_tn(a, b):
    return lax.dot_general(a, b, (((0,), (0,)), ((), ())), preferred_element_type=F32)


def _rope(x, cos, sin_signed, blk):
    w = x.shape[-1]
    lane = lax.broadcasted_iota(jnp.int32, x.shape, 1)
    first = (lane % (2 * blk)) < blk
    partner = jnp.where(first, pltpu.roll(x, w - blk, 1), pltpu.roll(x, blk, 1))
    return x * cos + partner * sin_signed


def _rope_tables(t, cn, width, offset, rot):
    half = rot // 2
    q = half // 2
    inv = ROPE_BASE ** (-np.arange(q, dtype=np.float64) / q)
    pos = np.arange(t)
    cos = np.ones((t + cn, width), np.float64)
    sin = np.zeros((t + cn, width), np.float64)
    for part, p in enumerate((pos // GRID_W, pos % GRID_W)):
        ang = p[:, None] * inv[None, :]
        lo = offset + part * half
        cos[:t, lo:lo + half] = np.concatenate([np.cos(ang), np.cos(ang)], axis=1)
        sin[:t, lo:lo + half] = np.concatenate([-np.sin(ang), np.sin(ang)], axis=1)
    return jnp.asarray(cos, F32), jnp.asarray(sin, F32)


def _ada_kernel(cs_ref, w_ref, b_ref, o_ref):
    cs = cs_ref[...]
    s = cs * jax.nn.sigmoid(cs)
    o_ref[0] = _dot(s.astype(BF16), w_ref[0].astype(BF16)) + b_ref[0]


def _ada(cs, ada_w, ada_b):
    nl, d, n = ada_w.shape
    tn = 1536
    rows = cs.shape[0]
    return pl.pallas_call(
        _ada_kernel,
        grid=(nl, n // tn),
        in_specs=[
            pl.BlockSpec((rows, d), lambda l, j: (0, 0)),
            pl.BlockSpec((1, d, tn), lambda l, j: (l, 0, j)),
            pl.BlockSpec((1, 1, tn), lambda l, j: (l, 0, j)),
        ],
        out_specs=pl.BlockSpec((1, rows, tn), lambda l, j: (l, 0, j)),
        out_shape=jax.ShapeDtypeStruct((nl, rows, n), F32),
        compiler_params=_cparams(("parallel", "parallel")),
        name="ada",
    )(cs, ada_w, ada_b.reshape(nl, 1, n))


def _inproj_kernel(x_ref, mod_ref, nw_ref, w_ref, o_ref, *, d):
    mod = mod_ref[0]
    h = _rms(x_ref[0], nw_ref[...]) * (1.0 + mod[:, d:2 * d]) + mod[:, 0:d]
    o_ref[0] = _dot(h.astype(BF16), w_ref[...]).astype(BF16)


def _inproj(xs, modsel, nw, w_p, nlat_blocks):
    b, s, d = xs.shape
    tm, tn = TOKEN_TILE, 2048
    return pl.pallas_call(
        functools.partial(_inproj_kernel, d=d),
        grid=(P_COLS // tn, b, s // tm),
        in_specs=[
            pl.BlockSpec((1, tm, d), lambda j, bb, i: (bb, i, 0)),
            pl.BlockSpec((1, 1, 6 * d), lambda j, bb, i: (bb * 2 + jnp.where(i >= nlat_blocks, 1, 0), 0, 0)),
            pl.BlockSpec((1, d), lambda j, bb, i: (0, 0)),
            pl.BlockSpec((d, tn), lambda j, bb, i: (0, j)),
        ],
        out_specs=pl.BlockSpec((1, tm, tn), lambda j, bb, i: (bb, i, j)),
        out_shape=jax.ShapeDtypeStruct((b, s, P_COLS), BF16),
        compiler_params=_cparams(("parallel", "parallel", "parallel")),
        name="inproj",
    )(xs, modsel, nw, w_p)


def _prep_kernel(acq_ref, ackv_ref, akr_ref, bq_ref, bk_ref, dq_ref, dk_ref,
                 cosa_ref, sina_ref, cosb_ref, sinb_ref, cosd_ref, sind_ref,
                 aqn_ref, akvn_ref, wq_ref, wk_ref, wv_ref, dqn_ref, dkn_ref,
                 qa_ref, ka_ref, va_ref, qb_ref, kb_ref, qd_ref, kd_ref):
    cosa = jnp.concatenate([cosa_ref[...]] * A_HEADS, axis=1)
    sina = jnp.concatenate([sina_ref[...]] * A_HEADS, axis=1)
    cq = _rms(acq_ref[0].astype(F32), aqn_ref[...]).astype(BF16)
    q = _dot(cq, wq_ref[...]) * ((A_NOPE + A_ROPE) ** -0.5)
    qa_ref[0] = _rope(q, cosa, sina, A_ROPE // 4).astype(BF16)
    ckv = _rms(ackv_ref[0].astype(F32), akvn_ref[...]).astype(BF16)
    k = _dot(jnp.concatenate([ckv, akr_ref[0]], axis=1), wk_ref[...])
    ka_ref[0] = _rope(k, cosa, sina, A_ROPE // 4).astype(BF16)
    va_ref[0] = _dot(ckv, wv_ref[...]).astype(BF16)

    cosb, sinb = cosb_ref[...], sinb_ref[...]
    nrep = (B_HEADS * B_HD) // LANE
    qb = _rope(bq_ref[0].astype(F32), jnp.concatenate([cosb] * nrep, axis=1),
               jnp.concatenate([sinb] * nrep, axis=1), B_HD // 4)
    qb_ref[0] = (qb * (B_HD ** -0.5)).astype(BF16)
    kb_ref[0] = _rope(bk_ref[0].astype(F32), cosb, sinb, B_HD // 4).astype(BF16)

    cosd, sind = cosd_ref[...], sind_ref[...]
    dq = dq_ref[0].astype(F32)
    qs = [_rope(_rms(dq[:, h * D_HD:(h + 1) * D_HD], dqn_ref[...]), cosd, sind, D_HD // 4)
          for h in range(D_HEADS)]
    qd_ref[0] = (jnp.concatenate(qs, axis=1) * (D_HD ** -0.5)).astype(BF16)
    dk = dk_ref[0].astype(F32)
    ks = [_rope(_rms(dk[:, h * D_HD:(h + 1) * D_HD], dkn_ref[...]), cosd, sind, D_HD // 4)
          for h in range(D_KV_HEADS)]
    kd_ref[0] = jnp.concatenate(ks, axis=1).astype(BF16)


def _pspec(seg, tm):
    off, width = seg
    return pl.BlockSpec((1, tm, width), lambda bb, i: (bb, i, off // width))


def _prep(p, tabs, wts):
    b, s, _ = p.shape
    tm = TOKEN_TILE
    segs = [SEG_ACQ, SEG_ACKV, SEG_AKR, SEG_BQ, SEG_BK, SEG_DQ, SEG_DK]
    tab_specs = [pl.BlockSpec((tm, t.shape[1]), lambda bb, i: (i, 0)) for t in tabs]
    wt_specs = [pl.BlockSpec(w.shape, lambda bb, i: (0, 0)) for w in wts]
    widths = [A_HEADS * A_HEAD_PAD, A_HEADS * A_HEAD_PAD, A_HEADS * A_V,
              B_HEADS * B_HD, B_KV_HEADS * B_HD, D_HEADS * D_HD, D_KV_HEADS * D_HD]
    return pl.pallas_call(
        _prep_kernel,
        grid=(b, s // tm),
        in_specs=[_pspec(sg, tm) for sg in segs] + tab_specs + wt_specs,
        out_specs=[pl.BlockSpec((1, tm, w), lambda bb, i: (bb, i, 0)) for w in widths],
        out_shape=[jax.ShapeDtypeStruct((b, s, w), BF16) for w in widths],
        compiler_params=_cparams(("parallel", "parallel")),
        name="prep",
    )(*([p] * len(segs)), *tabs, *wts)


def _attn_kernel(q_ref, k_ref, v_ref, o_ref, *, t, nlat_blocks):
    def run(k, v):
        s = _dot_nt(q_ref[0], k)
        m = jnp.max(s, axis=-1, keepdims=True)
        p = jnp.exp(s - m)
        l = jnp.sum(p, axis=-1, keepdims=True)
        o_ref[0] = (_dot(p.astype(BF16), v) / l).astype(o_ref.dtype)

    i = pl.program_id(3)

    @pl.when(i < nlat_blocks)
    def _():
        run(k_ref[0], v_ref[0])

    @pl.when(i >= nlat_blocks)
    def _():
        run(k_ref[0, t:, :], v_ref[0, t:, :])


def _attention(q, k, v, *, hkv, g, dk, dv, vcol0, t, nq_blocks, name):
    b, s, _ = q.shape
    tq = TOKEN_TILE
    return pl.pallas_call(
        functools.partial(_attn_kernel, t=t, nlat_blocks=t // tq),
        grid=(b, hkv, g, nq_blocks),
        in_specs=[
            pl.BlockSpec((1, tq, dk), lambda bb, h, gg, i: (bb, i, h * g + gg)),
            pl.BlockSpec((1, s, dk), lambda bb, h, gg, i: (bb, 0, h)),
            pl.BlockSpec((1, s, dv), lambda bb, h, gg, i: (bb, 0, vcol0 + h)),
        ],
        out_specs=pl.BlockSpec((1, tq, dv), lambda bb, h, gg, i: (bb, i, h * g + gg)),
        out_shape=jax.ShapeDtypeStruct((b, s, hkv * g * dv), BF16),
        compiler_params=_cparams(("parallel", "parallel", "parallel", "parallel")),
        name=name,
    )(q, k, v)


def _swa_kernel(sink_ref, q_ref, k_ref, v_ref, o_ref, *, t, cn):
    n = pl.program_id(1)
    nlat = t // B_BLOCK
    grp = B_HEADS // B_KV_HEADS
    band = 3 * B_BLOCK

    def run(is_ctx):
        q = q_ref[0]
        kc, vc = k_ref[0, t:t + cn, :], v_ref[0, t:t + cn, :]
        if not is_ctx:
            start = pl.multiple_of(jnp.clip((n - 1) * B_BLOCK, 0, t - band), B_BLOCK)
            kb, vb = k_ref[0, pl.ds(start, band), :], v_ref[0, pl.ds(start, band), :]
            rows = lax.broadcasted_iota(jnp.int32, (grp * B_BLOCK, band), 0)
            qpos = n * B_BLOCK + rows % B_BLOCK
            kpos = start + lax.broadcasted_iota(jnp.int32, (grp * B_BLOCK, band), 1)
            valid = jnp.abs(qpos - kpos) <= WINDOW
        outs = []
        for hk in range(B_KV_HEADS):
            hs = slice(hk * B_HD, (hk + 1) * B_HD)
            q4 = jnp.concatenate(
                [q[:, (hk * grp + gi) * B_HD:(hk * grp + gi + 1) * B_HD] for gi in range(grp)], axis=0)
            sink = jnp.concatenate(
                [jnp.full((B_BLOCK, 1), sink_ref[hk * grp + gi], F32) for gi in range(grp)], axis=0)
            s_ctx = _dot_nt(q4, kc[:, hs])
            m = jnp.maximum(jnp.max(s_ctx, axis=-1, keepdims=True), sink)
            if not is_ctx:
                s_loc = jnp.where(valid, _dot_nt(q4, kb[:, hs]), NEG_INF)
                m = jnp.maximum(m, jnp.max(s_loc, axis=-1, keepdims=True))
            p_ctx = jnp.exp(s_ctx - m)
            l = jnp.sum(p_ctx, axis=-1, keepdims=True) + jnp.exp(sink - m)
            o = _dot(p_ctx.astype(BF16), vc[:, hs])
            if not is_ctx:
                p_loc = jnp.exp(s_loc - m)
                l = l + jnp.sum(p_loc, axis=-1, keepdims=True)
                o = o + _dot(p_loc.astype(BF16), vb[:, hs])
            o = o / l
            outs += [o[gi * B_BLOCK:(gi + 1) * B_BLOCK] for gi in range(grp)]
        o_ref[0] = jnp.concatenate(outs, axis=1).astype(o_ref.dtype)

    @pl.when(n < nlat)
    def _():
        run(False)

    @pl.when(n >= nlat)
    def _():
        run(True)


def _swa(sink, q, k, p, *, t, cn, n_blocks):
    b, s, _ = q.shape
    return pl.pallas_call(
        functools.partial(_swa_kernel, t=t, cn=cn),
        grid=(b, n_blocks),
        in_specs=[
            pl.BlockSpec(memory_space=pltpu.SMEM),
            pl.BlockSpec((1, B_BLOCK, B_HEADS * B_HD), lambda bb, n: (bb, n, 0)),
            pl.BlockSpec((1, s, B_KV_HEADS * B_HD), lambda bb, n: (bb, 0, 0)),
            pl.BlockSpec((1, s, SEG_BV[1]), lambda bb, n: (bb, 0, SEG_BV[0] // SEG_BV[1])),
        ],
        out_specs=pl.BlockSpec((1, B_BLOCK, B_HEADS * B_HD), lambda bb, n: (bb, n, 0)),
        out_shape=jax.ShapeDtypeStruct((b, s, B_HEADS * B_HD), BF16),
        compiler_params=_cparams(("parallel", "parallel")),
        name="swa",
    )(sink, q, k, p)


def _gla_kernel(qf_ref, kf_ref, vf_ref, lrf_ref, qb_ref, kb_ref, vb_ref, lrb_ref,
                wdec_ref, bdec_ref, sel_ref, of_ref, ob_ref, st_ref, e_ref):
    c = C_CHUNK
    pair = LANE // C_DK
    npair = C_HEADS // pair
    tile = 16

    @pl.when((pl.program_id(0) == 0) & (pl.program_id(1) == 0))
    def _():
        e_ref[...] = jnp.zeros_like(e_ref)

    @pl.when(pl.program_id(1) == 0)
    def _():
        st_ref[...] = jnp.zeros_like(st_ref)

    r_i = lax.broadcasted_iota(jnp.int32, (c, c), 0)
    c_i = lax.broadcasted_iota(jnp.int32, (c, c), 1)
    tri = ((c_i <= r_i).astype(BF16), (c_i >= r_i).astype(BF16))
    refs = ((qf_ref, kf_ref, vf_ref, lrf_ref), (qb_ref, kb_ref, vb_ref, lrb_ref))
    per_dir = []
    for d, (q_ref, k_ref, v_ref, lr_ref) in enumerate(refs):
        q = q_ref[0].astype(F32) * (C_DK ** -0.5)
        k = k_ref[0].astype(F32)
        z = _dot(lr_ref[0], wdec_ref[d]) + bdec_ref[d]
        la = (jnp.minimum(z, 0.0) - jnp.log(1.0 + jnp.exp(-jnp.abs(z)))) / C_DECAY_TEMP
        hi = la.astype(BF16)
        lo = (la - hi.astype(F32)).astype(BF16)
        cum = _dot(tri[d], hi) + _dot(tri[d], lo)
        tot = cum[c - 1:c] if d == 0 else cum[0:1]
        qe = (q * jnp.exp(cum)).astype(BF16)
        kd = (k * jnp.exp(tot - cum)).astype(BF16)
        per_dir.append((qe, kd, tot, v_ref[0]))
        for p in range(npair):
            ls = slice(p * LANE, (p + 1) * LANE)
            q2, k2, c2 = q[:, ls], k[:, ls], cum[:, ls]
            r0 = (d * npair + p) * c
            for s in range(c):
                bt = (s // tile) * tile
                t0, t1 = (bt, c) if d == 0 else (0, bt + tile)
                e = q2[t0:t1] * k2[s:s + 1] * jnp.exp(c2[t0:t1] - c2[s:s + 1])
                ridx = lax.broadcasted_iota(jnp.int32, (t1 - t0, LANE), 0) + t0
                keep = (ridx >= s) if d == 0 else (ridx <= s)
                e_ref[r0 + t0:r0 + t1, s * LANE:(s + 1) * LANE] = jnp.where(keep, e, 0.0).astype(BF16)
    pm = _dot(e_ref[...], sel_ref[...])
    for d, o_ref in enumerate((of_ref, ob_ref)):
        qe, kd, tot, v = per_dir[d]
        st = st_ref[d]
        outs, upds = [], []
        for h in range(C_HEADS):
            p, hh = divmod(h, pair)
            r0 = (d * npair + p) * c
            ph = pm[r0:r0 + c, hh * c:(hh + 1) * c].astype(BF16)
            vh = v[:, h * C_DV:(h + 1) * C_DV]
            hs = slice(h * C_DK, (h + 1) * C_DK)
            outs.append(_dot(ph, vh) + _dot_nt(qe[:, hs], st[:, hs].astype(BF16)))
            upds.append(_dot_tn(vh, kd[:, hs]))
        o_ref[0] = jnp.concatenate(outs, axis=1)
        st_ref[d] = st * jnp.exp(tot) + jnp.concatenate(upds, axis=1)


def _gla_sel():
    pair = LANE // C_DK
    sel = np.zeros((C_CHUNK, pair, C_DK, pair, C_CHUNK), np.float32)
    for s in range(C_CHUNK):
        for h in range(pair):
            sel[s, h, :, h, s] = 1.0
    return jnp.asarray(sel.reshape(C_CHUNK * LANE, pair * C_CHUNK), BF16)


def _gla(p, wdec, bdec, sel, *, t, cn):
    b, s, _ = p.shape
    c = C_CHUNK
    ncl, ncc = t // c, cn // c

    def fwd(j):
        return jnp.where(j < ncc, ncl + j, j - ncc)

    def bwd(j):
        return jnp.where(j < ncc, ncl + ncc - 1 - j, ncl + ncc - 1 - j)

    def specs(order):
        return [pl.BlockSpec((1, c, sg[1]), lambda bb, j, sg=sg: (bb, order(j), sg[0] // sg[1]))
                for sg in (SEG_CQ, SEG_CK, SEG_CV, SEG_CLR)]

    width = C_HEADS * C_DV
    return pl.pallas_call(
        _gla_kernel,
        grid=(b, ncl + ncc),
        in_specs=specs(fwd) + specs(bwd) + [
            pl.BlockSpec(wdec.shape, lambda bb, j: (0, 0, 0)),
            pl.BlockSpec(bdec.shape, lambda bb, j: (0, 0, 0)),
            pl.BlockSpec(sel.shape, lambda bb, j: (0, 0)),
        ],
        out_specs=[pl.BlockSpec((1, c, width), lambda bb, j: (bb, fwd(j), 0)),
                   pl.BlockSpec((1, c, width), lambda bb, j: (bb, bwd(j), 0))],
        out_shape=[jax.ShapeDtypeStruct((b, s, width), F32)] * 2,
        scratch_shapes=[pltpu.VMEM((2, C_DV, C_HEADS * C_DK), F32),
                        pltpu.VMEM((2 * (C_HEADS * C_DK // LANE) * c, c * LANE), BF16)],
        compiler_params=_cparams(("arbitrary", "arbitrary")),
        name="gla",
    )(*([p] * 8), wdec, bdec, sel)


def _merge_kernel(x_ref, mod_ref, ya_ref, yb_ref, of_ref, ob_ref, r_ref, yd_ref, g_ref,
                  gnw_ref, wb_ref, wo_ref, o_ref, *, d):
    o = of_ref[0] + ob_ref[0]
    r = r_ref[0].astype(F32)
    yg = jnp.concatenate([_rms(o[:, h * C_DV:(h + 1) * C_DV], gnw_ref[...]) for h in range(C_HEADS)], axis=1)
    yg = (yg * (r * jax.nn.sigmoid(r))).astype(BF16)
    acc = None
    for n, y in enumerate((ya_ref[0], yb_ref[0], yg, yd_ref[0])):
        gate = jax.nn.sigmoid(g_ref[0, :, n * d:(n + 1) * d].astype(F32))
        term = gate * _dot(y, wb_ref[n])
        acc = term if acc is None else acc + term
    out = _dot(acc.astype(BF16), wo_ref[...])
    o_ref[0] = x_ref[0] + mod_ref[0][:, 2 * d:3 * d] * out


def _merge(xs, modsel, ya, yb, o_f, o_b, p, yd, gnw, wb, wo, *, n_blocks, nlat_blocks):
    b, s, d = xs.shape
    tm = TOKEN_TILE

    def tok(w, col=0):
        return pl.BlockSpec((1, tm, w), lambda bb, i: (bb, i, col))

    return pl.pallas_call(
        functools.partial(_merge_kernel, d=d),
        grid=(b, n_blocks),
        in_specs=[
            tok(d),
            pl.BlockSpec((1, 1, 6 * d), lambda bb, i: (bb * 2 + jnp.where(i >= nlat_blocks, 1, 0), 0, 0)),
            tok(BRANCH_W), tok(BRANCH_W), tok(BRANCH_W), tok(BRANCH_W),
            tok(SEG_CR[1], SEG_CR[0] // SEG_CR[1]),
            tok(BRANCH_W),
            tok(SEG_GATE[1], 0),
            pl.BlockSpec(gnw.shape, lambda bb, i: (0, 0)),
            pl.BlockSpec(wb.shape, lambda bb, i: (0, 0, 0)),
            pl.BlockSpec(wo.shape, lambda bb, i: (0, 0)),
        ],
        out_specs=tok(d),
        out_shape=jax.ShapeDtypeStruct((b, s, d), F32),
        compiler_params=_cparams(("parallel", "parallel")),
        name="merge",
    )(xs, modsel, ya, yb, o_f, o_b, p, yd, p, gnw, wb, wo)


def _mlp_kernel(x_ref, mod_ref, nw_ref, w1_ref, w2_ref, fnw_ref, o_ref, *, d, final):
    x = x_ref[0]
    mod = mod_ref[0]
    h = _rms(x, nw_ref[...]) * (1.0 + mod[:, 4 * d:5 * d]) + mod[:, 3 * d:4 * d]
    a = jnp.maximum(_dot(h.astype(BF16), w1_ref[...]), 0.0)
    y = x + mod[:, 5 * d:6 * d] * _dot((a * a).astype(BF16), w2_ref[...])
    o_ref[0] = _rms(y, fnw_ref[...]) if final else y


def _mlp(xs, modsel, nw, w1, w2, fnw, *, n_blocks, nlat_blocks, out_rows, final):
    b, s, d = xs.shape
    tm = TOKEN_TILE
    return pl.pallas_call(
        functools.partial(_mlp_kernel, d=d, final=final),
        grid=(b, n_blocks),
        in_specs=[
            pl.BlockSpec((1, tm, d), lambda bb, i: (bb, i, 0)),
            pl.BlockSpec((1, 1, 6 * d), lambda bb, i: (bb * 2 + jnp.where(i >= nlat_blocks, 1, 0), 0, 0)),
            pl.BlockSpec(nw.shape, lambda bb, i: (0, 0)),
            pl.BlockSpec(w1.shape, lambda bb, i: (0, 0)),
            pl.BlockSpec(w2.shape, lambda bb, i: (0, 0)),
            pl.BlockSpec(fnw.shape, lambda bb, i: (0, 0)),
        ],
        out_specs=pl.BlockSpec((1, tm, d), lambda bb, i: (bb, i, 0)),
        out_shape=jax.ShapeDtypeStruct((b, out_rows, d), F32),
        compiler_params=_cparams(("parallel", "parallel")),
        name="mlp",
    )(xs, modsel, nw, w1, w2, fnw)


def _pad_cols(w, n):
    return jnp.concatenate([w, jnp.zeros((w.shape[0], n), w.dtype)], axis=1)


def _layout_w_in(w):
    a0 = 0
    b0 = a0 + A_Q_RANK + A_KV_RANK + A_ROPE
    c0 = b0 + (B_HEADS + 2 * B_KV_HEADS) * B_HD
    d0 = c0 + 2 * C_HEADS * C_DK + 2 * C_HEADS * C_DV + 2 * C_DECAY_RANK
    g0 = d0 + (D_HEADS + 2 * D_KV_HEADS) * D_HD

    def cols(lo, n):
        return w[:, lo:lo + n]

    cq = c0
    ck = cq + C_HEADS * C_DK
    cv = ck + C_HEADS * C_DK
    cr = cv + C_HEADS * C_DV
    clr = cr + C_HEADS * C_DV
    parts = [
        cols(g0, SEG_GATE[1]),
        cols(d0, SEG_DQ[1] + SEG_DK[1] + SEG_DV[1]),
        cols(b0, SEG_BQ[1] + SEG_BK[1] + SEG_BV[1]),
        cols(a0, A_Q_RANK + A_KV_RANK),
        _pad_cols(cols(a0 + A_Q_RANK + A_KV_RANK, A_ROPE), SEG_AKR[1] - A_ROPE),
        cols(cq, SEG_CQ[1]), cols(cv, SEG_CV[1]), cols(cr, SEG_CR[1]), cols(ck, SEG_CK[1]),
        _pad_cols(cols(clr, 2 * C_DECAY_RANK), P_COLS - SEG_CLR[0] - 2 * C_DECAY_RANK),
    ]
    return jnp.concatenate(parts, axis=1).astype(BF16)


def _layout_mla(w_uq, w_ukv):
    zq = jnp.zeros((A_Q_RANK, A_HEAD_PAD - A_NOPE - A_ROPE), w_uq.dtype)
    hq = A_NOPE + A_ROPE
    wq = jnp.concatenate([jnp.concatenate([w_uq[:, h * hq:(h + 1) * hq], zq], axis=1)
                          for h in range(A_HEADS)], axis=1)
    hk = A_NOPE + A_V
    zk = jnp.zeros((A_KV_RANK, A_HEAD_PAD - A_NOPE), w_ukv.dtype)
    wk_top = jnp.concatenate([jnp.concatenate([w_ukv[:, h * hk:h * hk + A_NOPE], zk], axis=1)
                              for h in range(A_HEADS)], axis=1)
    place = np.zeros((SEG_AKR[1], A_HEADS * A_HEAD_PAD), np.float32)
    for h in range(A_HEADS):
        place[np.arange(A_ROPE), h * A_HEAD_PAD + A_NOPE + np.arange(A_ROPE)] = 1.0
    wk = jnp.concatenate([wk_top, jnp.asarray(place)], axis=0)
    wv = jnp.concatenate([w_ukv[:, h * hk + A_NOPE:(h + 1) * hk] for h in range(A_HEADS)], axis=1)
    return wq.astype(BF16), wk.astype(BF16), wv.astype(BF16)


def _layout_decay(w_decay):
    z = jnp.zeros((C_DECAY_RANK, w_decay.shape[-1]), w_decay.dtype)
    tail = jnp.zeros((SEG_CLR[1] - 2 * C_DECAY_RANK, w_decay.shape[-1]), w_decay.dtype)
    return jnp.stack([jnp.concatenate([w_decay[0], z, tail], axis=0),
                      jnp.concatenate([z, w_decay[1], tail], axis=0)]).astype(BF16)


def kernel(x, c, ctx, c_ctx, ada_w, ada_b, norm1_w, norm2_w, w_in, mla_q_norm_w, mla_w_uq, mla_kv_norm_w, mla_w_ukv, swa_sink, gla_w_decay, gla_b_decay, gla_norm_w, gqa_q_norm_w, gqa_k_norm_w, w_branch, w_o, mlp_w1, mlp_w2, final_norm_w):
    b, t, d = x.shape
    cn = ctx.shape[1]
    depth = ada_w.shape[0]
    tm = TOKEN_TILE
    assert t % tm == 0 and cn % tm == 0 and t % GRID_W == 0 and t >= 3 * B_BLOCK
    nlat, nall = t // tm, (t + cn) // tm

    rows = 16
    cs = jnp.concatenate([c, c_ctx[None], jnp.zeros((rows - b - 1, d), F32)], axis=0)
    mod = _ada(cs, ada_w, ada_b)
    modsel = jnp.stack([mod[:, :b], jnp.broadcast_to(mod[:, b:b + 1], (depth, b, 6 * d))], axis=2)
    modsel = modsel.reshape(depth, b * 2, 1, 6 * d)

    tabs = (_rope_tables(t, cn, A_HEAD_PAD, A_NOPE, A_ROPE)
            + tuple(jnp.tile(tb, (1, LANE // B_HD)) for tb in _rope_tables(t, cn, B_HD, 0, B_HD))
            + _rope_tables(t, cn, D_HD, 0, D_HD))
    sel = _gla_sel()

    xs = jnp.concatenate([x, ctx], axis=1)
    for l in range(depth):
        last = l == depth - 1
        nblk = nlat if last else nall
        w_p = _layout_w_in(w_in[l])
        wq, wk, wv = _layout_mla(mla_w_uq[l], mla_w_ukv[l])
        p = _inproj(xs, modsel[l], norm1_w[l][None], w_p, nlat)
        wts = (mla_q_norm_w[l][None], mla_kv_norm_w[l][None], wq, wk, wv,
               gqa_q_norm_w[l][None], gqa_k_norm_w[l][None])
        qa, ka, va, qb, kb, qd, kd = _prep(p, tabs, wts)
        ya = _attention(qa, ka, va, hkv=A_HEADS, g=1, dk=A_HEAD_PAD, dv=A_V, vcol0=0,
                        t=t, nq_blocks=nblk, name="attn_a")
        yd = _attention(qd, kd, p, hkv=D_KV_HEADS, g=D_HEADS // D_KV_HEADS, dk=D_HD, dv=D_HD,
                        vcol0=SEG_DV[0] // D_HD, t=t, nq_blocks=nblk, name="attn_d")
        yb = _swa(swa_sink[l], qb, kb, p, t=t, cn=cn,
                  n_blocks=(t if last else t + cn) // B_BLOCK)
        o_f, o_b = _gla(p, _layout_decay(gla_w_decay[l]), gla_b_decay[l][:, None, :], sel, t=t, cn=cn)
        xs = _merge(xs, modsel[l], ya, yb, o_f, o_b, p, yd, gla_norm_w[l][None],
                    w_branch[l].astype(BF16), w_o[l].astype(BF16), n_blocks=nblk, nlat_blocks=nlat)
        xs = _mlp(xs, modsel[l], norm2_w[l][None], mlp_w1[l].astype(BF16), mlp_w2[l].astype(BF16),
                  final_norm_w[None], n_blocks=nblk, nlat_blocks=nlat,
                  out_rows=t if last else t + cn, final=last)
    return xs
```

```python
import functools

import numpy as np
import jax
import jax.numpy as jnp
from jax import lax
from jax.experimental import pallas as pl
from jax.experimental.pallas import tpu as pltpu

F32 = jnp.float32
BF16 = jnp.bfloat16

GRID_W = 64
EPS = 1e-6
ROPE_BASE = 10000.0
NEG_INF = -1e30

A_HEADS, A_NOPE, A_ROPE, A_V, A_Q_RANK, A_KV_RANK = 4, 128, 64, 128, 256, 128
A_HEAD_PAD = 256
B_HEADS, B_KV_HEADS, B_HD, WINDOW = 8, 2, 64, 128
SWA_BLOCK = 256
C_HEADS, C_DK, C_DV, C_DECAY_RANK, C_DECAY_TEMP, C_CHUNK = 4, 64, 128, 16, 16.0, 64
D_HEADS, D_KV_HEADS, D_HD = 4, 2, 128
N_BRANCH, BRANCH_W = 4, 512

LANE = 128
TOKEN_TILE = 256
ATTN_CHAINS = 1
VMEM_LIMIT = 56 * 1024 * 1024

P_COLS = 8192
SEG_GATE = (0, 4096)
SEG_DQ, SEG_DK, SEG_DV = (4096, 512), (4608, 256), (4864, 256)
SEG_BQ, SEG_BK, SEG_BV = (5120, 512), (5632, 128), (5760, 128)
SEG_ACQ, SEG_ACKV, SEG_AKR = (5888, 256), (6144, 128), (6272, 128)
SEG_CQ, SEG_CV, SEG_CR, SEG_CK, SEG_CLR = (6400, 256), (6656, 512), (7168, 512), (7680, 256), (7936, 128)


def _cparams(sem):
    return pltpu.CompilerParams(dimension_semantics=sem, vmem_limit_bytes=VMEM_LIMIT)


def _rms(xf, w):
    return xf * lax.rsqrt(jnp.mean(xf * xf, axis=-1, keepdims=True) + EPS) * w


def _dot(a, b):
    return jnp.dot(a, b, preferred_element_type=F32)


def _dot_nt(a, b):
    return lax.dot_general(a, b, (((1,), (1,)), ((), ())), preferred_element_type=F32)


def _dot_tn(a, b):
    return lax.dot_general(a, b, (((0,), (0,)), ((), ())), preferred_element_type=F32)


def _rope_tables(t, cn, width, offset, rot):
    half = rot // 2
    q = half // 2
    inv = ROPE_BASE ** (-np.arange(q, dtype=np.float64) / q)
    pos = np.arange(t)
    cos = np.ones((t + cn, width), np.float64)
    sin = np.zeros((t + cn, width), np.float64)
    for part, p in enumerate((pos // GRID_W, pos % GRID_W)):
        ang = p[:, None] * inv[None, :]
        lo = offset + part * half
        cos[:t, lo:lo + half] = np.concatenate([np.cos(ang), np.cos(ang)], axis=1)
        sin[:t, lo:lo + half] = np.concatenate([-np.sin(ang), np.sin(ang)], axis=1)
    return jnp.asarray(cos, F32), jnp.asarray(sin, F32)


def _ada_kernel(cs_ref, w_ref, b_ref, o_ref):
    cs = cs_ref[...]
    s = cs * jax.nn.sigmoid(cs)
    o_ref[0] = _dot(s.astype(BF16), w_ref[0].astype(BF16)) + b_ref[0]


def _ada(cs, ada_w, ada_b):
    nl, d, n = ada_w.shape
    tn = 1536
    rows = cs.shape[0]
    return pl.pallas_call(
        _ada_kernel,
        grid=(nl, n // tn),
        in_specs=[
            pl.BlockSpec((rows, d), lambda l, j: (0, 0)),
            pl.BlockSpec((1, d, tn), lambda l, j: (l, 0, j)),
            pl.BlockSpec((1, 1, tn), lambda l, j: (l, 0, j)),
        ],
        out_specs=pl.BlockSpec((1, rows, tn), lambda l, j: (l, 0, j)),
        out_shape=jax.ShapeDtypeStruct((nl, rows, n), F32),
        compiler_params=_cparams(("parallel", "parallel")),
        name="ada",
    )(cs, ada_w, ada_b.reshape(nl, 1, n))


def _inproj_kernel(x_ref, mod_ref, nw_ref, w_ref, o_ref, *, d):
    mod = mod_ref[0]
    h = _rms(x_ref[0], nw_ref[...]) * (1.0 + mod[:, d:2 * d]) + mod[:, 0:d]
    o_ref[0] = _dot(h.astype(BF16), w_ref[...]).astype(BF16)


def _inproj(xs, modsel, nw, w_p, nlat_blocks):
    b, s, d = xs.shape
    tm, tn = TOKEN_TILE, 2048
    return pl.pallas_call(
        functools.partial(_inproj_kernel, d=d),
        grid=(P_COLS // tn, b, s // tm),
        in_specs=[
            pl.BlockSpec((1, tm, d), lambda j, bb, i: (bb, i, 0)),
            pl.BlockSpec((1, 1, 6 * d), lambda j, bb, i: (bb * 2 + jnp.where(i >= nlat_blocks, 1, 0), 0, 0)),
            pl.BlockSpec((1, d), lambda j, bb, i: (0, 0)),
            pl.BlockSpec((d, tn), lambda j, bb, i: (0, j)),
        ],
        out_specs=pl.BlockSpec((1, tm, tn), lambda j, bb, i: (bb, i, j)),
        out_shape=jax.ShapeDtypeStruct((b, s, P_COLS), BF16),
        compiler_params=_cparams(("parallel", "parallel", "parallel")),
        name="inproj",
    )(xs, modsel, nw, w_p)


def _prep_kernel(acq_ref, ackv_ref, akr_ref, bq_ref, bk_ref, dq_ref, dk_ref,
                 cosa_ref, sina_ref, cosb_ref, sinb_ref, cosd_ref, sind_ref,
                 aqn_ref, akvn_ref, wq_ref, wk_ref, wv_ref, place_ref, permbk_ref,
                 permdq_ref, permdk_ref, dqn_ref, dqnp_ref, dkn_ref, dknp_ref,
                 qa_ref, ka_ref, va_ref, qb_ref, kb_ref, qd_ref, kd_ref):
    na = A_HEADS * A_HEAD_PAD
    cosa = jnp.concatenate([cosa_ref[...]] * A_HEADS, axis=1)
    sina = jnp.concatenate([sina_ref[...]] * A_HEADS, axis=1)
    cq = _rms(acq_ref[0].astype(F32), aqn_ref[...]).astype(BF16)
    q2 = _dot(cq, wq_ref[...])
    qa_ref[0] = ((q2[:, :na] * cosa + q2[:, na:] * sina) * ((A_NOPE + A_ROPE) ** -0.5)).astype(BF16)
    ckv = _rms(ackv_ref[0].astype(F32), akvn_ref[...]).astype(BF16)
    k2 = _dot(jnp.concatenate([ckv, akr_ref[0]], axis=1), wk_ref[...])
    ka_ref[0] = (k2[:, :na] * cosa + k2[:, na:] * sina).astype(BF16)
    va_ref[0] = _dot(ckv, wv_ref[...]).astype(BF16)

    nb = B_HEADS * LANE
    cosb, sinb = cosb_ref[...], sinb_ref[...]
    b2 = _dot(bq_ref[0], place_ref[...])
    qb = (b2[:, :nb] * jnp.concatenate([cosb] * B_HEADS, axis=1)
          + b2[:, nb:] * jnp.concatenate([sinb] * B_HEADS, axis=1))
    qb_ref[0] = (qb * (B_HD ** -0.5)).astype(BF16)
    bk = bk_ref[0]
    kb_ref[0] = (bk.astype(F32) * cosb + _dot(bk, permbk_ref[...]) * sinb).astype(BF16)

    cosd, sind = cosd_ref[...], sind_ref[...]

    def norm_rope(x_ref, perm_ref, nw_ref, nwp_ref, heads):
        x = x_ref[0]
        xp = _dot(x, perm_ref[...])
        xf = x.astype(F32)
        wc, ws = nw_ref[...] * cosd, nwp_ref[...] * sind
        parts = []
        for h in range(heads):
            hs = slice(h * D_HD, (h + 1) * D_HD)
            xh = xf[:, hs]
            inv = lax.rsqrt(jnp.mean(xh * xh, axis=-1, keepdims=True) + EPS)
            parts.append(inv * (xh * wc + xp[:, hs] * ws))
        return jnp.concatenate(parts, axis=1)

    qd_ref[0] = (norm_rope(dq_ref, permdq_ref, dqn_ref, dqnp_ref, D_HEADS) * (D_HD ** -0.5)).astype(BF16)
    kd_ref[0] = norm_rope(dk_ref, permdk_ref, dkn_ref, dknp_ref, D_KV_HEADS).astype(BF16)


def _pspec(seg, tm):
    off, width = seg
    return pl.BlockSpec((1, tm, width), lambda bb, i: (bb, i, off // width))


def _prep(p, tabs, wts):
    b, s, _ = p.shape
    tm = TOKEN_TILE
    segs = [SEG_ACQ, SEG_ACKV, SEG_AKR, SEG_BQ, SEG_BK, SEG_DQ, SEG_DK]
    tab_specs = [pl.BlockSpec((tm, t.shape[1]), lambda bb, i: (i, 0)) for t in tabs]
    wt_specs = [pl.BlockSpec(w.shape, lambda bb, i: (0, 0)) for w in wts]
    widths = [A_HEADS * A_HEAD_PAD, A_HEADS * A_HEAD_PAD, A_HEADS * A_V,
              B_HEADS * LANE, B_KV_HEADS * B_HD, D_HEADS * D_HD, D_KV_HEADS * D_HD]
    return pl.pallas_call(
        _prep_kernel,
        grid=(b, s // tm),
        in_specs=[_pspec(sg, tm) for sg in segs] + tab_specs + wt_specs,
        out_specs=[pl.BlockSpec((1, tm, w), lambda bb, i: (bb, i, 0)) for w in widths],
        out_shape=[jax.ShapeDtypeStruct((b, s, w), BF16) for w in widths],
        compiler_params=_cparams(("parallel", "parallel")),
        name="prep",
    )(*([p] * len(segs)), *tabs, *wts)


def _attn_kernel(q_ref, k_ref, v_ref, o_ref, *, t, nlat_blocks):
    def run(k, v):
        rows = q_ref.shape[1] // ATTN_CHAINS
        for r in range(ATTN_CHAINS):
            rs = slice(r * rows, (r + 1) * rows)
            s = _dot_nt(q_ref[0, rs, :], k)
            m = jnp.max(s, axis=-1, keepdims=True)
            p = jnp.exp(s - m)
            l = jnp.sum(p, axis=-1, keepdims=True)
            o_ref[0, rs, :] = (_dot(p.astype(BF16), v) / l).astype(o_ref.dtype)

    i = pl.program_id(3)

    @pl.when(i < nlat_blocks)
    def _():
        run(k_ref[0], v_ref[0])

    @pl.when(i >= nlat_blocks)
    def _():
        run(k_ref[0, t:, :], v_ref[0, t:, :])


def _attention(q, k, v, *, hkv, g, dk, dv, vcol0, t, nq_blocks, name):
    b, s, _ = q.shape
    tq = TOKEN_TILE
    return pl.pallas_call(
        functools.partial(_attn_kernel, t=t, nlat_blocks=t // tq),
        grid=(b, hkv, g, nq_blocks),
        in_specs=[
            pl.BlockSpec((1, tq, dk), lambda bb, h, gg, i: (bb, i, h * g + gg)),
            pl.BlockSpec((1, s, dk), lambda bb, h, gg, i: (bb, 0, h)),
            pl.BlockSpec((1, s, dv), lambda bb, h, gg, i: (bb, 0, vcol0 + h)),
        ],
        out_specs=pl.BlockSpec((1, tq, dv), lambda bb, h, gg, i: (bb, i, h * g + gg)),
        out_shape=jax.ShapeDtypeStruct((b, s, hkv * g * dv), BF16),
        compiler_params=_cparams(("parallel", "parallel", "parallel", "parallel")),
        name=name,
    )(q, k, v)


def _swa_kernel(sink_ref, q_ref, k_ref, v_ref, o_ref, *, t, cn):
    n = pl.program_id(1)
    blk = SWA_BLOCK
    nlat = t // blk
    grp = B_HEADS // B_KV_HEADS
    band = blk + 2 * WINDOW

    def run(is_ctx):
        q = q_ref[0]
        kc, vc = k_ref[0, t:t + cn, :], v_ref[0, t:t + cn, :]
        if not is_ctx:
            start = pl.multiple_of(jnp.clip(n * blk - WINDOW, 0, t - band), WINDOW)
            kb, vb = k_ref[0, pl.ds(start, band), :], v_ref[0, pl.ds(start, band), :]
            rows = lax.broadcasted_iota(jnp.int32, (grp * blk, band), 0)
            qpos = n * blk + rows % blk
            kpos = start + lax.broadcasted_iota(jnp.int32, (grp * blk, band), 1)
            valid = jnp.abs(qpos - kpos) <= WINDOW
        outs = []
        for hk in range(B_KV_HEADS):
            q4 = jnp.concatenate(
                [q[:, (hk * grp + gi) * LANE:(hk * grp + gi + 1) * LANE] for gi in range(grp)], axis=0)
            sink = jnp.concatenate(
                [jnp.full((blk, 1), sink_ref[hk * grp + gi], F32) for gi in range(grp)], axis=0)
            s_ctx = _dot_nt(q4, kc)
            m = jnp.maximum(jnp.max(s_ctx, axis=-1, keepdims=True), sink)
            if not is_ctx:
                s_loc = jnp.where(valid, _dot_nt(q4, kb), NEG_INF)
                m = jnp.maximum(m, jnp.max(s_loc, axis=-1, keepdims=True))
            p_ctx = jnp.exp(s_ctx - m)
            l = jnp.sum(p_ctx, axis=-1, keepdims=True) + jnp.exp(sink - m)
            o = _dot(p_ctx.astype(BF16), vc)
            if not is_ctx:
                p_loc = jnp.exp(s_loc - m)
                l = l + jnp.sum(p_loc, axis=-1, keepdims=True)
                o = o + _dot(p_loc.astype(BF16), vb)
            o = o / l
            outs += [o[gi * blk:(gi + 1) * blk] for gi in range(grp)]
        o_ref[0] = jnp.concatenate(outs, axis=1).astype(o_ref.dtype)

    @pl.when(n < nlat)
    def _():
        run(False)

    @pl.when(n >= nlat)
    def _():
        run(True)


def _swa(sink, q, k, p, *, t, cn, n_blocks):
    b, s, _ = q.shape
    return pl.pallas_call(
        functools.partial(_swa_kernel, t=t, cn=cn),
        grid=(b, n_blocks),
        in_specs=[
            pl.BlockSpec(memory_space=pltpu.SMEM),
            pl.BlockSpec((1, SWA_BLOCK, B_HEADS * LANE), lambda bb, n: (bb, n, 0)),
            pl.BlockSpec((1, s, B_KV_HEADS * B_HD), lambda bb, n: (bb, 0, 0)),
            pl.BlockSpec((1, s, SEG_BV[1]), lambda bb, n: (bb, 0, SEG_BV[0] // SEG_BV[1])),
        ],
        out_specs=pl.BlockSpec((1, SWA_BLOCK, B_HEADS * LANE), lambda bb, n: (bb, n, 0)),
        out_shape=jax.ShapeDtypeStruct((b, s, B_HEADS * LANE), BF16),
        compiler_params=_cparams(("parallel", "parallel")),
        name="swa",
    )(sink, q, k, p)


GLA_BLOCK = 256
GLA_SAFE_DECAY = 60.0


def _log_decay(lr, w, bias):
    z = _dot(lr, w) + bias
    return (jnp.minimum(z, 0.0) - jnp.log(1.0 + jnp.exp(-jnp.abs(z)))) / C_DECAY_TEMP


def _cum_decay(la, tri):
    hi = la.astype(BF16)
    lo = (la - hi.astype(F32)).astype(BF16)
    return _dot(tri, hi) + _dot(tri, lo)


def _gla_kernel(qf_ref, kf_ref, vf_ref, lrf_ref, qb_ref, kb_ref, vb_ref, lrb_ref,
                wdec_ref, bdec_ref, sel_ref, of_ref, ob_ref, st_ref, e_ref):
    n = GLA_BLOCK
    pair = LANE // C_DK
    npair = C_HEADS // pair
    qscale = C_DK ** -0.5
    refs = ((qf_ref, kf_ref, vf_ref, lrf_ref), (qb_ref, kb_ref, vb_ref, lrb_ref))
    outs = (of_ref, ob_ref)

    @pl.when(pl.program_id(1) == 0)
    def _():
        st_ref[...] = jnp.zeros_like(st_ref)

    r_i = lax.broadcasted_iota(jnp.int32, (n, n), 0)
    c_i = lax.broadcasted_iota(jnp.int32, (n, n), 1)
    keep = (c_i <= r_i, c_i >= r_i)
    lane = lax.broadcasted_iota(jnp.int32, (1, LANE), 1)
    head_mask = [(lane // C_DK) == hh for hh in range(pair)]

    cums, tots = [], []
    for d in range(2):
        la = _log_decay(refs[d][3][0], wdec_ref[d], bdec_ref[d])
        cum = _cum_decay(la, keep[d].astype(BF16))
        cums.append(cum)
        tots.append(cum[n - 1:n] if d == 0 else cum[0:1])
    safe = jnp.maximum(jnp.max(-tots[0]), jnp.max(-tots[1])) <= GLA_SAFE_DECAY

    @pl.when(safe)
    def _():
        for d in range(2):
            q_ref, k_ref, v_ref, _ = refs[d]
            cum, tot = cums[d], tots[d]
            half = 0.5 * tot
            a = cum - half
            eh = jnp.exp(half)
            qr = q_ref[0].astype(F32) * qscale * jnp.exp(a)
            kr = k_ref[0].astype(F32) * jnp.exp(-a)
            kr_b, kd_b = kr.astype(BF16), (kr * eh).astype(BF16)
            qe = qr * eh
            v, st = v_ref[0], st_ref[d]
            o_parts, u_parts = [], []
            for p in range(npair):
                ls = slice(p * LANE, (p + 1) * LANE)
                st2 = st[:, ls].astype(BF16)
                upd = []
                for hh in range(pair):
                    h = p * pair + hh
                    vh = v[:, h * C_DV:(h + 1) * C_DV]
                    qr_h = jnp.where(head_mask[hh], qr[:, ls], 0.0).astype(BF16)
                    qe_h = jnp.where(head_mask[hh], qe[:, ls], 0.0).astype(BF16)
                    pm = jnp.where(keep[d], _dot_nt(qr_h, kr_b[:, ls]), 0.0).astype(BF16)
                    o_parts.append(_dot(pm, vh) + _dot_nt(qe_h, st2))
                    upd.append(_dot_tn(vh, kd_b[:, ls]))
                u_parts.append(jnp.where(head_mask[0], upd[0], upd[1]))
            outs[d][0] = jnp.concatenate(o_parts, axis=1)
            st_ref[d] = st * jnp.exp(tot) + jnp.concatenate(u_parts, axis=1)

    @pl.when(jnp.logical_not(safe))
    def _():
        c = C_CHUNK
        tile = 16
        nsub = n // c
        r64 = lax.broadcasted_iota(jnp.int32, (c, c), 0)
        c64 = lax.broadcasted_iota(jnp.int32, (c, c), 1)
        tri64 = ((c64 <= r64).astype(BF16), (c64 >= r64).astype(BF16))
        for d in range(2):
            q_ref, k_ref, v_ref, lr_ref = refs[d]
            e_ref[...] = jnp.zeros_like(e_ref)

            def sub(i, carry, d=d, q_ref=q_ref, k_ref=k_ref, v_ref=v_ref, lr_ref=lr_ref):
                rows = pl.ds(pl.multiple_of((i if d == 0 else nsub - 1 - i) * c, c), c)
                q = q_ref[0, rows, :].astype(F32) * qscale
                k = k_ref[0, rows, :].astype(F32)
                v = v_ref[0, rows, :]
                cum = _cum_decay(_log_decay(lr_ref[0, rows, :], wdec_ref[d], bdec_ref[d]), tri64[d])
                tot = cum[c - 1:c] if d == 0 else cum[0:1]
                qe = (q * jnp.exp(cum)).astype(BF16)
                kd = (k * jnp.exp(tot - cum)).astype(BF16)
                for p in range(npair):
                    ls = slice(p * LANE, (p + 1) * LANE)
                    q2, k2, c2 = q[:, ls], k[:, ls], cum[:, ls]
                    for s in range(c):
                        bt = (s // tile) * tile
                        t0, t1 = (bt, c) if d == 0 else (0, bt + tile)
                        e = q2[t0:t1] * k2[s:s + 1] * jnp.exp(c2[t0:t1] - c2[s:s + 1])
                        ridx = lax.broadcasted_iota(jnp.int32, (t1 - t0, LANE), 0) + t0
                        causal = (ridx >= s) if d == 0 else (ridx <= s)
                        e_ref[p * c + t0:p * c + t1, s * LANE:(s + 1) * LANE] = (
                            jnp.where(causal, e, 0.0).astype(BF16))
                pm = _dot(e_ref[...], sel_ref[...])
                st = st_ref[d]
                o_parts, u_parts = [], []
                for h in range(C_HEADS):
                    p, hh = divmod(h, pair)
                    ph = pm[p * c:(p + 1) * c, hh * c:(hh + 1) * c].astype(BF16)
                    vh = v[:, h * C_DV:(h + 1) * C_DV]
                    hs = slice(h * C_DK, (h + 1) * C_DK)
                    o_parts.append(_dot(ph, vh) + _dot_nt(qe[:, hs], st[:, hs].astype(BF16)))
                    u_parts.append(_dot_tn(vh, kd[:, hs]))
                outs[d][0, rows, :] = jnp.concatenate(o_parts, axis=1)
                st_ref[d] = st * jnp.exp(tot) + jnp.concatenate(u_parts, axis=1)
                return carry

            lax.fori_loop(0, nsub, sub, 0)


def _gla_sel():
    pair = LANE // C_DK
    sel = np.zeros((C_CHUNK, pair, C_DK, pair, C_CHUNK), np.float32)
    for s in range(C_CHUNK):
        for h in range(pair):
            sel[s, h, :, h, s] = 1.0
    return jnp.asarray(sel.reshape(C_CHUNK * LANE, pair * C_CHUNK), BF16)


def _gla(p, wdec, bdec, sel, *, t, cn):
    b, s, _ = p.shape
    n = GLA_BLOCK
    nbl, nbc = t // n, cn // n

    def fwd(j):
        return jnp.where(j < nbc, nbl + j, j - nbc)

    def bwd(j):
        return nbl + nbc - 1 - j

    def specs(order):
        return [pl.BlockSpec((1, n, sg[1]), lambda bb, j, sg=sg: (bb, order(j), sg[0] // sg[1]))
                for sg in (SEG_CQ, SEG_CK, SEG_CV, SEG_CLR)]

    width = C_HEADS * C_DV
    return pl.pallas_call(
        _gla_kernel,
        grid=(b, nbl + nbc),
        in_specs=specs(fwd) + specs(bwd) + [
            pl.BlockSpec(wdec.shape, lambda bb, j: (0, 0, 0)),
            pl.BlockSpec(bdec.shape, lambda bb, j: (0, 0, 0)),
            pl.BlockSpec(sel.shape, lambda bb, j: (0, 0)),
        ],
        out_specs=[pl.BlockSpec((1, n, width), lambda bb, j: (bb, fwd(j), 0)),
                   pl.BlockSpec((1, n, width), lambda bb, j: (bb, bwd(j), 0))],
        out_shape=[jax.ShapeDtypeStruct((b, s, width), F32)] * 2,
        scratch_shapes=[pltpu.VMEM((2, C_DV, C_HEADS * C_DK), F32),
                        pltpu.VMEM(((C_HEADS * C_DK // LANE) * C_CHUNK, C_CHUNK * LANE), BF16)],
        compiler_params=_cparams(("parallel", "arbitrary")),
        name="gla",
    )(*([p] * 8), wdec, bdec, sel)


def _merge_kernel(x_ref, mod_ref, ya_ref, yb_ref, of_ref, ob_ref, r_ref, yd_ref, g_ref,
                  gnw_ref, wba_ref, wbb_ref, wbc_ref, wbd_ref, wo_ref, o_ref, *, d):
    o = of_ref[0] + ob_ref[0]
    r = r_ref[0].astype(F32)
    yg = jnp.concatenate([_rms(o[:, h * C_DV:(h + 1) * C_DV], gnw_ref[...]) for h in range(C_HEADS)], axis=1)
    yg = (yg * (r * jax.nn.sigmoid(r))).astype(BF16)
    acc = None
    branches = ((ya_ref[0], wba_ref), (yb_ref[0], wbb_ref), (yg, wbc_ref), (yd_ref[0], wbd_ref))
    for n, (y, w_ref) in enumerate(branches):
        gate = jax.nn.sigmoid(g_ref[0, :, n * d:(n + 1) * d].astype(F32))
        term = gate * _dot(y, w_ref[...])
        acc = term if acc is None else acc + term
    out = _dot(acc.astype(BF16), wo_ref[...])
    o_ref[0] = x_ref[0] + mod_ref[0][:, 2 * d:3 * d] * out


def _merge(xs, modsel, ya, yb, o_f, o_b, p, yd, gnw, wbs, wo, *, n_blocks, nlat_blocks):
    b, s, d = xs.shape
    tm = TOKEN_TILE

    def tok(w, col=0):
        return pl.BlockSpec((1, tm, w), lambda bb, i: (bb, i, col))

    def whole(w):
        return pl.BlockSpec(w.shape, lambda bb, i: (0, 0))

    return pl.pallas_call(
        functools.partial(_merge_kernel, d=d),
        grid=(b, n_blocks),
        in_specs=[
            tok(d),
            pl.BlockSpec((1, 1, 6 * d), lambda bb, i: (bb * 2 + jnp.where(i >= nlat_blocks, 1, 0), 0, 0)),
            tok(ya.shape[2]), tok(yb.shape[2]), tok(o_f.shape[2]), tok(o_b.shape[2]),
            tok(SEG_CR[1], SEG_CR[0] // SEG_CR[1]),
            tok(yd.shape[2]),
            tok(SEG_GATE[1], 0),
            whole(gnw), *[whole(w) for w in wbs], whole(wo),
        ],
        out_specs=tok(d),
        out_shape=jax.ShapeDtypeStruct((b, s, d), F32),
        compiler_params=_cparams(("parallel", "parallel")),
        name="merge",
    )(xs, modsel, ya, yb, o_f, o_b, p, yd, p, gnw, *wbs, wo)


def _mlp_kernel(x_ref, mod_ref, nw_ref, w1_ref, w2_ref, fnw_ref, o_ref, *, d, final):
    x = x_ref[0]
    mod = mod_ref[0]
    h = _rms(x, nw_ref[...]) * (1.0 + mod[:, 4 * d:5 * d]) + mod[:, 3 * d:4 * d]
    a = jnp.maximum(_dot(h.astype(BF16), w1_ref[...]), 0.0)
    y = x + mod[:, 5 * d:6 * d] * _dot((a * a).astype(BF16), w2_ref[...])
    o_ref[0] = _rms(y, fnw_ref[...]) if final else y


def _mlp(xs, modsel, nw, w1, w2, fnw, *, n_blocks, nlat_blocks, out_rows, final):
    b, s, d = xs.shape
    tm = TOKEN_TILE
    return pl.pallas_call(
        functools.partial(_mlp_kernel, d=d, final=final),
        grid=(b, n_blocks),
        in_specs=[
            pl.BlockSpec((1, tm, d), lambda bb, i: (bb, i, 0)),
            pl.BlockSpec((1, 1, 6 * d), lambda bb, i: (bb * 2 + jnp.where(i >= nlat_blocks, 1, 0), 0, 0)),
            pl.BlockSpec(nw.shape, lambda bb, i: (0, 0)),
            pl.BlockSpec(w1.shape, lambda bb, i: (0, 0)),
            pl.BlockSpec(w2.shape, lambda bb, i: (0, 0)),
            pl.BlockSpec(fnw.shape, lambda bb, i: (0, 0)),
        ],
        out_specs=pl.BlockSpec((1, tm, d), lambda bb, i: (bb, i, 0)),
        out_shape=jax.ShapeDtypeStruct((b, out_rows, d), F32),
        compiler_params=_cparams(("parallel", "parallel")),
        name="mlp",
    )(xs, modsel, nw, w1, w2, fnw)


def _pad_cols(w, n):
    return jnp.concatenate([w, jnp.zeros((w.shape[0], n), w.dtype)], axis=1)


def _layout_w_in(w):
    a0 = 0
    b0 = a0 + A_Q_RANK + A_KV_RANK + A_ROPE
    c0 = b0 + (B_HEADS + 2 * B_KV_HEADS) * B_HD
    d0 = c0 + 2 * C_HEADS * C_DK + 2 * C_HEADS * C_DV + 2 * C_DECAY_RANK
    g0 = d0 + (D_HEADS + 2 * D_KV_HEADS) * D_HD

    def cols(lo, n):
        return w[:, lo:lo + n]

    cq = c0
    ck = cq + C_HEADS * C_DK
    cv = ck + C_HEADS * C_DK
    cr = cv + C_HEADS * C_DV
    clr = cr + C_HEADS * C_DV
    parts = [
        cols(g0, SEG_GATE[1]),
        cols(d0, SEG_DQ[1] + SEG_DK[1] + SEG_DV[1]),
        cols(b0, SEG_BQ[1] + SEG_BK[1] + SEG_BV[1]),
        cols(a0, A_Q_RANK + A_KV_RANK),
        _pad_cols(cols(a0 + A_Q_RANK + A_KV_RANK, A_ROPE), SEG_AKR[1] - A_ROPE),
        cols(cq, SEG_CQ[1]), cols(cv, SEG_CV[1]), cols(cr, SEG_CR[1]), cols(ck, SEG_CK[1]),
        _pad_cols(cols(clr, 2 * C_DECAY_RANK), P_COLS - SEG_CLR[0] - 2 * C_DECAY_RANK),
    ]
    return jnp.concatenate(parts, axis=1).astype(BF16)


def _layout_mla(w_uq, w_ukv):
    zq = jnp.zeros((A_Q_RANK, A_HEAD_PAD - A_NOPE - A_ROPE), w_uq.dtype)
    hq = A_NOPE + A_ROPE
    wq = jnp.concatenate([jnp.concatenate([w_uq[:, h * hq:(h + 1) * hq], zq], axis=1)
                          for h in range(A_HEADS)], axis=1)
    hk = A_NOPE + A_V
    zk = jnp.zeros((A_KV_RANK, A_HEAD_PAD - A_NOPE), w_ukv.dtype)
    wk_top = jnp.concatenate([jnp.concatenate([w_ukv[:, h * hk:h * hk + A_NOPE], zk], axis=1)
                              for h in range(A_HEADS)], axis=1)
    place = np.zeros((SEG_AKR[1], A_HEADS * A_HEAD_PAD), np.float32)
    for h in range(A_HEADS):
        place[np.arange(A_ROPE), h * A_HEAD_PAD + A_NOPE + np.arange(A_ROPE)] = 1.0
    wk = jnp.concatenate([wk_top, jnp.asarray(place)], axis=0)
    wv = jnp.concatenate([w_ukv[:, h * hk + A_NOPE:(h + 1) * hk] for h in range(A_HEADS)], axis=1)
    lanes = np.arange(A_HEADS * A_HEAD_PAD)
    rot = ((lanes % A_HEAD_PAD) >= A_NOPE) & ((lanes % A_HEAD_PAD) < A_NOPE + A_ROPE)
    idx = _partner_lanes(lanes.size, A_ROPE // 4)

    def with_partner(w):
        return jnp.concatenate([w, jnp.where(jnp.asarray(rot)[None, :], w[:, idx], 0.0)], axis=1)

    return with_partner(wq).astype(BF16), with_partner(wk).astype(BF16), wv.astype(BF16)


def _partner_lanes(width, blk):
    j = np.arange(width)
    return np.where((j % (2 * blk)) < blk, j + blk, j - blk)


def _perm_matrix(width, blk):
    m = np.zeros((width, width), np.float32)
    m[_partner_lanes(width, blk), np.arange(width)] = 1.0
    return m


def _swa_place():
    grp = B_HEADS // B_KV_HEADS
    place = np.zeros((B_HEADS * B_HD, B_HEADS * LANE), np.float32)
    for h in range(B_HEADS):
        place[h * B_HD + np.arange(B_HD), h * LANE + (h // grp) * B_HD + np.arange(B_HD)] = 1.0
    both = np.concatenate([place, place @ _perm_matrix(B_HEADS * LANE, B_HD // 4)], axis=1)
    return jnp.asarray(both, BF16)


def _layout_w_branch_b(w):
    grp = B_HEADS // B_KV_HEADS
    z = jnp.zeros((B_HD, w.shape[1]), w.dtype)
    parts = []
    for h in range(B_HEADS):
        wh = w[h * B_HD:(h + 1) * B_HD]
        parts += [wh, z] if h // grp == 0 else [z, wh]
    return jnp.concatenate(parts, axis=0)


def _layout_decay(w_decay):
    z = jnp.zeros((C_DECAY_RANK, w_decay.shape[-1]), w_decay.dtype)
    tail = jnp.zeros((SEG_CLR[1] - 2 * C_DECAY_RANK, w_decay.shape[-1]), w_decay.dtype)
    return jnp.stack([jnp.concatenate([w_decay[0], z, tail], axis=0),
                      jnp.concatenate([z, w_decay[1], tail], axis=0)]).astype(BF16)


def kernel(x, c, ctx, c_ctx, ada_w, ada_b, norm1_w, norm2_w, w_in, mla_q_norm_w, mla_w_uq, mla_kv_norm_w, mla_w_ukv, swa_sink, gla_w_decay, gla_b_decay, gla_norm_w, gqa_q_norm_w, gqa_k_norm_w, w_branch, w_o, mlp_w1, mlp_w2, final_norm_w):
    b, t, d = x.shape
    cn = ctx.shape[1]
    depth = ada_w.shape[0]
    tm = TOKEN_TILE
    assert t % tm == 0 and cn % tm == 0 and t % GRID_W == 0 and t >= SWA_BLOCK + 2 * WINDOW
    nlat, nall = t // tm, (t + cn) // tm

    rows = 16
    cs = jnp.concatenate([c, c_ctx[None], jnp.zeros((rows - b - 1, d), F32)], axis=0)
    mod = _ada(cs, ada_w, ada_b)
    modsel = jnp.stack([mod[:, :b], jnp.broadcast_to(mod[:, b:b + 1], (depth, b, 6 * d))], axis=2)
    modsel = modsel.reshape(depth, b * 2, 1, 6 * d)

    tabs = (_rope_tables(t, cn, A_HEAD_PAD, A_NOPE, A_ROPE)
            + tuple(jnp.tile(tb, (1, LANE // B_HD)) for tb in _rope_tables(t, cn, B_HD, 0, B_HD))
            + _rope_tables(t, cn, D_HD, 0, D_HD))
    sel = _gla_sel()
    perms = (_swa_place(), jnp.asarray(_perm_matrix(B_KV_HEADS * B_HD, B_HD // 4), BF16),
             jnp.asarray(_perm_matrix(D_HEADS * D_HD, D_HD // 4), BF16),
             jnp.asarray(_perm_matrix(D_KV_HEADS * D_HD, D_HD // 4), BF16))
    d_partner = _partner_lanes(D_HD, D_HD // 4)

    xs = jnp.concatenate([x, ctx], axis=1)
    for l in range(depth):
        last = l == depth - 1
        nblk = nlat if last else nall
        w_p = _layout_w_in(w_in[l])
        wq, wk, wv = _layout_mla(mla_w_uq[l], mla_w_ukv[l])
        p = _inproj(xs, modsel[l], norm1_w[l][None], w_p, nlat)
        dqn, dkn = gqa_q_norm_w[l], gqa_k_norm_w[l]
        wts = (mla_q_norm_w[l][None], mla_kv_norm_w[l][None], wq, wk, wv, *perms,
               dqn[None], dqn[d_partner][None], dkn[None], dkn[d_partner][None])
        qa, ka, va, qb, kb, qd, kd = _prep(p, tabs, wts)
        ya = _attention(qa, ka, va, hkv=A_HEADS, g=1, dk=A_HEAD_PAD, dv=A_V, vcol0=0,
                        t=t, nq_blocks=nblk, name="attn_a")
        yd = _attention(qd, kd, p, hkv=D_KV_HEADS, g=D_HEADS // D_KV_HEADS, dk=D_HD, dv=D_HD,
                        vcol0=SEG_DV[0] // D_HD, t=t, nq_blocks=nblk, name="attn_d")
        yb = _swa(swa_sink[l], qb, kb, p, t=t, cn=cn,
                  n_blocks=(t if last else t + cn) // SWA_BLOCK)
        o_f, o_b = _gla(p, _layout_decay(gla_w_decay[l]), gla_b_decay[l][:, None, :], sel, t=t, cn=cn)
        wbs = [w_branch[l, 0], _layout_w_branch_b(w_branch[l, 1]), w_branch[l, 2], w_branch[l, 3]]
        xs = _merge(xs, modsel[l], ya, yb, o_f, o_b, p, yd, gla_norm_w[l][None],
                    [w.astype(BF16) for w in wbs], w_o[l].astype(BF16), n_blocks=nblk, nlat_blocks=nlat)
        xs = _mlp(xs, modsel[l], norm2_w[l][None], mlp_w1[l].astype(BF16), mlp_w2[l].astype(BF16),
                  final_norm_w[None], n_blocks=nblk, nlat_blocks=nlat,
                  out_rows=t if last else t + cn, final=last)
    return xs
```

```python
import functools

import numpy as np
import jax
import jax.numpy as jnp
from jax import lax
from jax.experimental import pallas as pl
from jax.experimental.pallas import tpu as pltpu

F32 = jnp.float32
BF16 = jnp.bfloat16

GRID_W = 64
EPS = 1e-6
ROPE_BASE = 10000.0
NEG_INF = -1e30

A_HEADS, A_NOPE, A_ROPE, A_V, A_Q_RANK, A_KV_RANK = 4, 128, 64, 128, 256, 128
A_HEAD_PAD = 256
B_HEADS, B_KV_HEADS, B_HD, WINDOW = 8, 2, 64, 128
SWA_BLOCK = 256
C_HEADS, C_DK, C_DV, C_DECAY_RANK, C_DECAY_TEMP, C_CHUNK = 4, 64, 128, 16, 16.0, 64
D_HEADS, D_KV_HEADS, D_HD = 4, 2, 128
N_BRANCH, BRANCH_W = 4, 512

LANE = 128
TOKEN_TILE = 256
ATTN_TQ = 512
ATTN_HEADS_PER_STEP = 4
VMEM_LIMIT = 56 * 1024 * 1024

P_COLS = 8192
SEG_GATE = (0, 4096)
SEG_DQ, SEG_DK, SEG_DV = (4096, 512), (4608, 256), (4864, 256)
SEG_BQ, SEG_BK, SEG_BV = (5120, 512), (5632, 128), (5760, 128)
SEG_ACQ, SEG_ACKV, SEG_AKR = (5888, 256), (6144, 128), (6272, 128)
SEG_CQ, SEG_CV, SEG_CR, SEG_CK, SEG_CLR = (6400, 256), (6656, 512), (7168, 512), (7680, 256), (7936, 128)


def _cparams(sem):
    return pltpu.CompilerParams(dimension_semantics=sem, vmem_limit_bytes=VMEM_LIMIT)


def _rms(xf, w):
    return xf * lax.rsqrt(jnp.mean(xf * xf, axis=-1, keepdims=True) + EPS) * w


def _dot(a, b):
    return jnp.dot(a, b, preferred_element_type=F32)


def _dot_nt(a, b):
    return lax.dot_general(a, b, (((1,), (1,)), ((), ())), preferred_element_type=F32)


def _dot_tn(a, b):
    return lax.dot_general(a, b, (((0,), (0,)), ((), ())), preferred_element_type=F32)


def _rope_tables(t, cn, width, offset, rot):
    half = rot // 2
    q = half // 2
    inv = ROPE_BASE ** (-np.arange(q, dtype=np.float64) / q)
    pos = np.arange(t)
    cos = np.ones((t + cn, width), np.float64)
    sin = np.zeros((t + cn, width), np.float64)
    for part, p in enumerate((pos // GRID_W, pos % GRID_W)):
        ang = p[:, None] * inv[None, :]
        lo = offset + part * half
        cos[:t, lo:lo + half] = np.concatenate([np.cos(ang), np.cos(ang)], axis=1)
        sin[:t, lo:lo + half] = np.concatenate([-np.sin(ang), np.sin(ang)], axis=1)
    return jnp.asarray(cos, F32), jnp.asarray(sin, F32)


def _ada_kernel(cs_ref, w_ref, b_ref, o_ref):
    cs = cs_ref[...]
    s = cs * jax.nn.sigmoid(cs)
    o_ref[0] = _dot(s.astype(BF16), w_ref[0].astype(BF16)) + b_ref[0]


def _ada(cs, ada_w, ada_b):
    nl, d, n = ada_w.shape
    tn = 1536
    rows = cs.shape[0]
    return pl.pallas_call(
        _ada_kernel,
        grid=(nl, n // tn),
        in_specs=[
            pl.BlockSpec((rows, d), lambda l, j: (0, 0)),
            pl.BlockSpec((1, d, tn), lambda l, j: (l, 0, j)),
            pl.BlockSpec((1, 1, tn), lambda l, j: (l, 0, j)),
        ],
        out_specs=pl.BlockSpec((1, rows, tn), lambda l, j: (l, 0, j)),
        out_shape=jax.ShapeDtypeStruct((nl, rows, n), F32),
        compiler_params=_cparams(("parallel", "parallel")),
        name="ada",
    )(cs, ada_w, ada_b.reshape(nl, 1, n))


def _inproj_kernel(x_ref, mod_ref, nw_ref, w_ref, o_ref, *, d):
    mod = mod_ref[0]
    h = _rms(x_ref[0], nw_ref[...]) * (1.0 + mod[:, d:2 * d]) + mod[:, 0:d]
    o_ref[0] = _dot(h.astype(BF16), w_ref[...]).astype(BF16)


def _inproj(xs, modsel, nw, w_p, nlat_blocks):
    b, s, d = xs.shape
    tm, tn = TOKEN_TILE, 2048
    return pl.pallas_call(
        functools.partial(_inproj_kernel, d=d),
        grid=(P_COLS // tn, b, s // tm),
        in_specs=[
            pl.BlockSpec((1, tm, d), lambda j, bb, i: (bb, i, 0)),
            pl.BlockSpec((1, 1, 6 * d), lambda j, bb, i: (bb * 2 + jnp.where(i >= nlat_blocks, 1, 0), 0, 0)),
            pl.BlockSpec((1, d), lambda j, bb, i: (0, 0)),
            pl.BlockSpec((d, tn), lambda j, bb, i: (0, j)),
        ],
        out_specs=pl.BlockSpec((1, tm, tn), lambda j, bb, i: (bb, i, j)),
        out_shape=jax.ShapeDtypeStruct((b, s, P_COLS), BF16),
        compiler_params=_cparams(("parallel", "parallel", "parallel")),
        name="inproj",
    )(xs, modsel, nw, w_p)


def _prep_kernel(acq_ref, ackv_ref, akr_ref, bq_ref, bk_ref, dq_ref, dk_ref,
                 cosa_ref, sina_ref, cosb_ref, sinb_ref, cosd_ref, sind_ref,
                 aqn_ref, akvn_ref, wq_ref, wk_ref, wv_ref, place_ref, permbk_ref,
                 permdq_ref, permdk_ref, dqn_ref, dqnp_ref, dkn_ref, dknp_ref,
                 qa_ref, ka_ref, va_ref, qb_ref, kb_ref, qd_ref, kd_ref):
    na = A_HEADS * A_HEAD_PAD
    cosa = jnp.concatenate([cosa_ref[...]] * A_HEADS, axis=1)
    sina = jnp.concatenate([sina_ref[...]] * A_HEADS, axis=1)
    cq = _rms(acq_ref[0].astype(F32), aqn_ref[...]).astype(BF16)
    q2 = _dot(cq, wq_ref[...])
    qa_ref[0] = ((q2[:, :na] * cosa + q2[:, na:] * sina) * ((A_NOPE + A_ROPE) ** -0.5)).astype(BF16)
    ckv = _rms(ackv_ref[0].astype(F32), akvn_ref[...]).astype(BF16)
    k2 = _dot(jnp.concatenate([ckv, akr_ref[0]], axis=1), wk_ref[...])
    ka_ref[0] = (k2[:, :na] * cosa + k2[:, na:] * sina).astype(BF16)
    va_ref[0] = _dot(ckv, wv_ref[...]).astype(BF16)

    nb = B_HEADS * LANE
    cosb, sinb = cosb_ref[...], sinb_ref[...]
    b2 = _dot(bq_ref[0], place_ref[...])
    qb = (b2[:, :nb] * jnp.concatenate([cosb] * B_HEADS, axis=1)
          + b2[:, nb:] * jnp.concatenate([sinb] * B_HEADS, axis=1))
    qb_ref[0] = (qb * (B_HD ** -0.5)).astype(BF16)
    bk = bk_ref[0]
    kb_ref[0] = (bk.astype(F32) * cosb + _dot(bk, permbk_ref[...]) * sinb).astype(BF16)

    cosd, sind = cosd_ref[...], sind_ref[...]

    def norm_rope(x_ref, perm_ref, nw_ref, nwp_ref, heads):
        x = x_ref[0]
        xp = _dot(x, perm_ref[...])
        xf = x.astype(F32)
        wc, ws = nw_ref[...] * cosd, nwp_ref[...] * sind
        parts = []
        for h in range(heads):
            hs = slice(h * D_HD, (h + 1) * D_HD)
            xh = xf[:, hs]
            inv = lax.rsqrt(jnp.mean(xh * xh, axis=-1, keepdims=True) + EPS)
            parts.append(inv * (xh * wc + xp[:, hs] * ws))
        return jnp.concatenate(parts, axis=1)

    qd_ref[0] = (norm_rope(dq_ref, permdq_ref, dqn_ref, dqnp_ref, D_HEADS) * (D_HD ** -0.5)).astype(BF16)
    kd_ref[0] = norm_rope(dk_ref, permdk_ref, dkn_ref, dknp_ref, D_KV_HEADS).astype(BF16)


def _pspec(seg, tm):
    off, width = seg
    return pl.BlockSpec((1, tm, width), lambda bb, i: (bb, i, off // width))


def _prep(p, tabs, wts):
    b, s, _ = p.shape
    tm = TOKEN_TILE
    segs = [SEG_ACQ, SEG_ACKV, SEG_AKR, SEG_BQ, SEG_BK, SEG_DQ, SEG_DK]
    tab_specs = [pl.BlockSpec((tm, t.shape[1]), lambda bb, i: (i, 0)) for t in tabs]
    wt_specs = [pl.BlockSpec(w.shape, lambda bb, i: (0, 0)) for w in wts]
    widths = [A_HEADS * A_HEAD_PAD, A_HEADS * A_HEAD_PAD, A_HEADS * A_V,
              B_HEADS * LANE, B_KV_HEADS * B_HD, D_HEADS * D_HD, D_KV_HEADS * D_HD]
    return pl.pallas_call(
        _prep_kernel,
        grid=(b, s // tm),
        in_specs=[_pspec(sg, tm) for sg in segs] + tab_specs + wt_specs,
        out_specs=[pl.BlockSpec((1, tm, w), lambda bb, i: (bb, i, 0)) for w in widths],
        out_shape=[jax.ShapeDtypeStruct((b, s, w), BF16) for w in widths],
        compiler_params=_cparams(("parallel", "parallel")),
        name="prep",
    )(*([p] * len(segs)), *tabs, *wts)


def _attn_kernel(q_ref, k_ref, v_ref, *rest, dk, dv, kv_group):
    o_ref = rest[-1]
    for c in range(ATTN_HEADS_PER_STEP):
        kc = c // kv_group
        s = _dot_nt(q_ref[0, :, c * dk:(c + 1) * dk], k_ref[0, :, kc * dk:(kc + 1) * dk])
        m = jnp.max(s, axis=-1, keepdims=True)
        p = jnp.exp(s - m)
        l = jnp.sum(p, axis=-1, keepdims=True)
        o = _dot(p.astype(BF16), v_ref[0, :, kc * dv:(kc + 1) * dv])
        o_ref[0, :, c * dv:(c + 1) * dv] = (o / l).astype(o_ref.dtype)


def _attention(q, k, v, *, heads, kv_group, dk, dv, vcol0, t, cn, with_ctx, name):
    b, s, _ = q.shape
    hps = ATTN_HEADS_PER_STEP
    nkv = hps // kv_group
    kern = functools.partial(_attn_kernel, dk=dk, dv=dv, kv_group=kv_group)
    sem = _cparams(("parallel", "parallel", "parallel"))
    out_rows = s if with_ctx else t

    def specs(tq, row0, krows, krow0):
        return dict(
            grid=(b, heads // hps, (t if row0 == 0 else cn) // tq),
            in_specs=[
                pl.BlockSpec((1, tq, hps * dk), lambda bb, h, i: (bb, row0 + i, h)),
                pl.BlockSpec((1, krows, nkv * dk), lambda bb, h, i: (bb, krow0, h)),
                pl.BlockSpec((1, krows, nkv * dv), lambda bb, h, i: (bb, krow0, vcol0 + h)),
            ],
            out_specs=pl.BlockSpec((1, tq, hps * dv), lambda bb, h, i: (bb, row0 + i, h)),
            out_shape=jax.ShapeDtypeStruct((b, out_rows, heads * dv), BF16),
            compiler_params=sem,
        )

    y = pl.pallas_call(kern, name=name, **specs(ATTN_TQ, 0, s, 0))(q, k, v)
    if with_ctx:
        sp = specs(cn, t // cn, cn, t // cn)
        sp["in_specs"] = sp["in_specs"] + [pl.BlockSpec(memory_space=pl.ANY)]
        y = pl.pallas_call(kern, name=name + "_ctx", input_output_aliases={3: 0}, **sp)(q, k, v, y)
    return y


def _swa_bias():
    band = SWA_BLOCK + 2 * WINDOW
    r = np.arange(SWA_BLOCK)[:, None]
    j = np.arange(band)[None, :]
    out = []
    for off in (0, WINDOW, band - SWA_BLOCK):
        out.append(np.where(np.abs(off + r - j) <= WINDOW, 0.0, NEG_INF))
    return np.stack(out).astype(np.float32)


def _swa_kernel(sink_ref, q_ref, k_ref, v_ref, bias_ref, o_ref, *, t, cn):
    n = pl.program_id(1)
    blk = SWA_BLOCK
    nlat = t // blk
    grp = B_HEADS // B_KV_HEADS
    band = blk + 2 * WINDOW

    def run(is_ctx):
        q = q_ref[0]
        k, v = k_ref[0, t:t + cn, :], v_ref[0, t:t + cn, :]
        if not is_ctx:
            start = pl.multiple_of(jnp.clip(n * blk - WINDOW, 0, t - band), WINDOW)
            k = jnp.concatenate([k, k_ref[0, pl.ds(start, band), :]], axis=0)
            v = jnp.concatenate([v, v_ref[0, pl.ds(start, band), :]], axis=0)
            bias = jnp.concatenate([bias_ref[0]] * grp, axis=0)
        ones = jnp.ones((k.shape[0], LANE), BF16)
        outs = []
        for hk in range(B_KV_HEADS):
            q4 = jnp.concatenate(
                [q[:, (hk * grp + gi) * LANE:(hk * grp + gi + 1) * LANE] for gi in range(grp)], axis=0)
            sink = jnp.concatenate(
                [jnp.full((blk, 1), sink_ref[hk * grp + gi], F32) for gi in range(grp)], axis=0)
            s = _dot_nt(q4, k)
            if not is_ctx:
                s = s + bias
            m = jnp.maximum(jnp.max(s, axis=-1, keepdims=True), sink)
            p = jnp.exp(s - m).astype(BF16)
            o = _dot(p, v) / (_dot(p, ones) + jnp.exp(sink - m))
            outs += [o[gi * blk:(gi + 1) * blk] for gi in range(grp)]
        o_ref[0] = jnp.concatenate(outs, axis=1).astype(o_ref.dtype)

    @pl.when(n < nlat)
    def _():
        run(False)

    @pl.when(n >= nlat)
    def _():
        run(True)


def _swa(sink, q, k, p, *, t, cn, n_blocks):
    b, s, _ = q.shape
    nlat = t // SWA_BLOCK
    band_bias = _swa_bias()
    bias = jnp.asarray(np.concatenate([np.zeros(band_bias.shape[:2] + (cn,), np.float32), band_bias], axis=2))
    return pl.pallas_call(
        functools.partial(_swa_kernel, t=t, cn=cn),
        grid=(b, n_blocks),
        in_specs=[
            pl.BlockSpec(memory_space=pltpu.SMEM),
            pl.BlockSpec((1, SWA_BLOCK, B_HEADS * LANE), lambda bb, n: (bb, n, 0)),
            pl.BlockSpec((1, s, B_KV_HEADS * B_HD), lambda bb, n: (bb, 0, 0)),
            pl.BlockSpec((1, s, SEG_BV[1]), lambda bb, n: (bb, 0, SEG_BV[0] // SEG_BV[1])),
            pl.BlockSpec((1,) + bias.shape[1:],
                         lambda bb, n: (jnp.where(n == 0, 0, jnp.where(n >= nlat - 1, 2, 1)), 0, 0)),
        ],
        out_specs=pl.BlockSpec((1, SWA_BLOCK, B_HEADS * LANE), lambda bb, n: (bb, n, 0)),
        out_shape=jax.ShapeDtypeStruct((b, n_blocks * SWA_BLOCK, B_HEADS * LANE), BF16),
        compiler_params=_cparams(("parallel", "parallel")),
        name="swa",
    )(sink, q, k, p, bias)


GLA_BLOCK = 256
GLA_SAFE_DECAY = 60.0


def _log_decay(lr, w, bias):
    z = _dot(lr, w) + bias
    return (jnp.minimum(z, 0.0) - jnp.log(1.0 + jnp.exp(-jnp.abs(z)))) / C_DECAY_TEMP


def _cum_decay(la, tri):
    hi = la.astype(BF16)
    lo = (la - hi.astype(F32)).astype(BF16)
    return _dot(tri, hi) + _dot(tri, lo)


def _gla_kernel(qf_ref, kf_ref, vf_ref, lrf_ref, qb_ref, kb_ref, vb_ref, lrb_ref,
                wdec_ref, bdec_ref, sel_ref, of_ref, ob_ref, st_ref, e_ref):
    n = GLA_BLOCK
    pair = LANE // C_DK
    npair = C_HEADS // pair
    qscale = C_DK ** -0.5
    refs = ((qf_ref, kf_ref, vf_ref, lrf_ref), (qb_ref, kb_ref, vb_ref, lrb_ref))
    outs = (of_ref, ob_ref)

    @pl.when(pl.program_id(1) == 0)
    def _():
        st_ref[...] = jnp.zeros_like(st_ref)

    r_i = lax.broadcasted_iota(jnp.int32, (n, n), 0)
    c_i = lax.broadcasted_iota(jnp.int32, (n, n), 1)
    keep = (c_i <= r_i, c_i >= r_i)
    lane = lax.broadcasted_iota(jnp.int32, (1, LANE), 1)
    head_mask = [(lane // C_DK) == hh for hh in range(pair)]

    cums, tots = [], []
    for d in range(2):
        la = _log_decay(refs[d][3][0], wdec_ref[d], bdec_ref[d])
        cum = _cum_decay(la, keep[d].astype(BF16))
        cums.append(cum)
        tots.append(cum[n - 1:n] if d == 0 else cum[0:1])
    safe = jnp.maximum(jnp.max(-tots[0]), jnp.max(-tots[1])) <= GLA_SAFE_DECAY

    @pl.when(safe)
    def _():
        for d in range(2):
            q_ref, k_ref, v_ref, _ = refs[d]
            cum, tot = cums[d], tots[d]
            half = 0.5 * tot
            a = cum - half
            eh = jnp.exp(half)
            qr = q_ref[0].astype(F32) * qscale * jnp.exp(a)
            kr = k_ref[0].astype(F32) * jnp.exp(-a)
            kr_b, kd_b = kr.astype(BF16), (kr * eh).astype(BF16)
            qe = qr * eh
            v, st = v_ref[0], st_ref[d]
            o_parts, u_parts = [], []
            for p in range(npair):
                ls = slice(p * LANE, (p + 1) * LANE)
                st2 = st[:, ls].astype(BF16)
                upd = []
                for hh in range(pair):
                    h = p * pair + hh
                    vh = v[:, h * C_DV:(h + 1) * C_DV]
                    qr_h = jnp.where(head_mask[hh], qr[:, ls], 0.0).astype(BF16)
                    qe_h = jnp.where(head_mask[hh], qe[:, ls], 0.0).astype(BF16)
                    pm = jnp.where(keep[d], _dot_nt(qr_h, kr_b[:, ls]), 0.0).astype(BF16)
                    o_parts.append(_dot(pm, vh) + _dot_nt(qe_h, st2))
                    upd.append(_dot_tn(vh, kd_b[:, ls]))
                u_parts.append(jnp.where(head_mask[0], upd[0], upd[1]))
            outs[d][0] = jnp.concatenate(o_parts, axis=1)
            st_ref[d] = st * jnp.exp(tot) + jnp.concatenate(u_parts, axis=1)

    @pl.when(jnp.logical_not(safe))
    def _():
        c = C_CHUNK
        tile = 16
        nsub = n // c
        r64 = lax.broadcasted_iota(jnp.int32, (c, c), 0)
        c64 = lax.broadcasted_iota(jnp.int32, (c, c), 1)
        tri64 = ((c64 <= r64).astype(BF16), (c64 >= r64).astype(BF16))
        for d in range(2):
            q_ref, k_ref, v_ref, lr_ref = refs[d]
            e_ref[...] = jnp.zeros_like(e_ref)

            def sub(i, carry, d=d, q_ref=q_ref, k_ref=k_ref, v_ref=v_ref, lr_ref=lr_ref):
                rows = pl.ds(pl.multiple_of((i if d == 0 else nsub - 1 - i) * c, c), c)
                q = q_ref[0, rows, :].astype(F32) * qscale
                k = k_ref[0, rows, :].astype(F32)
                v = v_ref[0, rows, :]
                cum = _cum_decay(_log_decay(lr_ref[0, rows, :], wdec_ref[d], bdec_ref[d]), tri64[d])
                tot = cum[c - 1:c] if d == 0 else cum[0:1]
                qe = (q * jnp.exp(cum)).astype(BF16)
                kd = (k * jnp.exp(tot - cum)).astype(BF16)
                for p in range(npair):
                    ls = slice(p * LANE, (p + 1) * LANE)
                    q2, k2, c2 = q[:, ls], k[:, ls], cum[:, ls]
                    for s in range(c):
                        bt = (s // tile) * tile
                        t0, t1 = (bt, c) if d == 0 else (0, bt + tile)
                        e = q2[t0:t1] * k2[s:s + 1] * jnp.exp(c2[t0:t1] - c2[s:s + 1])
                        ridx = lax.broadcasted_iota(jnp.int32, (t1 - t0, LANE), 0) + t0
                        causal = (ridx >= s) if d == 0 else (ridx <= s)
                        e_ref[p * c + t0:p * c + t1, s * LANE:(s + 1) * LANE] = (
                            jnp.where(causal, e, 0.0).astype(BF16))
                pm = _dot(e_ref[...], sel_ref[...])
                st = st_ref[d]
                o_parts, u_parts = [], []
                for h in range(C_HEADS):
                    p, hh = divmod(h, pair)
                    ph = pm[p * c:(p + 1) * c, hh * c:(hh + 1) * c].astype(BF16)
                    vh = v[:, h * C_DV:(h + 1) * C_DV]
                    hs = slice(h * C_DK, (h + 1) * C_DK)
                    o_parts.append(_dot(ph, vh) + _dot_nt(qe[:, hs], st[:, hs].astype(BF16)))
                    u_parts.append(_dot_tn(vh, kd[:, hs]))
                outs[d][0, rows, :] = jnp.concatenate(o_parts, axis=1)
                st_ref[d] = st * jnp.exp(tot) + jnp.concatenate(u_parts, axis=1)
                return carry

            lax.fori_loop(0, nsub, sub, 0)


def _gla_sel():
    pair = LANE // C_DK
    sel = np.zeros((C_CHUNK, pair, C_DK, pair, C_CHUNK), np.float32)
    for s in range(C_CHUNK):
        for h in range(pair):
            sel[s, h, :, h, s] = 1.0
    return jnp.asarray(sel.reshape(C_CHUNK * LANE, pair * C_CHUNK), BF16)


def _gla(p, wdec, bdec, sel, *, t, cn):
    b, s, _ = p.shape
    n = GLA_BLOCK
    nbl, nbc = t // n, cn // n

    def fwd(j):
        return jnp.where(j < nbc, nbl + j, j - nbc)

    def bwd(j):
        return nbl + nbc - 1 - j

    def specs(order):
        return [pl.BlockSpec((1, n, sg[1]), lambda bb, j, sg=sg: (bb, order(j), sg[0] // sg[1]))
                for sg in (SEG_CQ, SEG_CK, SEG_CV, SEG_CLR)]

    width = C_HEADS * C_DV
    return pl.pallas_call(
        _gla_kernel,
        grid=(b, nbl + nbc),
        in_specs=specs(fwd) + specs(bwd) + [
            pl.BlockSpec(wdec.shape, lambda bb, j: (0, 0, 0)),
            pl.BlockSpec(bdec.shape, lambda bb, j: (0, 0, 0)),
            pl.BlockSpec(sel.shape, lambda bb, j: (0, 0)),
        ],
        out_specs=[pl.BlockSpec((1, n, width), lambda bb, j: (bb, fwd(j), 0)),
                   pl.BlockSpec((1, n, width), lambda bb, j: (bb, bwd(j), 0))],
        out_shape=[jax.ShapeDtypeStruct((b, s, width), F32)] * 2,
        scratch_shapes=[pltpu.VMEM((2, C_DV, C_HEADS * C_DK), F32),
                        pltpu.VMEM(((C_HEADS * C_DK // LANE) * C_CHUNK, C_CHUNK * LANE), BF16)],
        compiler_params=_cparams(("parallel", "arbitrary")),
        name="gla",
    )(*([p] * 8), wdec, bdec, sel)


def _merge_kernel(x_ref, mod_ref, ya_ref, yb_ref, of_ref, ob_ref, r_ref, yd_ref, g_ref,
                  gnw_ref, wba_ref, wbb_ref, wbc_ref, wbd_ref, wo_ref, o_ref, *, d):
    o = of_ref[0] + ob_ref[0]
    r = r_ref[0].astype(F32)
    yg = jnp.concatenate([_rms(o[:, h * C_DV:(h + 1) * C_DV], gnw_ref[...]) for h in range(C_HEADS)], axis=1)
    yg = (yg * (r * jax.nn.sigmoid(r))).astype(BF16)
    acc = None
    branches = ((ya_ref[0], wba_ref), (yb_ref[0], wbb_ref), (yg, wbc_ref), (yd_ref[0], wbd_ref))
    for n, (y, w_ref) in enumerate(branches):
        gate = jax.nn.sigmoid(g_ref[0, :, n * d:(n + 1) * d].astype(F32))
        term = gate * _dot(y, w_ref[...])
        acc = term if acc is None else acc + term
    out = _dot(acc.astype(BF16), wo_ref[...])
    o_ref[0] = x_ref[0] + mod_ref[0][:, 2 * d:3 * d] * out


def _merge(xs, modsel, ya, yb, o_f, o_b, p, yd, gnw, wbs, wo, *, n_blocks, nlat_blocks):
    b, s, d = xs.shape
    tm = TOKEN_TILE

    def tok(w, col=0):
        return pl.BlockSpec((1, tm, w), lambda bb, i: (bb, i, col))

    def whole(w):
        return pl.BlockSpec(w.shape, lambda bb, i: (0, 0))

    return pl.pallas_call(
        functools.partial(_merge_kernel, d=d),
        grid=(b, n_blocks),
        in_specs=[
            tok(d),
            pl.BlockSpec((1, 1, 6 * d), lambda bb, i: (bb * 2 + jnp.where(i >= nlat_blocks, 1, 0), 0, 0)),
            tok(ya.shape[2]), tok(yb.shape[2]), tok(o_f.shape[2]), tok(o_b.shape[2]),
            tok(SEG_CR[1], SEG_CR[0] // SEG_CR[1]),
            tok(yd.shape[2]),
            tok(SEG_GATE[1], 0),
            whole(gnw), *[whole(w) for w in wbs], whole(wo),
        ],
        out_specs=tok(d),
        out_shape=jax.ShapeDtypeStruct((b, n_blocks * tm, d), F32),
        compiler_params=_cparams(("parallel", "parallel")),
        name="merge",
    )(xs, modsel, ya, yb, o_f, o_b, p, yd, p, gnw, *wbs, wo)


def _mlp_kernel(x_ref, mod_ref, nw_ref, w1_ref, w2_ref, fnw_ref, o_ref, *, d, final):
    x = x_ref[0]
    mod = mod_ref[0]
    h = _rms(x, nw_ref[...]) * (1.0 + mod[:, 4 * d:5 * d]) + mod[:, 3 * d:4 * d]
    a = jnp.maximum(_dot(h.astype(BF16), w1_ref[...]), 0.0)
    y = x + mod[:, 5 * d:6 * d] * _dot((a * a).astype(BF16), w2_ref[...])
    o_ref[0] = _rms(y, fnw_ref[...]) if final else y


def _mlp(xs, modsel, nw, w1, w2, fnw, *, n_blocks, nlat_blocks, out_rows, final):
    b, s, d = xs.shape
    tm = TOKEN_TILE
    return pl.pallas_call(
        functools.partial(_mlp_kernel, d=d, final=final),
        grid=(b, n_blocks),
        in_specs=[
            pl.BlockSpec((1, tm, d), lambda bb, i: (bb, i, 0)),
            pl.BlockSpec((1, 1, 6 * d), lambda bb, i: (bb * 2 + jnp.where(i >= nlat_blocks, 1, 0), 0, 0)),
            pl.BlockSpec(nw.shape, lambda bb, i: (0, 0)),
            pl.BlockSpec(w1.shape, lambda bb, i: (0, 0)),
            pl.BlockSpec(w2.shape, lambda bb, i: (0, 0)),
            pl.BlockSpec(fnw.shape, lambda bb, i: (0, 0)),
        ],
        out_specs=pl.BlockSpec((1, tm, d), lambda bb, i: (bb, i, 0)),
        out_shape=jax.ShapeDtypeStruct((b, out_rows, d), F32),
        compiler_params=_cparams(("parallel", "parallel")),
        name="mlp",
    )(xs, modsel, nw, w1, w2, fnw)


def _pad_cols(w, n):
    return jnp.concatenate([w, jnp.zeros((w.shape[0], n), w.dtype)], axis=1)


def _layout_w_in(w):
    a0 = 0
    b0 = a0 + A_Q_RANK + A_KV_RANK + A_ROPE
    c0 = b0 + (B_HEADS + 2 * B_KV_HEADS) * B_HD
    d0 = c0 + 2 * C_HEADS * C_DK + 2 * C_HEADS * C_DV + 2 * C_DECAY_RANK
    g0 = d0 + (D_HEADS + 2 * D_KV_HEADS) * D_HD

    def cols(lo, n):
        return w[:, lo:lo + n]

    cq = c0
    ck = cq + C_HEADS * C_DK
    cv = ck + C_HEADS * C_DK
    cr = cv + C_HEADS * C_DV
    clr = cr + C_HEADS * C_DV
    parts = [
        cols(g0, SEG_GATE[1]),
        cols(d0, SEG_DQ[1] + SEG_DK[1] + SEG_DV[1]),
        cols(b0, SEG_BQ[1] + SEG_BK[1] + SEG_BV[1]),
        cols(a0, A_Q_RANK + A_KV_RANK),
        _pad_cols(cols(a0 + A_Q_RANK + A_KV_RANK, A_ROPE), SEG_AKR[1] - A_ROPE),
        cols(cq, SEG_CQ[1]), cols(cv, SEG_CV[1]), cols(cr, SEG_CR[1]), cols(ck, SEG_CK[1]),
        _pad_cols(cols(clr, 2 * C_DECAY_RANK), P_COLS - SEG_CLR[0] - 2 * C_DECAY_RANK),
    ]
    return jnp.concatenate(parts, axis=1).astype(BF16)


def _layout_mla(w_uq, w_ukv):
    zq = jnp.zeros((A_Q_RANK, A_HEAD_PAD - A_NOPE - A_ROPE), w_uq.dtype)
    hq = A_NOPE + A_ROPE
    wq = jnp.concatenate([jnp.concatenate([w_uq[:, h * hq:(h + 1) * hq], zq], axis=1)
                          for h in range(A_HEADS)], axis=1)
    hk = A_NOPE + A_V
    zk = jnp.zeros((A_KV_RANK, A_HEAD_PAD - A_NOPE), w_ukv.dtype)
    wk_top = jnp.concatenate([jnp.concatenate([w_ukv[:, h * hk:h * hk + A_NOPE], zk], axis=1)
                              for h in range(A_HEADS)], axis=1)
    place = np.zeros((SEG_AKR[1], A_HEADS * A_HEAD_PAD), np.float32)
    for h in range(A_HEADS):
        place[np.arange(A_ROPE), h * A_HEAD_PAD + A_NOPE + np.arange(A_ROPE)] = 1.0
    wk = jnp.concatenate([wk_top, jnp.asarray(place)], axis=0)
    wv = jnp.concatenate([w_ukv[:, h * hk + A_NOPE:(h + 1) * hk] for h in range(A_HEADS)], axis=1)
    lanes = np.arange(A_HEADS * A_HEAD_PAD)
    rot = ((lanes % A_HEAD_PAD) >= A_NOPE) & ((lanes % A_HEAD_PAD) < A_NOPE + A_ROPE)
    idx = _partner_lanes(lanes.size, A_ROPE // 4)

    def with_partner(w):
        return jnp.concatenate([w, jnp.where(jnp.asarray(rot)[None, :], w[:, idx], 0.0)], axis=1)

    return with_partner(wq).astype(BF16), with_partner(wk).astype(BF16), wv.astype(BF16)


def _partner_lanes(width, blk):
    j = np.arange(width)
    return np.where((j % (2 * blk)) < blk, j + blk, j - blk)


def _perm_matrix(width, blk):
    m = np.zeros((width, width), np.float32)
    m[_partner_lanes(width, blk), np.arange(width)] = 1.0
    return m


def _swa_place():
    grp = B_HEADS // B_KV_HEADS
    place = np.zeros((B_HEADS * B_HD, B_HEADS * LANE), np.float32)
    for h in range(B_HEADS):
        place[h * B_HD + np.arange(B_HD), h * LANE + (h // grp) * B_HD + np.arange(B_HD)] = 1.0
    both = np.concatenate([place, place @ _perm_matrix(B_HEADS * LANE, B_HD // 4)], axis=1)
    return jnp.asarray(both, BF16)


def _layout_w_branch_b(w):
    grp = B_HEADS // B_KV_HEADS
    z = jnp.zeros((B_HD, w.shape[1]), w.dtype)
    parts = []
    for h in range(B_HEADS):
        wh = w[h * B_HD:(h + 1) * B_HD]
        parts += [wh, z] if h // grp == 0 else [z, wh]
    return jnp.concatenate(parts, axis=0)


def _layout_decay(w_decay):
    z = jnp.zeros((C_DECAY_RANK, w_decay.shape[-1]), w_decay.dtype)
    tail = jnp.zeros((SEG_CLR[1] - 2 * C_DECAY_RANK, w_decay.shape[-1]), w_decay.dtype)
    return jnp.stack([jnp.concatenate([w_decay[0], z, tail], axis=0),
                      jnp.concatenate([z, w_decay[1], tail], axis=0)]).astype(BF16)


def kernel(x, c, ctx, c_ctx, ada_w, ada_b, norm1_w, norm2_w, w_in, mla_q_norm_w, mla_w_uq, mla_kv_norm_w, mla_w_ukv, swa_sink, gla_w_decay, gla_b_decay, gla_norm_w, gqa_q_norm_w, gqa_k_norm_w, w_branch, w_o, mlp_w1, mlp_w2, final_norm_w):
    b, t, d = x.shape
    cn = ctx.shape[1]
    depth = ada_w.shape[0]
    tm = TOKEN_TILE
    assert t % tm == 0 and cn % tm == 0 and t % GRID_W == 0 and t >= SWA_BLOCK + 2 * WINDOW
    assert t % ATTN_TQ == 0 and t % cn == 0 and t % GLA_BLOCK == 0 and cn % GLA_BLOCK == 0
    nlat, nall = t // tm, (t + cn) // tm

    rows = 16
    cs = jnp.concatenate([c, c_ctx[None], jnp.zeros((rows - b - 1, d), F32)], axis=0)
    mod = _ada(cs, ada_w, ada_b)
    modsel = jnp.stack([mod[:, :b], jnp.broadcast_to(mod[:, b:b + 1], (depth, b, 6 * d))], axis=2)
    modsel = modsel.reshape(depth, b * 2, 1, 6 * d)

    tabs = (_rope_tables(t, cn, A_HEAD_PAD, A_NOPE, A_ROPE)
            + tuple(jnp.tile(tb, (1, LANE // B_HD)) for tb in _rope_tables(t, cn, B_HD, 0, B_HD))
            + _rope_tables(t, cn, D_HD, 0, D_HD))
    sel = _gla_sel()
    perms = (_swa_place(), jnp.asarray(_perm_matrix(B_KV_HEADS * B_HD, B_HD // 4), BF16),
             jnp.asarray(_perm_matrix(D_HEADS * D_HD, D_HD // 4), BF16),
             jnp.asarray(_perm_matrix(D_KV_HEADS * D_HD, D_HD // 4), BF16))
    d_partner = _partner_lanes(D_HD, D_HD // 4)

    xs = jnp.concatenate([x, ctx], axis=1)
    for l in range(depth):
        last = l == depth - 1
        nblk = nlat if last else nall
        w_p = _layout_w_in(w_in[l])
        wq, wk, wv = _layout_mla(mla_w_uq[l], mla_w_ukv[l])
        p = _inproj(xs, modsel[l], norm1_w[l][None], w_p, nlat)
        dqn, dkn = gqa_q_norm_w[l], gqa_k_norm_w[l]
        wts = (mla_q_norm_w[l][None], mla_kv_norm_w[l][None], wq, wk, wv, *perms,
               dqn[None], dqn[d_partner][None], dkn[None], dkn[d_partner][None])
        qa, ka, va, qb, kb, qd, kd = _prep(p, tabs, wts)
        ya = _attention(qa, ka, va, heads=A_HEADS, kv_group=1, dk=A_HEAD_PAD, dv=A_V, vcol0=0,
                        t=t, cn=cn, with_ctx=not last, name="attn_a")
        yd = _attention(qd, kd, p, heads=D_HEADS, kv_group=D_HEADS // D_KV_HEADS, dk=D_HD, dv=D_HD,
                        vcol0=SEG_DV[0] // SEG_DV[1], t=t, cn=cn, with_ctx=not last, name="attn_d")
        yb = _swa(swa_sink[l], qb, kb, p, t=t, cn=cn,
                  n_blocks=(t if last else t + cn) // SWA_BLOCK)
        o_f, o_b = _gla(p, _layout_decay(gla_w_decay[l]), gla_b_decay[l][:, None, :], sel, t=t, cn=cn)
        wbs = [w_branch[l, 0], _layout_w_branch_b(w_branch[l, 1]), w_branch[l, 2], w_branch[l, 3]]
        xs = _merge(xs, modsel[l], ya, yb, o_f, o_b, p, yd, gla_norm_w[l][None],
                    [w.astype(BF16) for w in wbs], w_o[l].astype(BF16), n_blocks=nblk, nlat_blocks=nlat)
        xs = _mlp(xs, modsel[l], norm2_w[l][None], mlp_w1[l].astype(BF16), mlp_w2[l].astype(BF16),
                  final_norm_w[None], n_blocks=nblk, nlat_blocks=nlat,
                  out_rows=t if last else t + cn, final=last)
    return xs
```

```python
import functools

import numpy as np
import jax
import jax.numpy as jnp
from jax import lax
from jax.experimental import pallas as pl
from jax.experimental.pallas import tpu as pltpu

F32 = jnp.float32
BF16 = jnp.bfloat16

GRID_W = 64
EPS = 1e-6
ROPE_BASE = 10000.0
NEG_INF = -1e30

A_HEADS, A_NOPE, A_ROPE, A_V, A_Q_RANK, A_KV_RANK = 4, 128, 64, 128, 256, 128
A_HEAD_PAD = 256
B_HEADS, B_KV_HEADS, B_HD, WINDOW = 8, 2, 64, 128
SWA_BLOCK = 256
SWA_ROW_TILE = 32
C_HEADS, C_DK, C_DV, C_DECAY_RANK, C_DECAY_TEMP, C_CHUNK = 4, 64, 128, 16, 16.0, 64
D_HEADS, D_KV_HEADS, D_HD = 4, 2, 128
N_BRANCH, BRANCH_W = 4, 512

LANE = 128
TOKEN_TILE = 256
ATTN_TQ = 512
ATTN_HEADS_PER_STEP = 4
INPROJ_TN = 2048
VMEM_LIMIT = 56 * 1024 * 1024

P_COLS = 8192
SEG_GATE = (0, 4096)
SEG_DQ, SEG_DK, SEG_DV = (4096, 512), (4608, 256), (4864, 256)
SEG_BQ, SEG_BK, SEG_BV = (5120, 512), (5632, 128), (5760, 128)
SEG_ACQ, SEG_ACKV, SEG_AKR = (5888, 256), (6144, 128), (6272, 128)
SEG_CQ, SEG_CV, SEG_CR, SEG_CK, SEG_CLR = (6400, 256), (6656, 512), (7168, 512), (7680, 256), (7936, 128)


def _cparams(sem):
    return pltpu.CompilerParams(dimension_semantics=sem, vmem_limit_bytes=VMEM_LIMIT)


def _rms(xf, w):
    return xf * lax.rsqrt(jnp.mean(xf * xf, axis=-1, keepdims=True) + EPS) * w


def _dot(a, b):
    return jnp.dot(a, b, preferred_element_type=F32)


def _dot_nt(a, b):
    return lax.dot_general(a, b, (((1,), (1,)), ((), ())), preferred_element_type=F32)


def _dot_tn(a, b):
    return lax.dot_general(a, b, (((0,), (0,)), ((), ())), preferred_element_type=F32)


def _rope_tables(t, cn, width, offset, rot):
    half = rot // 2
    q = half // 2
    inv = ROPE_BASE ** (-np.arange(q, dtype=np.float64) / q)
    pos = np.arange(t)
    cos = np.ones((t + cn, width), np.float64)
    sin = np.zeros((t + cn, width), np.float64)
    for part, p in enumerate((pos // GRID_W, pos % GRID_W)):
        ang = p[:, None] * inv[None, :]
        lo = offset + part * half
        cos[:t, lo:lo + half] = np.concatenate([np.cos(ang), np.cos(ang)], axis=1)
        sin[:t, lo:lo + half] = np.concatenate([-np.sin(ang), np.sin(ang)], axis=1)
    return jnp.asarray(cos, F32), jnp.asarray(sin, F32)


def _ada_kernel(cs_ref, w_ref, b_ref, o_ref):
    cs = cs_ref[...]
    s = cs * jax.nn.sigmoid(cs)
    o_ref[0] = _dot(s.astype(BF16), w_ref[0].astype(BF16)) + b_ref[0]


def _ada(cs, ada_w, ada_b):
    nl, d, n = ada_w.shape
    tn = 1536
    rows = cs.shape[0]
    return pl.pallas_call(
        _ada_kernel,
        grid=(nl, n // tn),
        in_specs=[
            pl.BlockSpec((rows, d), lambda l, j: (0, 0)),
            pl.BlockSpec((1, d, tn), lambda l, j: (l, 0, j)),
            pl.BlockSpec((1, 1, tn), lambda l, j: (l, 0, j)),
        ],
        out_specs=pl.BlockSpec((1, rows, tn), lambda l, j: (l, 0, j)),
        out_shape=jax.ShapeDtypeStruct((nl, rows, n), F32),
        compiler_params=_cparams(("parallel", "parallel")),
        name="ada",
    )(cs, ada_w, ada_b.reshape(nl, 1, n))


def _inproj_kernel(x_ref, mod_ref, nw_ref, w_ref, o_ref, *, d):
    mod = mod_ref[0]
    h = (_rms(x_ref[0], nw_ref[...]) * (1.0 + mod[:, d:2 * d]) + mod[:, 0:d]).astype(BF16)
    for j in range(P_COLS // INPROJ_TN):
        cs = slice(j * INPROJ_TN, (j + 1) * INPROJ_TN)
        o_ref[0, :, cs] = _dot(h, w_ref[:, cs]).astype(BF16)


def _resident(shape):
    return pl.BlockSpec(shape, lambda *_: (0,) * len(shape), pipeline_mode=pl.Buffered(1))


def _layer_resident(arr, *lead):
    rest = arr.shape[len(lead):]
    return pl.BlockSpec((None,) * len(lead) + rest, lambda *_: tuple(lead) + (0,) * len(rest),
                        pipeline_mode=pl.Buffered(1))


def _inproj(xs, modsel, nw, w_p, l, nlat_blocks):
    b, s, d = xs.shape
    tm = TOKEN_TILE
    return pl.pallas_call(
        functools.partial(_inproj_kernel, d=d),
        grid=(b, s // tm),
        in_specs=[
            pl.BlockSpec((1, tm, d), lambda bb, i: (bb, i, 0)),
            pl.BlockSpec((1, 1, 6 * d), lambda bb, i: (bb * 2 + jnp.where(i >= nlat_blocks, 1, 0), 0, 0)),
            _resident(nw.shape),
            _layer_resident(w_p, l),
        ],
        out_specs=pl.BlockSpec((1, tm, P_COLS), lambda bb, i: (bb, i, 0)),
        out_shape=jax.ShapeDtypeStruct((b, s, P_COLS), BF16),
        compiler_params=_cparams(("parallel", "parallel")),
        name="inproj",
    )(xs, modsel, nw, w_p)


def _prep_kernel(acq_ref, ackv_ref, akr_ref, bq_ref, bk_ref, dq_ref, dk_ref,
                 cosa_ref, sina_ref, cosb_ref, sinb_ref, cosd_ref, sind_ref,
                 aqn_ref, akvn_ref, wq_ref, wk_ref, wv_ref, place_ref, permbk_ref,
                 permdq_ref, permdk_ref, dqn_ref, dqnp_ref, dkn_ref, dknp_ref,
                 qa_ref, ka_ref, va_ref, qb_ref, kb_ref, qd_ref, kd_ref):
    na = A_HEADS * A_HEAD_PAD
    cosa = jnp.concatenate([cosa_ref[...]] * A_HEADS, axis=1)
    sina = jnp.concatenate([sina_ref[...]] * A_HEADS, axis=1)
    cq = _rms(acq_ref[0].astype(F32), aqn_ref[...]).astype(BF16)
    q2 = _dot(cq, wq_ref[...])
    qa_ref[0] = ((q2[:, :na] * cosa + q2[:, na:] * sina) * ((A_NOPE + A_ROPE) ** -0.5)).astype(BF16)
    ckv = _rms(ackv_ref[0].astype(F32), akvn_ref[...]).astype(BF16)
    k2 = _dot(jnp.concatenate([ckv, akr_ref[0]], axis=1), wk_ref[...])
    ka_ref[0] = (k2[:, :na] * cosa + k2[:, na:] * sina).astype(BF16)
    va_ref[0] = _dot(ckv, wv_ref[...]).astype(BF16)

    nb = B_HEADS * LANE
    cosb, sinb = cosb_ref[...], sinb_ref[...]
    b2 = _dot(bq_ref[0], place_ref[...])
    qb = (b2[:, :nb] * jnp.concatenate([cosb] * B_HEADS, axis=1)
          + b2[:, nb:] * jnp.concatenate([sinb] * B_HEADS, axis=1))
    qb_ref[0] = (qb * (B_HD ** -0.5)).astype(BF16)
    bk = bk_ref[0]
    kb_ref[0] = (bk.astype(F32) * cosb + _dot(bk, permbk_ref[...]) * sinb).astype(BF16)

    cosd, sind = cosd_ref[...], sind_ref[...]

    def norm_rope(x_ref, perm_ref, nw_ref, nwp_ref, heads):
        x = x_ref[0]
        xp = _dot(x, perm_ref[...])
        xf = x.astype(F32)
        wc, ws = nw_ref[...] * cosd, nwp_ref[...] * sind
        parts = []
        for h in range(heads):
            hs = slice(h * D_HD, (h + 1) * D_HD)
            xh = xf[:, hs]
            inv = lax.rsqrt(jnp.mean(xh * xh, axis=-1, keepdims=True) + EPS)
            parts.append(inv * (xh * wc + xp[:, hs] * ws))
        return jnp.concatenate(parts, axis=1)

    qd_ref[0] = (norm_rope(dq_ref, permdq_ref, dqn_ref, dqnp_ref, D_HEADS) * (D_HD ** -0.5)).astype(BF16)
    kd_ref[0] = norm_rope(dk_ref, permdk_ref, dkn_ref, dknp_ref, D_KV_HEADS).astype(BF16)


def _pspec(seg, tm):
    off, width = seg
    return pl.BlockSpec((1, tm, width), lambda bb, i: (bb, i, off // width))


def _prep(p, tabs, wts):
    b, s, _ = p.shape
    tm = TOKEN_TILE
    segs = [SEG_ACQ, SEG_ACKV, SEG_AKR, SEG_BQ, SEG_BK, SEG_DQ, SEG_DK]
    tab_specs = [pl.BlockSpec((tm, t.shape[1]), lambda bb, i: (i, 0)) for t in tabs]
    wt_specs = [pl.BlockSpec(w.shape, lambda bb, i: (0, 0)) for w in wts]
    widths = [A_HEADS * A_HEAD_PAD, A_HEADS * A_HEAD_PAD, A_HEADS * A_V,
              B_HEADS * LANE, B_KV_HEADS * B_HD, D_HEADS * D_HD, D_KV_HEADS * D_HD]
    return pl.pallas_call(
        _prep_kernel,
        grid=(b, s // tm),
        in_specs=[_pspec(sg, tm) for sg in segs] + tab_specs + wt_specs,
        out_specs=[pl.BlockSpec((1, tm, w), lambda bb, i: (bb, i, 0)) for w in widths],
        out_shape=[jax.ShapeDtypeStruct((b, s, w), BF16) for w in widths],
        compiler_params=_cparams(("parallel", "parallel")),
        name="prep",
    )(*([p] * len(segs)), *tabs, *wts)


def _attn_kernel(q_ref, k_ref, v_ref, *rest, dk, dv, kv_group):
    o_ref = rest[-1]
    for c in range(ATTN_HEADS_PER_STEP):
        kc = c // kv_group
        s = _dot_nt(q_ref[0, :, c * dk:(c + 1) * dk], k_ref[0, :, kc * dk:(kc + 1) * dk])
        m = jnp.max(s, axis=-1, keepdims=True)
        p = jnp.exp(s - m)
        l = jnp.sum(p, axis=-1, keepdims=True)
        o = _dot(p.astype(BF16), v_ref[0, :, kc * dv:(kc + 1) * dv])
        o_ref[0, :, c * dv:(c + 1) * dv] = (o / l).astype(o_ref.dtype)


def _attention(q, k, v, *, heads, kv_group, dk, dv, vcol0, t, cn, with_ctx, name):
    b, s, _ = q.shape
    hps = ATTN_HEADS_PER_STEP
    nkv = hps // kv_group
    kern = functools.partial(_attn_kernel, dk=dk, dv=dv, kv_group=kv_group)
    sem = _cparams(("parallel", "parallel", "parallel"))
    out_rows = s if with_ctx else t

    def specs(tq, row0, krows, krow0):
        return dict(
            grid=(b, heads // hps, (t if row0 == 0 else cn) // tq),
            in_specs=[
                pl.BlockSpec((1, tq, hps * dk), lambda bb, h, i: (bb, row0 + i, h)),
                pl.BlockSpec((1, krows, nkv * dk), lambda bb, h, i: (bb, krow0, h)),
                pl.BlockSpec((1, krows, nkv * dv), lambda bb, h, i: (bb, krow0, vcol0 + h)),
            ],
            out_specs=pl.BlockSpec((1, tq, hps * dv), lambda bb, h, i: (bb, row0 + i, h)),
            out_shape=jax.ShapeDtypeStruct((b, out_rows, heads * dv), BF16),
            compiler_params=sem,
        )

    y = pl.pallas_call(kern, name=name, **specs(ATTN_TQ, 0, s, 0))(q, k, v)
    if with_ctx:
        sp = specs(cn, t // cn, cn, t // cn)
        sp["in_specs"] = sp["in_specs"] + [pl.BlockSpec(memory_space=pl.ANY)]
        y = pl.pallas_call(kern, name=name + "_ctx", input_output_aliases={3: 0}, **sp)(q, k, v, y)
    return y


def _swa_bias():
    band = SWA_BLOCK + 2 * WINDOW
    r = np.arange(SWA_BLOCK)[:, None]
    j = np.arange(band)[None, :]
    out = []
    for off in (0, WINDOW, band - SWA_BLOCK):
        out.append(np.where(np.abs(off + r - j) <= WINDOW, 0.0, NEG_INF))
    return np.stack(out).astype(np.float32)


def _swa_kernel(sink_ref, q_ref, k_ref, v_ref, bias_ref, o_ref, s_ref, p_ref, e_ref, *, t, cn):
    n = pl.program_id(1)
    blk = SWA_BLOCK
    nlat = t // blk
    grp = B_HEADS // B_KV_HEADS
    band = blk + 2 * WINDOW
    rt = SWA_ROW_TILE

    def run(is_ctx):
        q = q_ref[0]
        k, v = k_ref[0, t:t + cn, :], v_ref[0, t:t + cn, :]
        if not is_ctx:
            start = pl.multiple_of(jnp.clip(n * blk - WINDOW, 0, t - band), WINDOW)
            k = jnp.concatenate([k, k_ref[0, pl.ds(start, band), :]], axis=0)
            v = jnp.concatenate([v, v_ref[0, pl.ds(start, band), :]], axis=0)
        nk = k.shape[0]
        halves = []
        for hk in range(B_KV_HEADS):
            q4 = jnp.concatenate(
                [q[:, (hk * grp + gi) * LANE:(hk * grp + gi + 1) * LANE] for gi in range(grp)], axis=0)
            s_ref[hk, :, :nk] = _dot_nt(q4, k)

            for i in range(grp * blk // rt):
                rs = slice(i * rt, (i + 1) * rt)
                sink = sink_ref[hk * grp + i // (blk // rt)]
                s = s_ref[hk, rs, :nk]
                if not is_ctx:
                    br = (i % (blk // rt)) * rt
                    s = s + bias_ref[0, br:br + rt, :]
                m = jnp.maximum(jnp.max(s, axis=-1, keepdims=True), sink)
                p = jnp.exp(s - m)
                l = jnp.sum(p, axis=-1, keepdims=True) + jnp.exp(sink - m)
                p_ref[hk, rs, :nk] = p.astype(BF16)
                e_ref[hk, rs, :] = jnp.broadcast_to(l, (rt, LANE))
            halves.append(_dot(p_ref[hk, :, :nk], v) / e_ref[hk])
        lane = lax.broadcasted_iota(jnp.int32, (blk, LANE), 1)
        outs = [jnp.where(lane < B_HD, halves[0][gi * blk:(gi + 1) * blk], halves[1][gi * blk:(gi + 1) * blk])
                for gi in range(grp)]
        o_ref[0] = jnp.concatenate(outs, axis=1).astype(o_ref.dtype)

    @pl.when(n < nlat)
    def _():
        run(False)

    @pl.when(n >= nlat)
    def _():
        run(True)


def _swa(sink, q, k, p, *, t, cn, n_blocks):
    b, s, _ = q.shape
    nlat = t // SWA_BLOCK
    rows = (B_HEADS // B_KV_HEADS) * SWA_BLOCK
    band_bias = _swa_bias()
    bias = jnp.asarray(np.concatenate([np.zeros(band_bias.shape[:2] + (cn,), np.float32), band_bias], axis=2))
    return pl.pallas_call(
        functools.partial(_swa_kernel, t=t, cn=cn),
        grid=(b, n_blocks),
        in_specs=[
            pl.BlockSpec(memory_space=pltpu.SMEM),
            pl.BlockSpec((1, SWA_BLOCK, B_HEADS * LANE), lambda bb, n: (bb, n, 0)),
            pl.BlockSpec((1, s, B_KV_HEADS * B_HD), lambda bb, n: (bb, 0, 0)),
            pl.BlockSpec((1, s, SEG_BV[1]), lambda bb, n: (bb, 0, SEG_BV[0] // SEG_BV[1])),
            pl.BlockSpec((1,) + bias.shape[1:],
                         lambda bb, n: (jnp.where(n == 0, 0, jnp.where(n >= nlat - 1, 2, 1)), 0, 0)),
        ],
        out_specs=pl.BlockSpec((1, SWA_BLOCK, B_HEADS * B_HD), lambda bb, n: (bb, n, 0)),
        out_shape=jax.ShapeDtypeStruct((b, n_blocks * SWA_BLOCK, B_HEADS * B_HD), BF16),
        scratch_shapes=[pltpu.VMEM((B_KV_HEADS, rows, bias.shape[2]), F32),
                        pltpu.VMEM((B_KV_HEADS, rows, bias.shape[2]), BF16),
                        pltpu.VMEM((B_KV_HEADS, rows, LANE), F32)],
        compiler_params=_cparams(("parallel", "parallel")),
        name="swa",
    )(sink, q, k, p, bias)


GLA_BLOCK = 256
GLA_SAFE_DECAY = 60.0


def _log_decay(lr, w, bias):
    z = _dot(lr, w) + bias
    return (jnp.minimum(z, 0.0) - jnp.log(1.0 + jnp.exp(-jnp.abs(z)))) / C_DECAY_TEMP


def _cum_decay(la, tri):
    hi = la.astype(BF16)
    lo = (la - hi.astype(F32)).astype(BF16)
    return _dot(tri, hi) + _dot(tri, lo)


def _gla_kernel(qf_ref, kf_ref, vf_ref, lrf_ref, qb_ref, kb_ref, vb_ref, lrb_ref,
                wdec_ref, bdec_ref, sel_ref, of_ref, ob_ref, st_ref, e_ref):
    n = GLA_BLOCK
    pair = LANE // C_DK
    npair = C_HEADS // pair
    qscale = C_DK ** -0.5
    refs = ((qf_ref, kf_ref, vf_ref, lrf_ref), (qb_ref, kb_ref, vb_ref, lrb_ref))
    outs = (of_ref, ob_ref)

    @pl.when(pl.program_id(1) == 0)
    def _():
        st_ref[...] = jnp.zeros_like(st_ref)

    r_i = lax.broadcasted_iota(jnp.int32, (n, n), 0)
    c_i = lax.broadcasted_iota(jnp.int32, (n, n), 1)
    keep = (c_i <= r_i, c_i >= r_i)
    lane = lax.broadcasted_iota(jnp.int32, (1, LANE), 1)
    head_mask = [(lane // C_DK) == hh for hh in range(pair)]

    cums, tots = [], []
    for d in range(2):
        la = _log_decay(refs[d][3][0], wdec_ref[d], bdec_ref[d])
        cum = _cum_decay(la, keep[d].astype(BF16))
        cums.append(cum)
        tots.append(cum[n - 1:n] if d == 0 else cum[0:1])
    safe = jnp.maximum(jnp.max(-tots[0]), jnp.max(-tots[1])) <= GLA_SAFE_DECAY

    @pl.when(safe)
    def _():
        for d in range(2):
            q_ref, k_ref, v_ref, _ = refs[d]
            cum, tot = cums[d], tots[d]
            half = 0.5 * tot
            a = cum - half
            eh = jnp.exp(half)
            qr = q_ref[0].astype(F32) * qscale * jnp.exp(a)
            kr = k_ref[0].astype(F32) * jnp.exp(-a)
            kr_b, kd_b = kr.astype(BF16), (kr * eh).astype(BF16)
            qe = qr * eh
            v, st = v_ref[0], st_ref[d]
            o_parts, u_parts = [], []
            for p in range(npair):
                ls = slice(p * LANE, (p + 1) * LANE)
                st2 = st[:, ls].astype(BF16)
                upd = []
                for hh in range(pair):
                    h = p * pair + hh
                    vh = v[:, h * C_DV:(h + 1) * C_DV]
                    qr_h = jnp.where(head_mask[hh], qr[:, ls], 0.0).astype(BF16)
                    qe_h = jnp.where(head_mask[hh], qe[:, ls], 0.0).astype(BF16)
                    pm = jnp.where(keep[d], _dot_nt(qr_h, kr_b[:, ls]), 0.0).astype(BF16)
                    o_parts.append(_dot(pm, vh) + _dot_nt(qe_h, st2))
                    upd.append(_dot_tn(vh, kd_b[:, ls]))
                u_parts.append(jnp.where(head_mask[0], upd[0], upd[1]))
            outs[d][0] = jnp.concatenate(o_parts, axis=1)
            st_ref[d] = st * jnp.exp(tot) + jnp.concatenate(u_parts, axis=1)

    @pl.when(jnp.logical_not(safe))
    def _():
        c = C_CHUNK
        tile = 16
        nsub = n // c
        r64 = lax.broadcasted_iota(jnp.int32, (c, c), 0)
        c64 = lax.broadcasted_iota(jnp.int32, (c, c), 1)
        tri64 = ((c64 <= r64).astype(BF16), (c64 >= r64).astype(BF16))
        for d in range(2):
            q_ref, k_ref, v_ref, lr_ref = refs[d]
            e_ref[...] = jnp.zeros_like(e_ref)

            def sub(i, carry, d=d, q_ref=q_ref, k_ref=k_ref, v_ref=v_ref, lr_ref=lr_ref):
                rows = pl.ds(pl.multiple_of((i if d == 0 else nsub - 1 - i) * c, c), c)
                q = q_ref[0, rows, :].astype(F32) * qscale
                k = k_ref[0, rows, :].astype(F32)
                v = v_ref[0, rows, :]
                cum = _cum_decay(_log_decay(lr_ref[0, rows, :], wdec_ref[d], bdec_ref[d]), tri64[d])
                tot = cum[c - 1:c] if d == 0 else cum[0:1]
                qe = (q * jnp.exp(cum)).astype(BF16)
                kd = (k * jnp.exp(tot - cum)).astype(BF16)
                for p in range(npair):
                    ls = slice(p * LANE, (p + 1) * LANE)
                    q2, k2, c2 = q[:, ls], k[:, ls], cum[:, ls]
                    for s in range(c):
                        bt = (s // tile) * tile
                        t0, t1 = (bt, c) if d == 0 else (0, bt + tile)
                        e = q2[t0:t1] * k2[s:s + 1] * jnp.exp(c2[t0:t1] - c2[s:s + 1])
                        ridx = lax.broadcasted_iota(jnp.int32, (t1 - t0, LANE), 0) + t0
                        causal = (ridx >= s) if d == 0 else (ridx <= s)
                        e_ref[p * c + t0:p * c + t1, s * LANE:(s + 1) * LANE] = (
                            jnp.where(causal, e, 0.0).astype(BF16))
                pm = _dot(e_ref[...], sel_ref[...])
                st = st_ref[d]
                o_parts, u_parts = [], []
                for h in range(C_HEADS):
                    p, hh = divmod(h, pair)
                    ph = pm[p * c:(p + 1) * c, hh * c:(hh + 1) * c].astype(BF16)
                    vh = v[:, h * C_DV:(h + 1) * C_DV]
                    hs = slice(h * C_DK, (h + 1) * C_DK)
                    o_parts.append(_dot(ph, vh) + _dot_nt(qe[:, hs], st[:, hs].astype(BF16)))
                    u_parts.append(_dot_tn(vh, kd[:, hs]))
                outs[d][0, rows, :] = jnp.concatenate(o_parts, axis=1)
                st_ref[d] = st * jnp.exp(tot) + jnp.concatenate(u_parts, axis=1)
                return carry

            lax.fori_loop(0, nsub, sub, 0)


def _gla_sel():
    pair = LANE // C_DK
    sel = np.zeros((C_CHUNK, pair, C_DK, pair, C_CHUNK), np.float32)
    for s in range(C_CHUNK):
        for h in range(pair):
            sel[s, h, :, h, s] = 1.0
    return jnp.asarray(sel.reshape(C_CHUNK * LANE, pair * C_CHUNK), BF16)


def _gla(p, wdec, bdec, sel, *, t, cn):
    b, s, _ = p.shape
    n = GLA_BLOCK
    nbl, nbc = t // n, cn // n

    def fwd(j):
        return jnp.where(j < nbc, nbl + j, j - nbc)

    def bwd(j):
        return nbl + nbc - 1 - j

    def specs(order):
        return [pl.BlockSpec((1, n, sg[1]), lambda bb, j, sg=sg: (bb, order(j), sg[0] // sg[1]))
                for sg in (SEG_CQ, SEG_CK, SEG_CV, SEG_CLR)]

    width = C_HEADS * C_DV
    return pl.pallas_call(
        _gla_kernel,
        grid=(b, nbl + nbc),
        in_specs=specs(fwd) + specs(bwd) + [
            pl.BlockSpec(wdec.shape, lambda bb, j: (0, 0, 0)),
            pl.BlockSpec(bdec.shape, lambda bb, j: (0, 0, 0)),
            pl.BlockSpec(sel.shape, lambda bb, j: (0, 0)),
        ],
        out_specs=[pl.BlockSpec((1, n, width), lambda bb, j: (bb, fwd(j), 0)),
                   pl.BlockSpec((1, n, width), lambda bb, j: (bb, bwd(j), 0))],
        out_shape=[jax.ShapeDtypeStruct((b, s, width), F32)] * 2,
        scratch_shapes=[pltpu.VMEM((2, C_DV, C_HEADS * C_DK), F32),
                        pltpu.VMEM(((C_HEADS * C_DK // LANE) * C_CHUNK, C_CHUNK * LANE), BF16)],
        compiler_params=_cparams(("parallel", "arbitrary")),
        name="gla",
    )(*([p] * 8), wdec, bdec, sel)


def _merge_kernel(x_ref, mod_ref, ya_ref, yb_ref, of_ref, ob_ref, r_ref, yd_ref, g_ref,
                  gnw_ref, wba_ref, wbb_ref, wbc_ref, wbd_ref, wo_ref, o_ref, *, d):
    o = of_ref[0] + ob_ref[0]
    r = r_ref[0].astype(F32)
    yg = jnp.concatenate([_rms(o[:, h * C_DV:(h + 1) * C_DV], gnw_ref[...]) for h in range(C_HEADS)], axis=1)
    yg = (yg * (r * jax.nn.sigmoid(r))).astype(BF16)
    acc = None
    branches = ((ya_ref[0], wba_ref), (yb_ref[0], wbb_ref), (yg, wbc_ref), (yd_ref[0], wbd_ref))
    for n, (y, w_ref) in enumerate(branches):
        gate = jax.nn.sigmoid(g_ref[0, :, n * d:(n + 1) * d].astype(F32))
        term = gate * _dot(y, w_ref[...])
        acc = term if acc is None else acc + term
    out = _dot(acc.astype(BF16), wo_ref[...])
    o_ref[0] = x_ref[0] + mod_ref[0][:, 2 * d:3 * d] * out


def _merge(xs, modsel, ya, yb, o_f, o_b, p, yd, gnw, w_branch, wb_b, wo, l, *, n_blocks, nlat_blocks):
    b, s, d = xs.shape
    tm = TOKEN_TILE

    def tok(w, col=0):
        return pl.BlockSpec((1, tm, w), lambda bb, i: (bb, i, col))

    w_specs = [_layer_resident(w_branch, l, 0), _layer_resident(wb_b, l),
               _layer_resident(w_branch, l, 2), _layer_resident(w_branch, l, 3), _layer_resident(wo, l)]
    return pl.pallas_call(
        functools.partial(_merge_kernel, d=d),
        grid=(b, n_blocks),
        in_specs=[
            tok(d),
            pl.BlockSpec((1, 1, 6 * d), lambda bb, i: (bb * 2 + jnp.where(i >= nlat_blocks, 1, 0), 0, 0)),
            tok(ya.shape[2]), tok(yb.shape[2]), tok(o_f.shape[2]), tok(o_b.shape[2]),
            tok(SEG_CR[1], SEG_CR[0] // SEG_CR[1]),
            tok(yd.shape[2]),
            tok(SEG_GATE[1], 0),
            _resident(gnw.shape), *w_specs,
        ],
        out_specs=tok(d),
        out_shape=jax.ShapeDtypeStruct((b, n_blocks * tm, d), F32),
        compiler_params=_cparams(("parallel", "parallel")),
        name="merge",
    )(xs, modsel, ya, yb, o_f, o_b, p, yd, p, gnw, w_branch, wb_b, w_branch, w_branch, wo)


def _mlp_kernel(x_ref, mod_ref, nw_ref, w1_ref, w2_ref, fnw_ref, o_ref, *, d, final):
    x = x_ref[0]
    mod = mod_ref[0]
    h = _rms(x, nw_ref[...]) * (1.0 + mod[:, 4 * d:5 * d]) + mod[:, 3 * d:4 * d]
    a = jnp.maximum(_dot(h.astype(BF16), w1_ref[...]), 0.0)
    y = x + mod[:, 5 * d:6 * d] * _dot((a * a).astype(BF16), w2_ref[...])
    o_ref[0] = _rms(y, fnw_ref[...]) if final else y


def _mlp(xs, modsel, nw, w1, w2, fnw, l, *, n_blocks, nlat_blocks, out_rows, final):
    b, s, d = xs.shape
    tm = TOKEN_TILE
    return pl.pallas_call(
        functools.partial(_mlp_kernel, d=d, final=final),
        grid=(b, n_blocks),
        in_specs=[
            pl.BlockSpec((1, tm, d), lambda bb, i: (bb, i, 0)),
            pl.BlockSpec((1, 1, 6 * d), lambda bb, i: (bb * 2 + jnp.where(i >= nlat_blocks, 1, 0), 0, 0)),
            _resident(nw.shape), _layer_resident(w1, l), _layer_resident(w2, l), _resident(fnw.shape),
        ],
        out_specs=pl.BlockSpec((1, tm, d), lambda bb, i: (bb, i, 0)),
        out_shape=jax.ShapeDtypeStruct((b, out_rows, d), F32),
        compiler_params=_cparams(("parallel", "parallel")),
        name="mlp",
    )(xs, modsel, nw, w1, w2, fnw)


def _pad_cols(w, n):
    return jnp.concatenate([w, jnp.zeros(w.shape[:-1] + (n,), w.dtype)], axis=-1)


def _layout_w_in(w):
    a0 = 0
    b0 = a0 + A_Q_RANK + A_KV_RANK + A_ROPE
    c0 = b0 + (B_HEADS + 2 * B_KV_HEADS) * B_HD
    d0 = c0 + 2 * C_HEADS * C_DK + 2 * C_HEADS * C_DV + 2 * C_DECAY_RANK
    g0 = d0 + (D_HEADS + 2 * D_KV_HEADS) * D_HD

    def cols(lo, n):
        return w[..., lo:lo + n]

    cq = c0
    ck = cq + C_HEADS * C_DK
    cv = ck + C_HEADS * C_DK
    cr = cv + C_HEADS * C_DV
    clr = cr + C_HEADS * C_DV
    parts = [
        cols(g0, SEG_GATE[1]),
        cols(d0, SEG_DQ[1] + SEG_DK[1] + SEG_DV[1]),
        cols(b0, SEG_BQ[1] + SEG_BK[1] + SEG_BV[1]),
        cols(a0, A_Q_RANK + A_KV_RANK),
        _pad_cols(cols(a0 + A_Q_RANK + A_KV_RANK, A_ROPE), SEG_AKR[1] - A_ROPE),
        cols(cq, SEG_CQ[1]), cols(cv, SEG_CV[1]), cols(cr, SEG_CR[1]), cols(ck, SEG_CK[1]),
        _pad_cols(cols(clr, 2 * C_DECAY_RANK), P_COLS - SEG_CLR[0] - 2 * C_DECAY_RANK),
    ]
    return jnp.concatenate(parts, axis=-1).astype(BF16)


def _layout_mla(w_uq, w_ukv):
    zq = jnp.zeros((A_Q_RANK, A_HEAD_PAD - A_NOPE - A_ROPE), w_uq.dtype)
    hq = A_NOPE + A_ROPE
    wq = jnp.concatenate([jnp.concatenate([w_uq[:, h * hq:(h + 1) * hq], zq], axis=1)
                          for h in range(A_HEADS)], axis=1)
    hk = A_NOPE + A_V
    zk = jnp.zeros((A_KV_RANK, A_HEAD_PAD - A_NOPE), w_ukv.dtype)
    wk_top = jnp.concatenate([jnp.concatenate([w_ukv[:, h * hk:h * hk + A_NOPE], zk], axis=1)
                              for h in range(A_HEADS)], axis=1)
    place = np.zeros((SEG_AKR[1], A_HEADS * A_HEAD_PAD), np.float32)
    for h in range(A_HEADS):
        place[np.arange(A_ROPE), h * A_HEAD_PAD + A_NOPE + np.arange(A_ROPE)] = 1.0
    wk = jnp.concatenate([wk_top, jnp.asarray(place)], axis=0)
    wv = jnp.concatenate([w_ukv[:, h * hk + A_NOPE:(h + 1) * hk] for h in range(A_HEADS)], axis=1)
    lanes = np.arange(A_HEADS * A_HEAD_PAD)
    rot = ((lanes % A_HEAD_PAD) >= A_NOPE) & ((lanes % A_HEAD_PAD) < A_NOPE + A_ROPE)
    idx = _partner_lanes(lanes.size, A_ROPE // 4)

    def with_partner(w):
        return jnp.concatenate([w, jnp.where(jnp.asarray(rot)[None, :], w[:, idx], 0.0)], axis=1)

    return with_partner(wq).astype(BF16), with_partner(wk).astype(BF16), wv.astype(BF16)


def _partner_lanes(width, blk):
    j = np.arange(width)
    return np.where((j % (2 * blk)) < blk, j + blk, j - blk)


def _perm_matrix(width, blk):
    m = np.zeros((width, width), np.float32)
    m[_partner_lanes(width, blk), np.arange(width)] = 1.0
    return m


def _swa_place():
    grp = B_HEADS // B_KV_HEADS
    place = np.zeros((B_HEADS * B_HD, B_HEADS * LANE), np.float32)
    for h in range(B_HEADS):
        place[h * B_HD + np.arange(B_HD), h * LANE + (h // grp) * B_HD + np.arange(B_HD)] = 1.0
    both = np.concatenate([place, place @ _perm_matrix(B_HEADS * LANE, B_HD // 4)], axis=1)
    return jnp.asarray(both, BF16)


def _layout_w_branch_b(w):
    grp = B_HEADS // B_KV_HEADS
    order = [h for g in range(grp) for h in (g, grp + g)]
    return jnp.concatenate([w[h * B_HD:(h + 1) * B_HD] for h in order], axis=0)


def _layout_decay(w_decay):
    z = jnp.zeros((C_DECAY_RANK, w_decay.shape[-1]), w_decay.dtype)
    tail = jnp.zeros((SEG_CLR[1] - 2 * C_DECAY_RANK, w_decay.shape[-1]), w_decay.dtype)
    return jnp.stack([jnp.concatenate([w_decay[0], z, tail], axis=0),
                      jnp.concatenate([z, w_decay[1], tail], axis=0)]).astype(BF16)


def kernel(x, c, ctx, c_ctx, ada_w, ada_b, norm1_w, norm2_w, w_in, mla_q_norm_w, mla_w_uq, mla_kv_norm_w, mla_w_ukv, swa_sink, gla_w_decay, gla_b_decay, gla_norm_w, gqa_q_norm_w, gqa_k_norm_w, w_branch, w_o, mlp_w1, mlp_w2, final_norm_w):
    b, t, d = x.shape
    cn = ctx.shape[1]
    depth = ada_w.shape[0]
    tm = TOKEN_TILE
    assert t % tm == 0 and cn % tm == 0 and t % GRID_W == 0 and t >= SWA_BLOCK + 2 * WINDOW
    assert t % ATTN_TQ == 0 and t % cn == 0 and t % GLA_BLOCK == 0 and cn % GLA_BLOCK == 0
    nlat, nall = t // tm, (t + cn) // tm

    rows = 16
    cs = jnp.concatenate([c, c_ctx[None], jnp.zeros((rows - b - 1, d), F32)], axis=0)
    mod = _ada(cs, ada_w, ada_b)
    modsel = jnp.stack([mod[:, :b], jnp.broadcast_to(mod[:, b:b + 1], (depth, b, 6 * d))], axis=2)
    modsel = modsel.reshape(depth, b * 2, 1, 6 * d)

    tabs = (_rope_tables(t, cn, A_HEAD_PAD, A_NOPE, A_ROPE)
            + tuple(jnp.tile(tb, (1, LANE // B_HD)) for tb in _rope_tables(t, cn, B_HD, 0, B_HD))
            + _rope_tables(t, cn, D_HD, 0, D_HD))
    sel = _gla_sel()
    perms = (_swa_place(), jnp.asarray(_perm_matrix(B_KV_HEADS * B_HD, B_HD // 4), BF16),
             jnp.asarray(_perm_matrix(D_HEADS * D_HD, D_HD // 4), BF16),
             jnp.asarray(_perm_matrix(D_KV_HEADS * D_HD, D_HD // 4), BF16))
    d_partner = _partner_lanes(D_HD, D_HD // 4)

    w_p = _layout_w_in(w_in)
    wbr, wo = w_branch.astype(BF16), w_o.astype(BF16)
    wb_b = jnp.stack([_layout_w_branch_b(w_branch[l, 1]) for l in range(depth)]).astype(BF16)
    w1, w2 = mlp_w1.astype(BF16), mlp_w2.astype(BF16)

    xs = jnp.concatenate([x, ctx], axis=1)
    for l in range(depth):
        last = l == depth - 1
        nblk = nlat if last else nall
        wq, wk, wv = _layout_mla(mla_w_uq[l], mla_w_ukv[l])
        p = _inproj(xs, modsel[l], norm1_w[l][None], w_p, l, nlat)
        dqn, dkn = gqa_q_norm_w[l], gqa_k_norm_w[l]
        wts = (mla_q_norm_w[l][None], mla_kv_norm_w[l][None], wq, wk, wv, *perms,
               dqn[None], dqn[d_partner][None], dkn[None], dkn[d_partner][None])
        qa, ka, va, qb, kb, qd, kd = _prep(p, tabs, wts)
        ya = _attention(qa, ka, va, heads=A_HEADS, kv_group=1, dk=A_HEAD_PAD, dv=A_V, vcol0=0,
                        t=t, cn=cn, with_ctx=not last, name="attn_a")
        yd = _attention(qd, kd, p, heads=D_HEADS, kv_group=D_HEADS // D_KV_HEADS, dk=D_HD, dv=D_HD,
                        vcol0=SEG_DV[0] // SEG_DV[1], t=t, cn=cn, with_ctx=not last, name="attn_d")
        yb = _swa(swa_sink[l], qb, kb, p, t=t, cn=cn,
                  n_blocks=(t if last else t + cn) // SWA_BLOCK)
        o_f, o_b = _gla(p, _layout_decay(gla_w_decay[l]), gla_b_decay[l][:, None, :], sel, t=t, cn=cn)
        xs = _merge(xs, modsel[l], ya, yb, o_f, o_b, p, yd, gla_norm_w[l][None],
                    wbr, wb_b, wo, l, n_blocks=nblk, nlat_blocks=nlat)
        xs = _mlp(xs, modsel[l], norm2_w[l][None], w1, w2, final_norm_w[None], l,
                  n_blocks=nblk, nlat_blocks=nlat, out_rows=t if last else t + cn, final=last)
    return xs
```

```python
import functools

import numpy as np
import jax
import jax.numpy as jnp
from jax import lax
from jax.experimental import pallas as pl
from jax.experimental.pallas import tpu as pltpu

F32 = jnp.float32
BF16 = jnp.bfloat16

GRID_W = 64
EPS = 1e-6
ROPE_BASE = 10000.0
NEG_INF = -1e30

A_HEADS, A_NOPE, A_ROPE, A_V, A_Q_RANK, A_KV_RANK = 4, 128, 64, 128, 256, 128
A_HEAD_PAD = 256
B_HEADS, B_KV_HEADS, B_HD, WINDOW = 8, 2, 64, 128
SWA_BLOCK = 256
SWA_ROW_TILE = 32
C_HEADS, C_DK, C_DV, C_DECAY_RANK, C_DECAY_TEMP, C_CHUNK = 4, 64, 128, 16, 16.0, 64
D_HEADS, D_KV_HEADS, D_HD = 4, 2, 128
N_BRANCH, BRANCH_W = 4, 512

LANE = 128
TOKEN_TILE = 256
ATTN_TQ = 1024
ATTN_CHAIN_ROWS = 512
ATTN_HEADS_PER_STEP = 4
INPROJ_TN = 2048
VMEM_LIMIT = 56 * 1024 * 1024

P_COLS = 8192
SEG_GATE = (0, 4096)
SEG_DQ, SEG_DK, SEG_DV = (4096, 512), (4608, 256), (4864, 256)
SEG_BQ, SEG_BK, SEG_BV = (5120, 512), (5632, 128), (5760, 128)
SEG_ACQ, SEG_ACKV, SEG_AKR = (5888, 256), (6144, 128), (6272, 128)
SEG_CQ, SEG_CV, SEG_CR, SEG_CK, SEG_CLR = (6400, 256), (6656, 512), (7168, 512), (7680, 256), (7936, 128)


def _cparams(sem):
    return pltpu.CompilerParams(dimension_semantics=sem, vmem_limit_bytes=VMEM_LIMIT)


def _rms(xf, w):
    return xf * lax.rsqrt(jnp.mean(xf * xf, axis=-1, keepdims=True) + EPS) * w


def _dot(a, b):
    return jnp.dot(a, b, preferred_element_type=F32)


def _dot_nt(a, b):
    return lax.dot_general(a, b, (((1,), (1,)), ((), ())), preferred_element_type=F32)


def _dot_tn(a, b):
    return lax.dot_general(a, b, (((0,), (0,)), ((), ())), preferred_element_type=F32)


def _rope_tables(t, cn, width, offset, rot):
    half = rot // 2
    q = half // 2
    inv = ROPE_BASE ** (-np.arange(q, dtype=np.float64) / q)
    pos = np.arange(t)
    cos = np.ones((t + cn, width), np.float64)
    sin = np.zeros((t + cn, width), np.float64)
    for part, p in enumerate((pos // GRID_W, pos % GRID_W)):
        ang = p[:, None] * inv[None, :]
        lo = offset + part * half
        cos[:t, lo:lo + half] = np.concatenate([np.cos(ang), np.cos(ang)], axis=1)
        sin[:t, lo:lo + half] = np.concatenate([-np.sin(ang), np.sin(ang)], axis=1)
    return jnp.asarray(cos, F32), jnp.asarray(sin, F32)


def _ada_kernel(cs_ref, w_ref, b_ref, o_ref):
    cs = cs_ref[...]
    s = cs * jax.nn.sigmoid(cs)
    o_ref[0] = _dot(s.astype(BF16), w_ref[0].astype(BF16)) + b_ref[0]


def _ada(cs, ada_w, ada_b):
    nl, d, n = ada_w.shape
    tn = 1536
    rows = cs.shape[0]
    return pl.pallas_call(
        _ada_kernel,
        grid=(nl, n // tn),
        in_specs=[
            pl.BlockSpec((rows, d), lambda l, j: (0, 0)),
            pl.BlockSpec((1, d, tn), lambda l, j: (l, 0, j)),
            pl.BlockSpec((1, 1, tn), lambda l, j: (l, 0, j)),
        ],
        out_specs=pl.BlockSpec((1, rows, tn), lambda l, j: (l, 0, j)),
        out_shape=jax.ShapeDtypeStruct((nl, rows, n), F32),
        compiler_params=_cparams(("parallel", "parallel")),
        name="ada",
    )(cs, ada_w, ada_b.reshape(nl, 1, n))


def _token_specs(xs, tm, nlat_blocks):
    if not isinstance(xs, tuple):
        return [pl.BlockSpec((1, tm, xs.shape[2]), lambda bb, i: (bb, i, 0))]
    x, ctx = xs
    return [pl.BlockSpec((1, tm, x.shape[2]), lambda bb, i: (bb, jnp.minimum(i, nlat_blocks - 1), 0)),
            pl.BlockSpec((1, tm, ctx.shape[2]), lambda bb, i: (bb, jnp.maximum(i - nlat_blocks, 0), 0))]


def _token_rows(x_refs, nlat_blocks):
    if len(x_refs) == 1:
        return x_refs[0][0]
    return jnp.where(pl.program_id(1) >= nlat_blocks, x_refs[1][0], x_refs[0][0])


def _inproj_kernel(*refs, d, nsrc, nlat_blocks):
    mod_ref, nw_ref, w_ref, o_ref = refs[nsrc:]
    mod = mod_ref[0]
    x = _token_rows(refs[:nsrc], nlat_blocks)
    h = (_rms(x, nw_ref[...]) * (1.0 + mod[:, d:2 * d]) + mod[:, 0:d]).astype(BF16)
    for j in range(P_COLS // INPROJ_TN):
        cs = slice(j * INPROJ_TN, (j + 1) * INPROJ_TN)
        o_ref[0, :, cs] = _dot(h, w_ref[:, cs]).astype(BF16)


def _resident(shape):
    return pl.BlockSpec(shape, lambda *_: (0,) * len(shape), pipeline_mode=pl.Buffered(1))


def _layer_resident(arr, *lead):
    rest = arr.shape[len(lead):]
    return pl.BlockSpec((None,) * len(lead) + rest, lambda *_: tuple(lead) + (0,) * len(rest),
                        pipeline_mode=pl.Buffered(1))


def _mod_spec(modsel, l, nlat_blocks):
    return pl.BlockSpec((None, 1) + modsel.shape[2:],
                        lambda bb, i: (l, bb * 2 + jnp.where(i >= nlat_blocks, 1, 0), 0, 0))


def _inproj(xs, modsel, nw, w_p, l, nlat_blocks):
    srcs = xs if isinstance(xs, tuple) else (xs,)
    b, d = srcs[0].shape[0], srcs[0].shape[2]
    s = sum(a.shape[1] for a in srcs)
    tm = TOKEN_TILE
    return pl.pallas_call(
        functools.partial(_inproj_kernel, d=d, nsrc=len(srcs), nlat_blocks=nlat_blocks),
        grid=(b, s // tm),
        in_specs=_token_specs(xs, tm, nlat_blocks) + [
            _mod_spec(modsel, l, nlat_blocks),
            _layer_resident(nw, l),
            _layer_resident(w_p, l),
        ],
        out_specs=pl.BlockSpec((1, tm, P_COLS), lambda bb, i: (bb, i, 0)),
        out_shape=jax.ShapeDtypeStruct((b, s, P_COLS), BF16),
        compiler_params=_cparams(("parallel", "parallel")),
        name="inproj",
    )(*srcs, modsel, nw, w_p)


def _prep_kernel(acq_ref, ackv_ref, akr_ref, bq_ref, bk_ref, dq_ref, dk_ref,
                 cosa_ref, sina_ref, cosb_ref, sinb_ref, cosd_ref, sind_ref,
                 aqn_ref, akvn_ref, wq_ref, wk_ref, wv_ref, permbq_ref, permbk_ref,
                 permdq_ref, permdk_ref, dqn_ref, dqnp_ref, dkn_ref, dknp_ref,
                 qa_ref, ka_ref, va_ref, qb_ref, kb_ref, qd_ref, kd_ref):
    na = A_HEADS * A_HEAD_PAD
    cosa = jnp.concatenate([cosa_ref[...]] * A_HEADS, axis=1)
    sina = jnp.concatenate([sina_ref[...]] * A_HEADS, axis=1)
    cq = _rms(acq_ref[0].astype(F32), aqn_ref[...]).astype(BF16)
    q2 = _dot(cq, wq_ref[...])
    qa_ref[0] = ((q2[:, :na] * cosa + q2[:, na:] * sina) * ((A_NOPE + A_ROPE) ** -0.5)).astype(BF16)
    ckv = _rms(ackv_ref[0].astype(F32), akvn_ref[...]).astype(BF16)
    k2 = _dot(jnp.concatenate([ckv, akr_ref[0]], axis=1), wk_ref[...])
    ka_ref[0] = (k2[:, :na] * cosa + k2[:, na:] * sina).astype(BF16)
    va_ref[0] = _dot(ckv, wv_ref[...]).astype(BF16)

    nrep = B_HEADS * B_HD // LANE
    cosb, sinb = cosb_ref[...], sinb_ref[...]
    bq = bq_ref[0]
    qb = (bq.astype(F32) * jnp.concatenate([cosb] * nrep, axis=1)
          + _dot(bq, permbq_ref[...]) * jnp.concatenate([sinb] * nrep, axis=1))
    qb_ref[0] = (qb * (B_HD ** -0.5)).astype(BF16)
    bk = bk_ref[0]
    kb_ref[0] = (bk.astype(F32) * cosb + _dot(bk, permbk_ref[...]) * sinb).astype(BF16)

    cosd, sind = cosd_ref[...], sind_ref[...]

    def norm_rope(x_ref, perm_ref, nw_ref, nwp_ref, heads):
        x = x_ref[0]
        xp = _dot(x, perm_ref[...])
        xf = x.astype(F32)
        wc, ws = nw_ref[...] * cosd, nwp_ref[...] * sind
        parts = []
        for h in range(heads):
            hs = slice(h * D_HD, (h + 1) * D_HD)
            xh = xf[:, hs]
            inv = lax.rsqrt(jnp.mean(xh * xh, axis=-1, keepdims=True) + EPS)
            parts.append(inv * (xh * wc + xp[:, hs] * ws))
        return jnp.concatenate(parts, axis=1)

    qd_ref[0] = (norm_rope(dq_ref, permdq_ref, dqn_ref, dqnp_ref, D_HEADS) * (D_HD ** -0.5)).astype(BF16)
    kd_ref[0] = norm_rope(dk_ref, permdk_ref, dkn_ref, dknp_ref, D_KV_HEADS).astype(BF16)


def _pspec(seg, tm):
    off, width = seg
    return pl.BlockSpec((1, tm, width), lambda bb, i: (bb, i, off // width))


def _prep(p, tabs, wts, l):
    b, s, _ = p.shape
    tm = TOKEN_TILE
    segs = [SEG_ACQ, SEG_ACKV, SEG_AKR, SEG_BQ, SEG_BK, SEG_DQ, SEG_DK]
    tab_specs = [pl.BlockSpec((tm, t.shape[1]), lambda bb, i: (i, 0)) for t in tabs]
    wt_specs = [_layer_resident(w, l) if stacked else _resident(w.shape) for w, stacked in wts]
    wts = [w for w, _ in wts]
    widths = [A_HEADS * A_HEAD_PAD, A_HEADS * A_HEAD_PAD, A_HEADS * A_V,
              B_HEADS * B_HD, B_KV_HEADS * B_HD, D_HEADS * D_HD, D_KV_HEADS * D_HD]
    return pl.pallas_call(
        _prep_kernel,
        grid=(b, s // tm),
        in_specs=[_pspec(sg, tm) for sg in segs] + tab_specs + wt_specs,
        out_specs=[pl.BlockSpec((1, tm, w), lambda bb, i: (bb, i, 0)) for w in widths],
        out_shape=[jax.ShapeDtypeStruct((b, s, w), BF16) for w in widths],
        compiler_params=_cparams(("parallel", "parallel")),
        name="prep",
    )(*([p] * len(segs)), *tabs, *wts)


def _attn_kernel(q_ref, k_ref, v_ref, *rest, dk, dv, kv_group):
    o_ref = rest[-1]
    rows = min(ATTN_CHAIN_ROWS, q_ref.shape[1])
    for r in range(q_ref.shape[1] // rows):
        rs = slice(r * rows, (r + 1) * rows)
        for c in range(ATTN_HEADS_PER_STEP):
            kc = c // kv_group
            s = _dot_nt(q_ref[0, rs, c * dk:(c + 1) * dk], k_ref[0, :, kc * dk:(kc + 1) * dk])
            m = jnp.max(s, axis=-1, keepdims=True)
            p = jnp.exp(s - m)
            l = jnp.sum(p, axis=-1, keepdims=True)
            o = _dot(p.astype(BF16), v_ref[0, :, kc * dv:(kc + 1) * dv])
            o_ref[0, rs, c * dv:(c + 1) * dv] = (o / l).astype(o_ref.dtype)


def _attention(q, k, v, *, heads, kv_group, dk, dv, vcol0, t, cn, with_ctx, name):
    b, s, _ = q.shape
    hps = ATTN_HEADS_PER_STEP
    nkv = hps // kv_group
    kern = functools.partial(_attn_kernel, dk=dk, dv=dv, kv_group=kv_group)
    sem = _cparams(("parallel", "parallel", "parallel"))
    out_rows = s if with_ctx else t

    def specs(tq, row0, krows, krow0):
        return dict(
            grid=(b, heads // hps, (t if row0 == 0 else cn) // tq),
            in_specs=[
                pl.BlockSpec((1, tq, hps * dk), lambda bb, h, i: (bb, row0 + i, h)),
                pl.BlockSpec((1, krows, nkv * dk), lambda bb, h, i: (bb, krow0, h)),
                pl.BlockSpec((1, krows, nkv * dv), lambda bb, h, i: (bb, krow0, vcol0 + h)),
            ],
            out_specs=pl.BlockSpec((1, tq, hps * dv), lambda bb, h, i: (bb, row0 + i, h)),
            out_shape=jax.ShapeDtypeStruct((b, out_rows, heads * dv), BF16),
            compiler_params=sem,
        )

    y = pl.pallas_call(kern, name=name, **specs(ATTN_TQ, 0, s, 0))(q, k, v)
    if with_ctx:
        sp = specs(cn, t // cn, cn, t // cn)
        sp["in_specs"] = sp["in_specs"] + [pl.BlockSpec(memory_space=pl.ANY)]
        y = pl.pallas_call(kern, name=name + "_ctx", input_output_aliases={3: 0}, **sp)(q, k, v, y)
    return y


def _swa_bias():
    band = SWA_BLOCK + 2 * WINDOW
    r = np.arange(SWA_BLOCK)[:, None]
    j = np.arange(band)[None, :]
    out = []
    for off in (0, WINDOW, band - SWA_BLOCK):
        out.append(np.where(np.abs(off + r - j) <= WINDOW, 0.0, NEG_INF))
    return np.stack(out).astype(np.float32)


def _swa_kernel(sink_ref, q_ref, k_ref, v_ref, bias_ref, o_ref, s_ref, p_ref, e_ref, *, t, cn, layer):
    n = pl.program_id(1)
    blk = SWA_BLOCK
    nlat = t // blk
    grp = B_HEADS // B_KV_HEADS
    band = blk + 2 * WINDOW
    rt = SWA_ROW_TILE

    def run(is_ctx):
        q = q_ref[0]
        k, v = k_ref[0, t:t + cn, :], v_ref[0, t:t + cn, :]
        if not is_ctx:
            start = pl.multiple_of(jnp.clip(n * blk - WINDOW, 0, t - band), WINDOW)
            k = jnp.concatenate([k, k_ref[0, pl.ds(start, band), :]], axis=0)
            v = jnp.concatenate([v, v_ref[0, pl.ds(start, band), :]], axis=0)
        nk = k.shape[0]
        halves = []
        qlane = lax.broadcasted_iota(jnp.int32, (blk, LANE), 1)
        for hk in range(B_KV_HEADS):
            mine = ((qlane // B_HD) == hk).astype(BF16)
            q4 = jnp.concatenate([q[:, gi * LANE:(gi + 1) * LANE] * mine for gi in range(grp)], axis=0)
            s_ref[hk, :, :nk] = _dot_nt(q4, k)

            for i in range(grp * blk // rt):
                rs = slice(i * rt, (i + 1) * rt)
                sink = sink_ref[layer, hk * grp + i // (blk // rt)]
                s = s_ref[hk, rs, :nk]
                if not is_ctx:
                    br = (i % (blk // rt)) * rt
                    s = s + bias_ref[0, br:br + rt, :]
                m = jnp.maximum(jnp.max(s, axis=-1, keepdims=True), sink)
                p = jnp.exp(s - m)
                l = jnp.sum(p, axis=-1, keepdims=True) + jnp.exp(sink - m)
                p_ref[hk, rs, :nk] = p.astype(BF16)
                e_ref[hk, rs, :] = jnp.broadcast_to(l, (rt, LANE))
            halves.append(_dot(p_ref[hk, :, :nk], v) / e_ref[hk])
        lane = lax.broadcasted_iota(jnp.int32, (blk, LANE), 1)
        outs = [jnp.where(lane < B_HD, halves[0][gi * blk:(gi + 1) * blk], halves[1][gi * blk:(gi + 1) * blk])
                for gi in range(grp)]
        o_ref[0] = jnp.concatenate(outs, axis=1).astype(o_ref.dtype)

    @pl.when(n < nlat)
    def _():
        run(False)

    @pl.when(n >= nlat)
    def _():
        run(True)


def _swa(sink, q, k, p, l, *, t, cn, n_blocks):
    b, s, _ = q.shape
    nlat = t // SWA_BLOCK
    rows = (B_HEADS // B_KV_HEADS) * SWA_BLOCK
    band_bias = _swa_bias()
    bias = jnp.asarray(np.concatenate([np.zeros(band_bias.shape[:2] + (cn,), np.float32), band_bias], axis=2))
    return pl.pallas_call(
        functools.partial(_swa_kernel, t=t, cn=cn, layer=l),
        grid=(b, n_blocks),
        in_specs=[
            pl.BlockSpec(memory_space=pltpu.SMEM),
            pl.BlockSpec((1, SWA_BLOCK, B_HEADS * B_HD), lambda bb, n: (bb, n, 0)),
            pl.BlockSpec((1, s, B_KV_HEADS * B_HD), lambda bb, n: (bb, 0, 0)),
            pl.BlockSpec((1, s, SEG_BV[1]), lambda bb, n: (bb, 0, SEG_BV[0] // SEG_BV[1])),
            pl.BlockSpec((1,) + bias.shape[1:],
                         lambda bb, n: (jnp.where(n == 0, 0, jnp.where(n >= nlat - 1, 2, 1)), 0, 0)),
        ],
        out_specs=pl.BlockSpec((1, SWA_BLOCK, B_HEADS * B_HD), lambda bb, n: (bb, n, 0)),
        out_shape=jax.ShapeDtypeStruct((b, n_blocks * SWA_BLOCK, B_HEADS * B_HD), BF16),
        scratch_shapes=[pltpu.VMEM((B_KV_HEADS, rows, bias.shape[2]), F32),
                        pltpu.VMEM((B_KV_HEADS, rows, bias.shape[2]), BF16),
                        pltpu.VMEM((B_KV_HEADS, rows, LANE), F32)],
        compiler_params=_cparams(("parallel", "parallel")),
        name="swa",
    )(sink, q, k, p, bias)


GLA_BLOCK = 256
GLA_SAFE_DECAY = 60.0


def _log_decay(lr, w, bias):
    z = _dot(lr, w) + bias
    return (jnp.minimum(z, 0.0) - jnp.log(1.0 + jnp.exp(-jnp.abs(z)))) / C_DECAY_TEMP


def _cum_decay(la, tri):
    hi = la.astype(BF16)
    lo = (la - hi.astype(F32)).astype(BF16)
    return _dot(tri, hi) + _dot(tri, lo)


def _gla_kernel(qf_ref, kf_ref, vf_ref, lrf_ref, qb_ref, kb_ref, vb_ref, lrb_ref,
                wdec_ref, bdec_ref, sel_ref, of_ref, ob_ref, st_ref, e_ref):
    n = GLA_BLOCK
    pair = LANE // C_DK
    npair = C_HEADS // pair
    qscale = C_DK ** -0.5
    refs = ((qf_ref, kf_ref, vf_ref, lrf_ref), (qb_ref, kb_ref, vb_ref, lrb_ref))
    outs = (of_ref, ob_ref)

    @pl.when(pl.program_id(1) == 0)
    def _():
        st_ref[...] = jnp.zeros_like(st_ref)

    r_i = lax.broadcasted_iota(jnp.int32, (n, n), 0)
    c_i = lax.broadcasted_iota(jnp.int32, (n, n), 1)
    keep = (c_i <= r_i, c_i >= r_i)
    lane = lax.broadcasted_iota(jnp.int32, (1, LANE), 1)
    head_mask = [(lane // C_DK) == hh for hh in range(pair)]

    cums, tots = [], []
    for d in range(2):
        la = _log_decay(refs[d][3][0], wdec_ref[d], bdec_ref[d])
        cum = _cum_decay(la, keep[d].astype(BF16))
        cums.append(cum)
        tots.append(cum[n - 1:n] if d == 0 else cum[0:1])
    safe = jnp.maximum(jnp.max(-tots[0]), jnp.max(-tots[1])) <= GLA_SAFE_DECAY

    @pl.when(safe)
    def _():
        for d in range(2):
            q_ref, k_ref, v_ref, _ = refs[d]
            cum, tot = cums[d], tots[d]
            half = 0.5 * tot
            a = cum - half
            eh = jnp.exp(half)
            qr = q_ref[0].astype(F32) * qscale * jnp.exp(a)
            kr = k_ref[0].astype(F32) * jnp.exp(-a)
            kr_b, kd_b = kr.astype(BF16), (kr * eh).astype(BF16)
            qe = qr * eh
            v, st = v_ref[0], st_ref[d]
            o_parts, u_parts = [], []
            for p in range(npair):
                ls = slice(p * LANE, (p + 1) * LANE)
                st2 = st[:, ls].astype(BF16)
                upd = []
                for hh in range(pair):
                    h = p * pair + hh
                    vh = v[:, h * C_DV:(h + 1) * C_DV]
                    qr_h = jnp.where(head_mask[hh], qr[:, ls], 0.0).astype(BF16)
                    qe_h = jnp.where(head_mask[hh], qe[:, ls], 0.0).astype(BF16)
                    pm = jnp.where(keep[d], _dot_nt(qr_h, kr_b[:, ls]), 0.0).astype(BF16)
                    o_parts.append(_dot(pm, vh) + _dot_nt(qe_h, st2))
                    upd.append(_dot_tn(vh, kd_b[:, ls]))
                u_parts.append(jnp.where(head_mask[0], upd[0], upd[1]))
            outs[d][0] = jnp.concatenate(o_parts, axis=1)
            st_ref[d] = st * jnp.exp(tot) + jnp.concatenate(u_parts, axis=1)

    @pl.when(jnp.logical_not(safe))
    def _():
        c = C_CHUNK
        tile = 16
        nsub = n // c
        r64 = lax.broadcasted_iota(jnp.int32, (c, c), 0)
        c64 = lax.broadcasted_iota(jnp.int32, (c, c), 1)
        tri64 = ((c64 <= r64).astype(BF16), (c64 >= r64).astype(BF16))
        for d in range(2):
            q_ref, k_ref, v_ref, lr_ref = refs[d]
            e_ref[...] = jnp.zeros_like(e_ref)

            def sub(i, carry, d=d, q_ref=q_ref, k_ref=k_ref, v_ref=v_ref, lr_ref=lr_ref):
                rows = pl.ds(pl.multiple_of((i if d == 0 else nsub - 1 - i) * c, c), c)
                q = q_ref[0, rows, :].astype(F32) * qscale
                k = k_ref[0, rows, :].astype(F32)
                v = v_ref[0, rows, :]
                cum = _cum_decay(_log_decay(lr_ref[0, rows, :], wdec_ref[d], bdec_ref[d]), tri64[d])
                tot = cum[c - 1:c] if d == 0 else cum[0:1]
                qe = (q * jnp.exp(cum)).astype(BF16)
                kd = (k * jnp.exp(tot - cum)).astype(BF16)
                for p in range(npair):
                    ls = slice(p * LANE, (p + 1) * LANE)
                    q2, k2, c2 = q[:, ls], k[:, ls], cum[:, ls]
                    for s in range(c):
                        bt = (s // tile) * tile
                        t0, t1 = (bt, c) if d == 0 else (0, bt + tile)
                        e = q2[t0:t1] * k2[s:s + 1] * jnp.exp(c2[t0:t1] - c2[s:s + 1])
                        ridx = lax.broadcasted_iota(jnp.int32, (t1 - t0, LANE), 0) + t0
                        causal = (ridx >= s) if d == 0 else (ridx <= s)
                        e_ref[p * c + t0:p * c + t1, s * LANE:(s + 1) * LANE] = (
                            jnp.where(causal, e, 0.0).astype(BF16))
                pm = _dot(e_ref[...], sel_ref[...])
                st = st_ref[d]
                o_parts, u_parts = [], []
                for h in range(C_HEADS):
                    p, hh = divmod(h, pair)
                    ph = pm[p * c:(p + 1) * c, hh * c:(hh + 1) * c].astype(BF16)
                    vh = v[:, h * C_DV:(h + 1) * C_DV]
                    hs = slice(h * C_DK, (h + 1) * C_DK)
                    o_parts.append(_dot(ph, vh) + _dot_nt(qe[:, hs], st[:, hs].astype(BF16)))
                    u_parts.append(_dot_tn(vh, kd[:, hs]))
                outs[d][0, rows, :] = jnp.concatenate(o_parts, axis=1)
                st_ref[d] = st * jnp.exp(tot) + jnp.concatenate(u_parts, axis=1)
                return carry

            lax.fori_loop(0, nsub, sub, 0)


def _gla_sel():
    pair = LANE // C_DK
    sel = np.zeros((C_CHUNK, pair, C_DK, pair, C_CHUNK), np.float32)
    for s in range(C_CHUNK):
        for h in range(pair):
            sel[s, h, :, h, s] = 1.0
    return jnp.asarray(sel.reshape(C_CHUNK * LANE, pair * C_CHUNK), BF16)


def _gla(p, wdec, bdec, sel, l, *, t, cn):
    b, s, _ = p.shape
    n = GLA_BLOCK
    nbl, nbc = t // n, cn // n

    def fwd(j):
        return jnp.where(j < nbc, nbl + j, j - nbc)

    def bwd(j):
        return nbl + nbc - 1 - j

    def specs(order):
        return [pl.BlockSpec((1, n, sg[1]), lambda bb, j, sg=sg: (bb, order(j), sg[0] // sg[1]))
                for sg in (SEG_CQ, SEG_CK, SEG_CV, SEG_CLR)]

    width = C_HEADS * C_DV
    return pl.pallas_call(
        _gla_kernel,
        grid=(b, nbl + nbc),
        in_specs=specs(fwd) + specs(bwd) + [
            _layer_resident(wdec, l), _layer_resident(bdec, l), _resident(sel.shape),
        ],
        out_specs=[pl.BlockSpec((1, n, width), lambda bb, j: (bb, fwd(j), 0)),
                   pl.BlockSpec((1, n, width), lambda bb, j: (bb, bwd(j), 0))],
        out_shape=[jax.ShapeDtypeStruct((b, s, width), F32)] * 2,
        scratch_shapes=[pltpu.VMEM((2, C_DV, C_HEADS * C_DK), F32),
                        pltpu.VMEM(((C_HEADS * C_DK // LANE) * C_CHUNK, C_CHUNK * LANE), BF16)],
        compiler_params=_cparams(("parallel", "arbitrary")),
        name="gla",
    )(*([p] * 8), wdec, bdec, sel)


def _merge_kernel(*refs, d, nsrc, nlat_blocks):
    (mod_ref, ya_ref, yb_ref, of_ref, ob_ref, r_ref, yd_ref, g_ref,
     gnw_ref, wba_ref, wbb_ref, wbc_ref, wbd_ref, wo_ref, o_ref) = refs[nsrc:]
    o = of_ref[0] + ob_ref[0]
    r = r_ref[0].astype(F32)
    yg = jnp.concatenate([_rms(o[:, h * C_DV:(h + 1) * C_DV], gnw_ref[...]) for h in range(C_HEADS)], axis=1)
    yg = (yg * (r * jax.nn.sigmoid(r))).astype(BF16)
    acc = None
    branches = ((ya_ref[0], wba_ref), (yb_ref[0], wbb_ref), (yg, wbc_ref), (yd_ref[0], wbd_ref))
    for n, (y, w_ref) in enumerate(branches):
        gate = jax.nn.sigmoid(g_ref[0, :, n * d:(n + 1) * d].astype(F32))
        term = gate * _dot(y, w_ref[...])
        acc = term if acc is None else acc + term
    out = _dot(acc.astype(BF16), wo_ref[...])
    o_ref[0] = _token_rows(refs[:nsrc], nlat_blocks) + mod_ref[0][:, 2 * d:3 * d] * out


def _merge(xs, modsel, ya, yb, o_f, o_b, p, yd, gnw, w_branch, wb_b, wo, l, *, n_blocks, nlat_blocks):
    srcs = xs if isinstance(xs, tuple) else (xs,)
    b, d = srcs[0].shape[0], srcs[0].shape[2]
    tm = TOKEN_TILE

    def tok(w, col=0):
        return pl.BlockSpec((1, tm, w), lambda bb, i: (bb, i, col))

    w_specs = [_layer_resident(w_branch, l, 0), _layer_resident(wb_b, l),
               _layer_resident(w_branch, l, 2), _layer_resident(w_branch, l, 3), _layer_resident(wo, l)]
    return pl.pallas_call(
        functools.partial(_merge_kernel, d=d, nsrc=len(srcs), nlat_blocks=nlat_blocks),
        grid=(b, n_blocks),
        in_specs=_token_specs(xs, tm, nlat_blocks) + [
            _mod_spec(modsel, l, nlat_blocks),
            tok(ya.shape[2]), tok(yb.shape[2]), tok(o_f.shape[2]), tok(o_b.shape[2]),
            tok(SEG_CR[1], SEG_CR[0] // SEG_CR[1]),
            tok(yd.shape[2]),
            tok(SEG_GATE[1], 0),
            _layer_resident(gnw, l), *w_specs,
        ],
        out_specs=tok(d),
        out_shape=jax.ShapeDtypeStruct((b, n_blocks * tm, d), F32),
        compiler_params=_cparams(("parallel", "parallel")),
        name="merge",
    )(*srcs, modsel, ya, yb, o_f, o_b, p, yd, p, gnw, w_branch, wb_b, w_branch, w_branch, wo)


def _mlp_kernel(x_ref, mod_ref, nw_ref, w1_ref, w2_ref, fnw_ref, o_ref, *, d, final):
    x = x_ref[0]
    mod = mod_ref[0]
    h = _rms(x, nw_ref[...]) * (1.0 + mod[:, 4 * d:5 * d]) + mod[:, 3 * d:4 * d]
    a = jnp.maximum(_dot(h.astype(BF16), w1_ref[...]), 0.0)
    y = x + mod[:, 5 * d:6 * d] * _dot((a * a).astype(BF16), w2_ref[...])
    o_ref[0] = _rms(y, fnw_ref[...]) if final else y


def _mlp(xs, modsel, nw, w1, w2, fnw, l, *, n_blocks, nlat_blocks, out_rows, final):
    b, s, d = xs.shape
    tm = TOKEN_TILE
    return pl.pallas_call(
        functools.partial(_mlp_kernel, d=d, final=final),
        grid=(b, n_blocks),
        in_specs=[
            pl.BlockSpec((1, tm, d), lambda bb, i: (bb, i, 0)),
            _mod_spec(modsel, l, nlat_blocks),
            _layer_resident(nw, l), _layer_resident(w1, l), _layer_resident(w2, l), _resident(fnw.shape),
        ],
        out_specs=pl.BlockSpec((1, tm, d), lambda bb, i: (bb, i, 0)),
        out_shape=jax.ShapeDtypeStruct((b, out_rows, d), F32),
        compiler_params=_cparams(("parallel", "parallel")),
        name="mlp",
    )(xs, modsel, nw, w1, w2, fnw)


def _pad_cols(w, n):
    return jnp.concatenate([w, jnp.zeros(w.shape[:-1] + (n,), w.dtype)], axis=-1)


def _layout_w_in(w):
    a0 = 0
    b0 = a0 + A_Q_RANK + A_KV_RANK + A_ROPE
    c0 = b0 + (B_HEADS + 2 * B_KV_HEADS) * B_HD
    d0 = c0 + 2 * C_HEADS * C_DK + 2 * C_HEADS * C_DV + 2 * C_DECAY_RANK
    g0 = d0 + (D_HEADS + 2 * D_KV_HEADS) * D_HD

    def cols(lo, n):
        return w[..., lo:lo + n]

    cq = c0
    ck = cq + C_HEADS * C_DK
    cv = ck + C_HEADS * C_DK
    cr = cv + C_HEADS * C_DV
    clr = cr + C_HEADS * C_DV
    parts = [
        cols(g0, SEG_GATE[1]),
        cols(d0, SEG_DQ[1] + SEG_DK[1] + SEG_DV[1]),
        *[cols(b0 + h * B_HD, B_HD) for h in _swa_head_order()],
        cols(b0 + SEG_BQ[1], SEG_BK[1] + SEG_BV[1]),
        cols(a0, A_Q_RANK + A_KV_RANK),
        _pad_cols(cols(a0 + A_Q_RANK + A_KV_RANK, A_ROPE), SEG_AKR[1] - A_ROPE),
        cols(cq, SEG_CQ[1]), cols(cv, SEG_CV[1]), cols(cr, SEG_CR[1]), cols(ck, SEG_CK[1]),
        _pad_cols(cols(clr, 2 * C_DECAY_RANK), P_COLS - SEG_CLR[0] - 2 * C_DECAY_RANK),
    ]
    return jnp.concatenate(parts, axis=-1).astype(BF16)


def _layout_mla(w_uq, w_ukv):
    zq = jnp.zeros((A_Q_RANK, A_HEAD_PAD - A_NOPE - A_ROPE), w_uq.dtype)
    hq = A_NOPE + A_ROPE
    wq = jnp.concatenate([jnp.concatenate([w_uq[:, h * hq:(h + 1) * hq], zq], axis=1)
                          for h in range(A_HEADS)], axis=1)
    hk = A_NOPE + A_V
    zk = jnp.zeros((A_KV_RANK, A_HEAD_PAD - A_NOPE), w_ukv.dtype)
    wk_top = jnp.concatenate([jnp.concatenate([w_ukv[:, h * hk:h * hk + A_NOPE], zk], axis=1)
                              for h in range(A_HEADS)], axis=1)
    place = np.zeros((SEG_AKR[1], A_HEADS * A_HEAD_PAD), np.float32)
    for h in range(A_HEADS):
        place[np.arange(A_ROPE), h * A_HEAD_PAD + A_NOPE + np.arange(A_ROPE)] = 1.0
    wk = jnp.concatenate([wk_top, jnp.asarray(place)], axis=0)
    wv = jnp.concatenate([w_ukv[:, h * hk + A_NOPE:(h + 1) * hk] for h in range(A_HEADS)], axis=1)
    lanes = np.arange(A_HEADS * A_HEAD_PAD)
    rot = ((lanes % A_HEAD_PAD) >= A_NOPE) & ((lanes % A_HEAD_PAD) < A_NOPE + A_ROPE)
    idx = _partner_lanes(lanes.size, A_ROPE // 4)

    def with_partner(w):
        return jnp.concatenate([w, jnp.where(jnp.asarray(rot)[None, :], w[:, idx], 0.0)], axis=1)

    return with_partner(wq).astype(BF16), with_partner(wk).astype(BF16), wv.astype(BF16)


def _partner_lanes(width, blk):
    j = np.arange(width)
    return np.where((j % (2 * blk)) < blk, j + blk, j - blk)


def _perm_matrix(width, blk):
    m = np.zeros((width, width), np.float32)
    m[_partner_lanes(width, blk), np.arange(width)] = 1.0
    return m


def _swa_head_order():
    grp = B_HEADS // B_KV_HEADS
    return [h for g in range(grp) for h in (g, grp + g)]


def _layout_w_branch_b(w):
    return jnp.concatenate([w[h * B_HD:(h + 1) * B_HD] for h in _swa_head_order()], axis=0)


def _layout_decay(w_decay):
    z = jnp.zeros((C_DECAY_RANK, w_decay.shape[-1]), w_decay.dtype)
    tail = jnp.zeros((SEG_CLR[1] - 2 * C_DECAY_RANK, w_decay.shape[-1]), w_decay.dtype)
    return jnp.stack([jnp.concatenate([w_decay[0], z, tail], axis=0),
                      jnp.concatenate([z, w_decay[1], tail], axis=0)]).astype(BF16)


def kernel(x, c, ctx, c_ctx, ada_w, ada_b, norm1_w, norm2_w, w_in, mla_q_norm_w, mla_w_uq, mla_kv_norm_w, mla_w_ukv, swa_sink, gla_w_decay, gla_b_decay, gla_norm_w, gqa_q_norm_w, gqa_k_norm_w, w_branch, w_o, mlp_w1, mlp_w2, final_norm_w):
    b, t, d = x.shape
    cn = ctx.shape[1]
    depth = ada_w.shape[0]
    tm = TOKEN_TILE
    assert t % tm == 0 and cn % tm == 0 and t % GRID_W == 0 and t >= SWA_BLOCK + 2 * WINDOW
    assert t % ATTN_TQ == 0 and t % cn == 0 and t % GLA_BLOCK == 0 and cn % GLA_BLOCK == 0
    nlat, nall = t // tm, (t + cn) // tm

    rows = 16
    cs = jnp.concatenate([c, c_ctx[None], jnp.zeros((rows - b - 1, d), F32)], axis=0)
    mod = _ada(cs, ada_w, ada_b)
    modsel = jnp.stack([mod[:, :b], jnp.broadcast_to(mod[:, b:b + 1], (depth, b, 6 * d))], axis=2)
    modsel = modsel.reshape(depth, b * 2, 1, 6 * d)

    tabs = (_rope_tables(t, cn, A_HEAD_PAD, A_NOPE, A_ROPE)
            + tuple(jnp.tile(tb, (1, LANE // B_HD)) for tb in _rope_tables(t, cn, B_HD, 0, B_HD))
            + _rope_tables(t, cn, D_HD, 0, D_HD))
    sel = _gla_sel()
    perms = (jnp.asarray(_perm_matrix(B_HEADS * B_HD, B_HD // 4), BF16),
             jnp.asarray(_perm_matrix(B_KV_HEADS * B_HD, B_HD // 4), BF16),
             jnp.asarray(_perm_matrix(D_HEADS * D_HD, D_HD // 4), BF16),
             jnp.asarray(_perm_matrix(D_KV_HEADS * D_HD, D_HD // 4), BF16))
    d_partner = _partner_lanes(D_HD, D_HD // 4)

    w_p = _layout_w_in(w_in)
    wbr, wo = w_branch.astype(BF16), w_o.astype(BF16)
    wb_b = jax.vmap(_layout_w_branch_b)(w_branch[:, 1]).astype(BF16)
    w1, w2 = mlp_w1.astype(BF16), mlp_w2.astype(BF16)
    wq, wk, wv = jax.vmap(_layout_mla)(mla_w_uq, mla_w_ukv)
    wdec, bdec = jax.vmap(_layout_decay)(gla_w_decay), gla_b_decay[:, :, None, :]

    def rows(w):
        return w[:, None, :]

    prep_wts = ([(rows(mla_q_norm_w), True), (rows(mla_kv_norm_w), True), (wq, True), (wk, True), (wv, True)]
                + [(pm, False) for pm in perms]
                + [(rows(gqa_q_norm_w), True), (rows(gqa_q_norm_w[:, d_partner]), True),
                   (rows(gqa_k_norm_w), True), (rows(gqa_k_norm_w[:, d_partner]), True)])
    n1, n2, gnw = rows(norm1_w), rows(norm2_w), rows(gla_norm_w)

    xs = (x, ctx)
    for l in range(depth):
        last = l == depth - 1
        nblk = nlat if last else nall
        p = _inproj(xs, modsel, n1, w_p, l, nlat)
        qa, ka, va, qb, kb, qd, kd = _prep(p, tabs, prep_wts, l)
        ya = _attention(qa, ka, va, heads=A_HEADS, kv_group=1, dk=A_HEAD_PAD, dv=A_V, vcol0=0,
                        t=t, cn=cn, with_ctx=not last, name="attn_a")
        yd = _attention(qd, kd, p, heads=D_HEADS, kv_group=D_HEADS // D_KV_HEADS, dk=D_HD, dv=D_HD,
                        vcol0=SEG_DV[0] // SEG_DV[1], t=t, cn=cn, with_ctx=not last, name="attn_d")
        yb = _swa(swa_sink, qb, kb, p, l, t=t, cn=cn,
                  n_blocks=(t if last else t + cn) // SWA_BLOCK)
        o_f, o_b = _gla(p, wdec, bdec, sel, l, t=t, cn=cn)
        xs = _merge(xs, modsel, ya, yb, o_f, o_b, p, yd, gnw,
                    wbr, wb_b, wo, l, n_blocks=nblk, nlat_blocks=nlat)
        xs = _mlp(xs, modsel, n2, w1, w2, final_norm_w[None], l,
                  n_blocks=nblk, nlat_blocks=nlat, out_rows=t if last else t + cn, final=last)
    return xs
```

```python
import functools

import numpy as np
import jax
import jax.numpy as jnp
from jax import lax
from jax.experimental import pallas as pl
from jax.experimental.pallas import tpu as pltpu

F32 = jnp.float32
BF16 = jnp.bfloat16

GRID_W = 64
EPS = 1e-6
ROPE_BASE = 10000.0
NEG_INF = -1e30

A_HEADS, A_NOPE, A_ROPE, A_V, A_Q_RANK, A_KV_RANK = 4, 128, 64, 128, 256, 128
A_HEAD_PAD = 256
B_HEADS, B_KV_HEADS, B_HD, WINDOW = 8, 2, 64, 128
SWA_BLOCK = 256
SWA_ROW_TILE = 32
C_HEADS, C_DK, C_DV, C_DECAY_RANK, C_DECAY_TEMP, C_CHUNK = 4, 64, 128, 16, 16.0, 64
D_HEADS, D_KV_HEADS, D_HD = 4, 2, 128
N_BRANCH, BRANCH_W = 4, 512

LANE = 128
TOKEN_TILE = 256
LAST_TILE = 512
ATTN_TQ = 1024
ATTN_CHAIN_ROWS = 512
ATTN_HEADS_PER_STEP = 4
INPROJ_TN = 2048
MERGE_TN = 256
VMEM_LIMIT = 56 * 1024 * 1024

P_COLS = 8192
SEG_GATE = (0, 4096)
SEG_DQ, SEG_DK, SEG_DV = (4096, 512), (4608, 256), (4864, 256)
SEG_BQ, SEG_BK, SEG_BV = (5120, 512), (5632, 128), (5760, 128)
SEG_ACQ, SEG_ACKV, SEG_AKR = (5888, 256), (6144, 128), (6272, 128)
SEG_CQ, SEG_CV, SEG_CR, SEG_CK, SEG_CLR = (6400, 256), (6656, 512), (7168, 512), (7680, 256), (7936, 128)


def _cparams(sem):
    return pltpu.CompilerParams(dimension_semantics=sem, vmem_limit_bytes=VMEM_LIMIT)


def _rms(xf, w):
    return xf * lax.rsqrt(jnp.mean(xf * xf, axis=-1, keepdims=True) + EPS) * w


def _dot(a, b):
    return jnp.dot(a, b, preferred_element_type=F32)


def _dot_nt(a, b):
    return lax.dot_general(a, b, (((1,), (1,)), ((), ())), preferred_element_type=F32)


def _dot_tn(a, b):
    return lax.dot_general(a, b, (((0,), (0,)), ((), ())), preferred_element_type=F32)


def _rope_tables(t, cn, width, offset, rot):
    half = rot // 2
    q = half // 2
    inv = ROPE_BASE ** (-np.arange(q, dtype=np.float64) / q)
    pos = np.arange(t)
    cos = np.ones((t + cn, width), np.float64)
    sin = np.zeros((t + cn, width), np.float64)
    for part, p in enumerate((pos // GRID_W, pos % GRID_W)):
        ang = p[:, None] * inv[None, :]
        lo = offset + part * half
        cos[:t, lo:lo + half] = np.concatenate([np.cos(ang), np.cos(ang)], axis=1)
        sin[:t, lo:lo + half] = np.concatenate([-np.sin(ang), np.sin(ang)], axis=1)
    return jnp.asarray(cos, F32), jnp.asarray(sin, F32)


def _ada_kernel(cs_ref, w_ref, b_ref, o_ref):
    cs = cs_ref[...]
    s = cs * jax.nn.sigmoid(cs)
    o_ref[0] = _dot(s.astype(BF16), w_ref[0].astype(BF16)) + b_ref[0]


def _ada(cs, ada_w, ada_b):
    nl, d, n = ada_w.shape
    tn = 1536
    rows = cs.shape[0]
    return pl.pallas_call(
        _ada_kernel,
        grid=(nl, n // tn),
        in_specs=[
            pl.BlockSpec((rows, d), lambda l, j: (0, 0)),
            pl.BlockSpec((1, d, tn), lambda l, j: (l, 0, j)),
            pl.BlockSpec((1, 1, tn), lambda l, j: (l, 0, j)),
        ],
        out_specs=pl.BlockSpec((1, rows, tn), lambda l, j: (l, 0, j)),
        out_shape=jax.ShapeDtypeStruct((nl, rows, n), F32),
        compiler_params=_cparams(("parallel", "parallel")),
        name="ada",
    )(cs, ada_w, ada_b.reshape(nl, 1, n))


def _token_specs(xs, tm, nlat_blocks):
    if not isinstance(xs, tuple):
        return [pl.BlockSpec((1, tm, xs.shape[2]), lambda bb, i: (bb, i, 0))]
    x, ctx = xs
    return [pl.BlockSpec((1, tm, x.shape[2]), lambda bb, i: (bb, jnp.minimum(i, nlat_blocks - 1), 0)),
            pl.BlockSpec((1, tm, ctx.shape[2]), lambda bb, i: (bb, jnp.maximum(i - nlat_blocks, 0), 0))]


def _token_rows(x_refs, nlat_blocks):
    if len(x_refs) == 1:
        return x_refs[0][0]
    return jnp.where(pl.program_id(1) >= nlat_blocks, x_refs[1][0], x_refs[0][0])


def _inproj_kernel(*refs, d, nsrc, nlat_blocks):
    mod_ref, nw_ref, w_ref, o_ref = refs[nsrc:]
    mod = mod_ref[0]
    x = _token_rows(refs[:nsrc], nlat_blocks)
    h = (_rms(x, nw_ref[...]) * (1.0 + mod[:, d:2 * d]) + mod[:, 0:d]).astype(BF16)
    for j in range(P_COLS // INPROJ_TN):
        cs = slice(j * INPROJ_TN, (j + 1) * INPROJ_TN)
        o_ref[0, :, cs] = _dot(h, w_ref[:, cs]).astype(BF16)


def _resident(shape):
    return pl.BlockSpec(shape, lambda *_: (0,) * len(shape), pipeline_mode=pl.Buffered(1))


def _layer_resident(arr, *lead):
    rest = arr.shape[len(lead):]
    return pl.BlockSpec((None,) * len(lead) + rest, lambda *_: tuple(lead) + (0,) * len(rest),
                        pipeline_mode=pl.Buffered(1))


def _mod_spec(modsel, l, nlat_blocks):
    return pl.BlockSpec((None, 1) + modsel.shape[2:],
                        lambda bb, i: (l, bb * 2 + jnp.where(i >= nlat_blocks, 1, 0), 0, 0))


def _inproj(xs, modsel, nw, w_p, l, nlat_blocks):
    srcs = xs if isinstance(xs, tuple) else (xs,)
    b, d = srcs[0].shape[0], srcs[0].shape[2]
    s = sum(a.shape[1] for a in srcs)
    tm = TOKEN_TILE
    return pl.pallas_call(
        functools.partial(_inproj_kernel, d=d, nsrc=len(srcs), nlat_blocks=nlat_blocks),
        grid=(b, s // tm),
        in_specs=_token_specs(xs, tm, nlat_blocks) + [
            _mod_spec(modsel, l, nlat_blocks),
            _layer_resident(nw, l),
            _layer_resident(w_p, l),
        ],
        out_specs=pl.BlockSpec((1, tm, P_COLS), lambda bb, i: (bb, i, 0)),
        out_shape=jax.ShapeDtypeStruct((b, s, P_COLS), BF16),
        compiler_params=_cparams(("parallel", "parallel")),
        name="inproj",
    )(*srcs, modsel, nw, w_p)


def _prep_kernel(acq_ref, ackv_ref, akr_ref, bq_ref, bk_ref, dq_ref, dk_ref,
                 cosa_ref, sina_ref, cosb_ref, sinb_ref, cosd_ref, sind_ref,
                 aqn_ref, akvn_ref, wq_ref, wk_ref, wv_ref, permbq_ref, permbk_ref,
                 permdq_ref, permdk_ref, dqn_ref, dqnp_ref, dkn_ref, dknp_ref,
                 qa_ref, ka_ref, va_ref, qb_ref, kb_ref, qd_ref, kd_ref):
    na = A_HEADS * A_HEAD_PAD
    cos_r, sin_r = cosa_ref[:, A_NOPE:], sina_ref[:, A_NOPE:]

    def rope_heads(x, w_ref, o_ref, scale):
        for h in range(A_HEADS):
            c0 = h * A_HEAD_PAD
            main = _dot(x, w_ref[:, c0:c0 + A_HEAD_PAD])
            part = _dot(x, w_ref[:, na + c0 + A_NOPE:na + c0 + A_HEAD_PAD])
            out = jnp.concatenate([main[:, :A_NOPE], main[:, A_NOPE:] * cos_r + part * sin_r], axis=1)
            o_ref[0, :, c0:c0 + A_HEAD_PAD] = (out * scale).astype(BF16)

    cq = _rms(acq_ref[0].astype(F32), aqn_ref[...]).astype(BF16)
    rope_heads(cq, wq_ref, qa_ref, (A_NOPE + A_ROPE) ** -0.5)
    ckv = _rms(ackv_ref[0].astype(F32), akvn_ref[...]).astype(BF16)
    rope_heads(jnp.concatenate([ckv, akr_ref[0]], axis=1), wk_ref, ka_ref, 1.0)
    va_ref[0] = _dot(ckv, wv_ref[...]).astype(BF16)

    nrep = B_HEADS * B_HD // LANE
    cosb, sinb = cosb_ref[...], sinb_ref[...]
    bq = bq_ref[0]
    qb = (bq.astype(F32) * jnp.concatenate([cosb] * nrep, axis=1)
          + _dot(bq, permbq_ref[...]) * jnp.concatenate([sinb] * nrep, axis=1))
    qb_ref[0] = (qb * (B_HD ** -0.5)).astype(BF16)
    bk = bk_ref[0]
    kb_ref[0] = (bk.astype(F32) * cosb + _dot(bk, permbk_ref[...]) * sinb).astype(BF16)

    cosd, sind = cosd_ref[...], sind_ref[...]

    def norm_rope(x_ref, perm_ref, nw_ref, nwp_ref, heads):
        x = x_ref[0]
        xp = _dot(x, perm_ref[...])
        xf = x.astype(F32)
        wc, ws = nw_ref[...] * cosd, nwp_ref[...] * sind
        parts = []
        for h in range(heads):
            hs = slice(h * D_HD, (h + 1) * D_HD)
            xh = xf[:, hs]
            inv = lax.rsqrt(jnp.mean(xh * xh, axis=-1, keepdims=True) + EPS)
            parts.append(inv * (xh * wc + xp[:, hs] * ws))
        return jnp.concatenate(parts, axis=1)

    qd_ref[0] = (norm_rope(dq_ref, permdq_ref, dqn_ref, dqnp_ref, D_HEADS) * (D_HD ** -0.5)).astype(BF16)
    kd_ref[0] = norm_rope(dk_ref, permdk_ref, dkn_ref, dknp_ref, D_KV_HEADS).astype(BF16)


def _pspec(seg, tm):
    off, width = seg
    return pl.BlockSpec((1, tm, width), lambda bb, i: (bb, i, off // width))


def _prep(p, tabs, wts, l):
    b, s, _ = p.shape
    tm = TOKEN_TILE
    segs = [SEG_ACQ, SEG_ACKV, SEG_AKR, SEG_BQ, SEG_BK, SEG_DQ, SEG_DK]
    tab_specs = [pl.BlockSpec((tm, t.shape[1]), lambda bb, i: (i, 0)) for t in tabs]
    wt_specs = [_layer_resident(w, l) if stacked else _resident(w.shape) for w, stacked in wts]
    wts = [w for w, _ in wts]
    widths = [A_HEADS * A_HEAD_PAD, A_HEADS * A_HEAD_PAD, A_HEADS * A_V,
              B_HEADS * B_HD, B_KV_HEADS * B_HD, D_HEADS * D_HD, D_KV_HEADS * D_HD]
    return pl.pallas_call(
        _prep_kernel,
        grid=(b, s // tm),
        in_specs=[_pspec(sg, tm) for sg in segs] + tab_specs + wt_specs,
        out_specs=[pl.BlockSpec((1, tm, w), lambda bb, i: (bb, i, 0)) for w in widths],
        out_shape=[jax.ShapeDtypeStruct((b, s, w), BF16) for w in widths],
        compiler_params=_cparams(("parallel", "parallel")),
        name="prep",
    )(*([p] * len(segs)), *tabs, *wts)


def _attn_kernel(q_ref, k_ref, v_ref, o_ref, *, dk, dv, kv_group):
    rows = min(ATTN_CHAIN_ROWS, q_ref.shape[1])
    for r in range(q_ref.shape[1] // rows):
        rs = slice(r * rows, (r + 1) * rows)
        for c in range(ATTN_HEADS_PER_STEP):
            kc = c // kv_group
            s = _dot_nt(q_ref[0, rs, c * dk:(c + 1) * dk], k_ref[0, :, kc * dk:(kc + 1) * dk])
            m = jnp.max(s, axis=-1, keepdims=True)
            p = jnp.exp(s - m)
            l = jnp.sum(p, axis=-1, keepdims=True)
            o = _dot(p.astype(BF16), v_ref[0, :, kc * dv:(kc + 1) * dv])
            o_ref[0, rs, c * dv:(c + 1) * dv] = (o / l).astype(o_ref.dtype)


def _attention(q, k, v, *, heads, kv_group, dk, dv, vcol0, t, cn, with_ctx, name):
    b, s, _ = q.shape
    hps = ATTN_HEADS_PER_STEP
    nkv = hps // kv_group
    kern = functools.partial(_attn_kernel, dk=dk, dv=dv, kv_group=kv_group)
    sem = _cparams(("parallel", "parallel", "parallel"))

    def call(tq, nq, row0, krows, krow0, nm):
        return pl.pallas_call(
            kern,
            grid=(b, heads // hps, nq // tq),
            in_specs=[
                pl.BlockSpec((1, tq, hps * dk), lambda bb, h, i: (bb, row0 + i, h)),
                pl.BlockSpec((1, krows, nkv * dk), lambda bb, h, i: (bb, krow0, h)),
                pl.BlockSpec((1, krows, nkv * dv), lambda bb, h, i: (bb, krow0, vcol0 + h)),
            ],
            out_specs=pl.BlockSpec((1, tq, hps * dv), lambda bb, h, i: (bb, i, h)),
            out_shape=jax.ShapeDtypeStruct((b, nq, heads * dv), BF16),
            compiler_params=sem,
            name=nm,
        )(q, k, v)

    y = call(ATTN_TQ, t, 0, s, 0, name)
    if with_ctx:
        return y, call(cn, cn, t // cn, cn, t // cn, name + "_ctx")
    return y


def _swa_bias():
    band = SWA_BLOCK + 2 * WINDOW
    r = np.arange(SWA_BLOCK)[:, None]
    j = np.arange(band)[None, :]
    out = []
    for off in (0, WINDOW, band - SWA_BLOCK):
        out.append(np.where(np.abs(off + r - j) <= WINDOW, 0.0, NEG_INF))
    return np.stack(out).astype(np.float32)


def _swa_kernel(sink_ref, q_ref, k_ref, v_ref, bias_ref, o_ref, s_ref, p_ref, e_ref, *, t, cn, layer):
    n = pl.program_id(1)
    blk = SWA_BLOCK
    nlat = t // blk
    grp = B_HEADS // B_KV_HEADS
    band = blk + 2 * WINDOW
    rt = SWA_ROW_TILE

    def run(is_ctx):
        q = q_ref[0]
        k, v = k_ref[0, t:t + cn, :], v_ref[0, t:t + cn, :]
        if not is_ctx:
            start = pl.multiple_of(jnp.clip(n * blk - WINDOW, 0, t - band), WINDOW)
            k = jnp.concatenate([k, k_ref[0, pl.ds(start, band), :]], axis=0)
            v = jnp.concatenate([v, v_ref[0, pl.ds(start, band), :]], axis=0)
        nk = k.shape[0]
        halves = []
        qlane = lax.broadcasted_iota(jnp.int32, (blk, LANE), 1)
        for hk in range(B_KV_HEADS):
            mine = ((qlane // B_HD) == hk).astype(BF16)
            q4 = jnp.concatenate([q[:, gi * LANE:(gi + 1) * LANE] * mine for gi in range(grp)], axis=0)
            s_ref[hk, :, :nk] = _dot_nt(q4, k)

            for i in range(grp * blk // rt):
                rs = slice(i * rt, (i + 1) * rt)
                sink = sink_ref[layer, hk * grp + i // (blk // rt)]
                s = s_ref[hk, rs, :nk]
                if not is_ctx:
                    br = (i % (blk // rt)) * rt
                    s = s + bias_ref[0, br:br + rt, :]
                m = jnp.maximum(jnp.max(s, axis=-1, keepdims=True), sink)
                p = jnp.exp(s - m)
                l = jnp.sum(p, axis=-1, keepdims=True) + jnp.exp(sink - m)
                p_ref[hk, rs, :nk] = p.astype(BF16)
                e_ref[hk, rs, :] = jnp.broadcast_to(l, (rt, LANE))
            halves.append(_dot(p_ref[hk, :, :nk], v) / e_ref[hk])
        lane = lax.broadcasted_iota(jnp.int32, (blk, LANE), 1)
        outs = [jnp.where(lane < B_HD, halves[0][gi * blk:(gi + 1) * blk], halves[1][gi * blk:(gi + 1) * blk])
                for gi in range(grp)]
        o_ref[0] = jnp.concatenate(outs, axis=1).astype(o_ref.dtype)

    @pl.when(n < nlat)
    def _():
        run(False)

    @pl.when(n >= nlat)
    def _():
        run(True)


def _swa(sink, q, k, p, l, *, t, cn, n_blocks):
    b, s, _ = q.shape
    nlat = t // SWA_BLOCK
    rows = (B_HEADS // B_KV_HEADS) * SWA_BLOCK
    band_bias = _swa_bias()
    bias = jnp.asarray(np.concatenate([np.zeros(band_bias.shape[:2] + (cn,), np.float32), band_bias], axis=2))
    return pl.pallas_call(
        functools.partial(_swa_kernel, t=t, cn=cn, layer=l),
        grid=(b, n_blocks),
        in_specs=[
            pl.BlockSpec(memory_space=pltpu.SMEM),
            pl.BlockSpec((1, SWA_BLOCK, B_HEADS * B_HD), lambda bb, n: (bb, n, 0)),
            pl.BlockSpec((1, s, B_KV_HEADS * B_HD), lambda bb, n: (bb, 0, 0)),
            pl.BlockSpec((1, s, SEG_BV[1]), lambda bb, n: (bb, 0, SEG_BV[0] // SEG_BV[1])),
            pl.BlockSpec((1,) + bias.shape[1:],
                         lambda bb, n: (jnp.where(n == 0, 0, jnp.where(n >= nlat - 1, 2, 1)), 0, 0)),
        ],
        out_specs=pl.BlockSpec((1, SWA_BLOCK, B_HEADS * B_HD), lambda bb, n: (bb, n, 0)),
        out_shape=jax.ShapeDtypeStruct((b, n_blocks * SWA_BLOCK, B_HEADS * B_HD), BF16),
        scratch_shapes=[pltpu.VMEM((B_KV_HEADS, rows, bias.shape[2]), F32),
                        pltpu.VMEM((B_KV_HEADS, rows, bias.shape[2]), BF16),
                        pltpu.VMEM((B_KV_HEADS, rows, LANE), F32)],
        compiler_params=_cparams(("parallel", "parallel")),
        name="swa",
    )(sink, q, k, p, bias)


GLA_BLOCK = 256
GLA_SAFE_DECAY = 60.0


def _log_decay(lr, w, bias):
    z = _dot(lr, w) + bias
    return (jnp.minimum(z, 0.0) - jnp.log(1.0 + jnp.exp(-jnp.abs(z)))) / C_DECAY_TEMP


def _cum_decay(la, tri):
    hi = la.astype(BF16)
    lo = (la - hi.astype(F32)).astype(BF16)
    return _dot(tri, hi) + _dot(tri, lo)


def _gla_kernel(qf_ref, kf_ref, vf_ref, lrf_ref, qb_ref, kb_ref, vb_ref, lrb_ref,
                wdec_ref, bdec_ref, sel_ref, of_ref, ob_ref, st_ref, e_ref):
    n = GLA_BLOCK
    pair = LANE // C_DK
    npair = C_HEADS // pair
    qscale = C_DK ** -0.5
    refs = ((qf_ref, kf_ref, vf_ref, lrf_ref), (qb_ref, kb_ref, vb_ref, lrb_ref))
    outs = (of_ref, ob_ref)

    @pl.when(pl.program_id(1) == 0)
    def _():
        st_ref[...] = jnp.zeros_like(st_ref)

    r_i = lax.broadcasted_iota(jnp.int32, (n, n), 0)
    c_i = lax.broadcasted_iota(jnp.int32, (n, n), 1)
    keep = (c_i <= r_i, c_i >= r_i)
    lane = lax.broadcasted_iota(jnp.int32, (1, LANE), 1)
    head_mask = [(lane // C_DK) == hh for hh in range(pair)]

    cums, tots = [], []
    for d in range(2):
        la = _log_decay(refs[d][3][0], wdec_ref[d], bdec_ref[d])
        cum = _cum_decay(la, keep[d].astype(BF16))
        cums.append(cum)
        tots.append(cum[n - 1:n] if d == 0 else cum[0:1])
    safe = jnp.maximum(jnp.max(-tots[0]), jnp.max(-tots[1])) <= GLA_SAFE_DECAY

    @pl.when(safe)
    def _():
        for d in range(2):
            q_ref, k_ref, v_ref, _ = refs[d]
            cum, tot = cums[d], tots[d]
            half = 0.5 * tot
            a = cum - half
            eh = jnp.exp(half)
            qr = q_ref[0].astype(F32) * qscale * jnp.exp(a)
            kr = k_ref[0].astype(F32) * jnp.exp(-a)
            kr_b, kd_b = kr.astype(BF16), (kr * eh).astype(BF16)
            qe = qr * eh
            v, st = v_ref[0], st_ref[d]
            o_parts, u_parts = [], []
            for p in range(npair):
                ls = slice(p * LANE, (p + 1) * LANE)
                st2 = st[:, ls].astype(BF16)
                upd = []
                for hh in range(pair):
                    h = p * pair + hh
                    vh = v[:, h * C_DV:(h + 1) * C_DV]
                    qr_h = jnp.where(head_mask[hh], qr[:, ls], 0.0).astype(BF16)
                    qe_h = jnp.where(head_mask[hh], qe[:, ls], 0.0).astype(BF16)
                    pm = jnp.where(keep[d], _dot_nt(qr_h, kr_b[:, ls]), 0.0).astype(BF16)
                    o_parts.append(_dot(pm, vh) + _dot_nt(qe_h, st2))
                    upd.append(_dot_tn(vh, kd_b[:, ls]))
                u_parts.append(jnp.where(head_mask[0], upd[0], upd[1]))
            outs[d][0] = jnp.concatenate(o_parts, axis=1)
            st_ref[d] = st * jnp.exp(tot) + jnp.concatenate(u_parts, axis=1)

    @pl.when(jnp.logical_not(safe))
    def _():
        c = C_CHUNK
        tile = 16
        nsub = n // c
        r64 = lax.broadcasted_iota(jnp.int32, (c, c), 0)
        c64 = lax.broadcasted_iota(jnp.int32, (c, c), 1)
        tri64 = ((c64 <= r64).astype(BF16), (c64 >= r64).astype(BF16))
        for d in range(2):
            q_ref, k_ref, v_ref, lr_ref = refs[d]
            e_ref[...] = jnp.zeros_like(e_ref)

            def sub(i, carry, d=d, q_ref=q_ref, k_ref=k_ref, v_ref=v_ref, lr_ref=lr_ref):
                rows = pl.ds(pl.multiple_of((i if d == 0 else nsub - 1 - i) * c, c), c)
                q = q_ref[0, rows, :].astype(F32) * qscale
                k = k_ref[0, rows, :].astype(F32)
                v = v_ref[0, rows, :]
                cum = _cum_decay(_log_decay(lr_ref[0, rows, :], wdec_ref[d], bdec_ref[d]), tri64[d])
                tot = cum[c - 1:c] if d == 0 else cum[0:1]
                qe = (q * jnp.exp(cum)).astype(BF16)
                kd = (k * jnp.exp(tot - cum)).astype(BF16)
                for p in range(npair):
                    ls = slice(p * LANE, (p + 1) * LANE)
                    q2, k2, c2 = q[:, ls], k[:, ls], cum[:, ls]
                    for s in range(c):
                        bt = (s // tile) * tile
                        t0, t1 = (bt, c) if d == 0 else (0, bt + tile)
                        e = q2[t0:t1] * k2[s:s + 1] * jnp.exp(c2[t0:t1] - c2[s:s + 1])
                        ridx = lax.broadcasted_iota(jnp.int32, (t1 - t0, LANE), 0) + t0
                        causal = (ridx >= s) if d == 0 else (ridx <= s)
                        e_ref[p * c + t0:p * c + t1, s * LANE:(s + 1) * LANE] = (
                            jnp.where(causal, e, 0.0).astype(BF16))
                pm = _dot(e_ref[...], sel_ref[...])
                st = st_ref[d]
                o_parts, u_parts = [], []
                for h in range(C_HEADS):
                    p, hh = divmod(h, pair)
                    ph = pm[p * c:(p + 1) * c, hh * c:(hh + 1) * c].astype(BF16)
                    vh = v[:, h * C_DV:(h + 1) * C_DV]
                    hs = slice(h * C_DK, (h + 1) * C_DK)
                    o_parts.append(_dot(ph, vh) + _dot_nt(qe[:, hs], st[:, hs].astype(BF16)))
                    u_parts.append(_dot_tn(vh, kd[:, hs]))
                outs[d][0, rows, :] = jnp.concatenate(o_parts, axis=1)
                st_ref[d] = st * jnp.exp(tot) + jnp.concatenate(u_parts, axis=1)
                return carry

            lax.fori_loop(0, nsub, sub, 0)


def _gla_sel():
    pair = LANE // C_DK
    sel = np.zeros((C_CHUNK, pair, C_DK, pair, C_CHUNK), np.float32)
    for s in range(C_CHUNK):
        for h in range(pair):
            sel[s, h, :, h, s] = 1.0
    return jnp.asarray(sel.reshape(C_CHUNK * LANE, pair * C_CHUNK), BF16)


def _gla(p, wdec, bdec, sel, l, *, t, cn):
    b, s, _ = p.shape
    n = GLA_BLOCK
    nbl, nbc = t // n, cn // n

    def fwd(j):
        return jnp.where(j < nbc, nbl + j, j - nbc)

    def bwd(j):
        return nbl + nbc - 1 - j

    def specs(order):
        return [pl.BlockSpec((1, n, sg[1]), lambda bb, j, sg=sg: (bb, order(j), sg[0] // sg[1]))
                for sg in (SEG_CQ, SEG_CK, SEG_CV, SEG_CLR)]

    width = C_HEADS * C_DV
    return pl.pallas_call(
        _gla_kernel,
        grid=(b, nbl + nbc),
        in_specs=specs(fwd) + specs(bwd) + [
            _layer_resident(wdec, l), _layer_resident(bdec, l), _resident(sel.shape),
        ],
        out_specs=[pl.BlockSpec((1, n, width), lambda bb, j: (bb, fwd(j), 0)),
                   pl.BlockSpec((1, n, width), lambda bb, j: (bb, bwd(j), 0))],
        out_shape=[jax.ShapeDtypeStruct((b, s, width), F32)] * 2,
        scratch_shapes=[pltpu.VMEM((2, C_DV, C_HEADS * C_DK), F32),
                        pltpu.VMEM(((C_HEADS * C_DK // LANE) * C_CHUNK, C_CHUNK * LANE), BF16)],
        compiler_params=_cparams(("parallel", "arbitrary")),
        name="gla",
    )(*([p] * 8), wdec, bdec, sel)


def _merge_kernel(*refs, d, nsrc, nlat_blocks):
    x_refs, ya_refs, yd_refs = (refs[i * nsrc:(i + 1) * nsrc] for i in range(3))
    (mod_ref, yb_ref, of_ref, ob_ref, r_ref, g_ref,
     gnw_ref, wba_ref, wbb_ref, wbc_ref, wbd_ref, wo_ref, o_ref) = refs[3 * nsrc:]
    o = of_ref[0] + ob_ref[0]
    r = r_ref[0].astype(F32)
    yg = jnp.concatenate([_rms(o[:, h * C_DV:(h + 1) * C_DV], gnw_ref[...]) for h in range(C_HEADS)], axis=1)
    yg = (yg * (r * jax.nn.sigmoid(r))).astype(BF16)
    branches = ((_token_rows(ya_refs, nlat_blocks), wba_ref), (yb_ref[0], wbb_ref), (yg, wbc_ref),
                (_token_rows(yd_refs, nlat_blocks), wbd_ref))
    mixed = []
    for c in range(d // MERGE_TN):
        cs = slice(c * MERGE_TN, (c + 1) * MERGE_TN)
        acc_z = acc_t = None
        for n, (y, w_ref) in enumerate(branches):
            z = _dot(y, w_ref[:, cs])
            tz = jnp.tanh(g_ref[0, :, n * d + c * MERGE_TN:n * d + (c + 1) * MERGE_TN].astype(F32)) * z
            acc_z = z if acc_z is None else acc_z + z
            acc_t = tz if acc_t is None else acc_t + tz
        mixed.append((0.5 * (acc_z + acc_t)).astype(BF16))
    out = _dot(jnp.concatenate(mixed, axis=1), wo_ref[...])
    o_ref[0] = _token_rows(x_refs, nlat_blocks) + mod_ref[0][:, 2 * d:3 * d] * out


def _merge(xs, modsel, ya, yb, o_f, o_b, p, yd, gnw, w_branch, wb_b, wo, l, *, tm, n_blocks, nlat_blocks):
    assert isinstance(xs, tuple) == isinstance(ya, tuple) == isinstance(yd, tuple)
    streams = [s if isinstance(s, tuple) else (s,) for s in (xs, ya, yd)]
    srcs = [a for s in streams for a in s]
    b, d = srcs[0].shape[0], srcs[0].shape[2]

    def tok(w, col=0):
        return pl.BlockSpec((1, tm, w), lambda bb, i: (bb, i, col))

    w_specs = [_layer_resident(w_branch, l, 0), _layer_resident(wb_b, l),
               _layer_resident(w_branch, l, 2), _layer_resident(w_branch, l, 3), _layer_resident(wo, l)]
    return pl.pallas_call(
        functools.partial(_merge_kernel, d=d, nsrc=len(streams[0]), nlat_blocks=nlat_blocks),
        grid=(b, n_blocks),
        in_specs=[sp for s in (xs, ya, yd) for sp in _token_specs(s, tm, nlat_blocks)] + [
            _mod_spec(modsel, l, nlat_blocks),
            tok(yb.shape[2]), tok(o_f.shape[2]), tok(o_b.shape[2]),
            tok(SEG_CR[1], SEG_CR[0] // SEG_CR[1]),
            tok(SEG_GATE[1], 0),
            _layer_resident(gnw, l), *w_specs,
        ],
        out_specs=tok(d),
        out_shape=jax.ShapeDtypeStruct((b, n_blocks * tm, d), F32),
        compiler_params=_cparams(("parallel", "parallel")),
        name="merge",
    )(*srcs, modsel, yb, o_f, o_b, p, p, gnw, w_branch, wb_b, w_branch, w_branch, wo)


def _mlp_kernel(x_ref, mod_ref, nw_ref, w1_ref, w2_ref, fnw_ref, o_ref, *, d, final):
    x = x_ref[0]
    mod = mod_ref[0]
    h = _rms(x, nw_ref[...]) * (1.0 + mod[:, 4 * d:5 * d]) + mod[:, 3 * d:4 * d]
    a = jnp.maximum(_dot(h.astype(BF16), w1_ref[...]), 0.0)
    y = x + mod[:, 5 * d:6 * d] * _dot((a * a).astype(BF16), w2_ref[...])
    o_ref[0] = _rms(y, fnw_ref[...]) if final else y


def _mlp(xs, modsel, nw, w1, w2, fnw, l, *, tm, n_blocks, nlat_blocks, out_rows, final):
    b, s, d = xs.shape
    return pl.pallas_call(
        functools.partial(_mlp_kernel, d=d, final=final),
        grid=(b, n_blocks),
        in_specs=[
            pl.BlockSpec((1, tm, d), lambda bb, i: (bb, i, 0)),
            _mod_spec(modsel, l, nlat_blocks),
            _layer_resident(nw, l), _layer_resident(w1, l), _layer_resident(w2, l), _resident(fnw.shape),
        ],
        out_specs=pl.BlockSpec((1, tm, d), lambda bb, i: (bb, i, 0)),
        out_shape=jax.ShapeDtypeStruct((b, out_rows, d), F32),
        compiler_params=_cparams(("parallel", "parallel")),
        name="mlp",
    )(xs, modsel, nw, w1, w2, fnw)


def _pad_cols(w, n):
    return jnp.concatenate([w, jnp.zeros(w.shape[:-1] + (n,), w.dtype)], axis=-1)


def _layout_w_in(w):
    a0 = 0
    b0 = a0 + A_Q_RANK + A_KV_RANK + A_ROPE
    c0 = b0 + (B_HEADS + 2 * B_KV_HEADS) * B_HD
    d0 = c0 + 2 * C_HEADS * C_DK + 2 * C_HEADS * C_DV + 2 * C_DECAY_RANK
    g0 = d0 + (D_HEADS + 2 * D_KV_HEADS) * D_HD

    def cols(lo, n):
        return w[..., lo:lo + n]

    cq = c0
    ck = cq + C_HEADS * C_DK
    cv = ck + C_HEADS * C_DK
    cr = cv + C_HEADS * C_DV
    clr = cr + C_HEADS * C_DV
    parts = [
        cols(g0, SEG_GATE[1]) * 0.5,
        cols(d0, SEG_DQ[1] + SEG_DK[1] + SEG_DV[1]),
        *[cols(b0 + h * B_HD, B_HD) for h in _swa_head_order()],
        cols(b0 + SEG_BQ[1], SEG_BK[1] + SEG_BV[1]),
        cols(a0, A_Q_RANK + A_KV_RANK),
        _pad_cols(cols(a0 + A_Q_RANK + A_KV_RANK, A_ROPE), SEG_AKR[1] - A_ROPE),
        cols(cq, SEG_CQ[1]), cols(cv, SEG_CV[1]), cols(cr, SEG_CR[1]), cols(ck, SEG_CK[1]),
        _pad_cols(cols(clr, 2 * C_DECAY_RANK), P_COLS - SEG_CLR[0] - 2 * C_DECAY_RANK),
    ]
    return jnp.concatenate(parts, axis=-1).astype(BF16)


def _layout_mla(w_uq, w_ukv):
    zq = jnp.zeros((A_Q_RANK, A_HEAD_PAD - A_NOPE - A_ROPE), w_uq.dtype)
    hq = A_NOPE + A_ROPE
    wq = jnp.concatenate([jnp.concatenate([w_uq[:, h * hq:(h + 1) * hq], zq], axis=1)
                          for h in range(A_HEADS)], axis=1)
    hk = A_NOPE + A_V
    zk = jnp.zeros((A_KV_RANK, A_HEAD_PAD - A_NOPE), w_ukv.dtype)
    wk_top = jnp.concatenate([jnp.concatenate([w_ukv[:, h * hk:h * hk + A_NOPE], zk], axis=1)
                              for h in range(A_HEADS)], axis=1)
    place = np.zeros((SEG_AKR[1], A_HEADS * A_HEAD_PAD), np.float32)
    for h in range(A_HEADS):
        place[np.arange(A_ROPE), h * A_HEAD_PAD + A_NOPE + np.arange(A_ROPE)] = 1.0
    wk = jnp.concatenate([wk_top, jnp.asarray(place)], axis=0)
    wv = jnp.concatenate([w_ukv[:, h * hk + A_NOPE:(h + 1) * hk] for h in range(A_HEADS)], axis=1)
    lanes = np.arange(A_HEADS * A_HEAD_PAD)
    rot = ((lanes % A_HEAD_PAD) >= A_NOPE) & ((lanes % A_HEAD_PAD) < A_NOPE + A_ROPE)
    idx = _partner_lanes(lanes.size, A_ROPE // 4)

    def with_partner(w):
        return jnp.concatenate([w, jnp.where(jnp.asarray(rot)[None, :], w[:, idx], 0.0)], axis=1)

    return with_partner(wq).astype(BF16), with_partner(wk).astype(BF16), wv.astype(BF16)


def _partner_lanes(width, blk):
    j = np.arange(width)
    return np.where((j % (2 * blk)) < blk, j + blk, j - blk)


def _perm_matrix(width, blk):
    m = np.zeros((width, width), np.float32)
    m[_partner_lanes(width, blk), np.arange(width)] = 1.0
    return m


def _swa_head_order():
    grp = B_HEADS // B_KV_HEADS
    return [h for g in range(grp) for h in (g, grp + g)]


def _layout_w_branch_b(w):
    return jnp.concatenate([w[h * B_HD:(h + 1) * B_HD] for h in _swa_head_order()], axis=0)


def _layout_decay(w_decay):
    z = jnp.zeros((C_DECAY_RANK, w_decay.shape[-1]), w_decay.dtype)
    tail = jnp.zeros((SEG_CLR[1] - 2 * C_DECAY_RANK, w_decay.shape[-1]), w_decay.dtype)
    return jnp.stack([jnp.concatenate([w_decay[0], z, tail], axis=0),
                      jnp.concatenate([z, w_decay[1], tail], axis=0)]).astype(BF16)


def kernel(x, c, ctx, c_ctx, ada_w, ada_b, norm1_w, norm2_w, w_in, mla_q_norm_w, mla_w_uq, mla_kv_norm_w, mla_w_ukv, swa_sink, gla_w_decay, gla_b_decay, gla_norm_w, gqa_q_norm_w, gqa_k_norm_w, w_branch, w_o, mlp_w1, mlp_w2, final_norm_w):
    b, t, d = x.shape
    cn = ctx.shape[1]
    depth = ada_w.shape[0]
    tm = TOKEN_TILE
    assert t % tm == 0 and cn % tm == 0 and t % GRID_W == 0 and t >= SWA_BLOCK + 2 * WINDOW
    assert t % ATTN_TQ == 0 and t % cn == 0 and t % GLA_BLOCK == 0 and cn % GLA_BLOCK == 0
    nlat, nall = t // tm, (t + cn) // tm

    rows = 16
    cs = jnp.concatenate([c, c_ctx[None], jnp.zeros((rows - b - 1, d), F32)], axis=0)
    mod = _ada(cs, ada_w, ada_b)
    modsel = jnp.stack([mod[:, :b], jnp.broadcast_to(mod[:, b:b + 1], (depth, b, 6 * d))], axis=2)
    modsel = modsel.reshape(depth, b * 2, 1, 6 * d)

    tabs = (_rope_tables(t, cn, A_HEAD_PAD, A_NOPE, A_ROPE)
            + tuple(jnp.tile(tb, (1, LANE // B_HD)) for tb in _rope_tables(t, cn, B_HD, 0, B_HD))
            + _rope_tables(t, cn, D_HD, 0, D_HD))
    sel = _gla_sel()
    perms = (jnp.asarray(_perm_matrix(B_HEADS * B_HD, B_HD // 4), BF16),
             jnp.asarray(_perm_matrix(B_KV_HEADS * B_HD, B_HD // 4), BF16),
             jnp.asarray(_perm_matrix(D_HEADS * D_HD, D_HD // 4), BF16),
             jnp.asarray(_perm_matrix(D_KV_HEADS * D_HD, D_HD // 4), BF16))
    d_partner = _partner_lanes(D_HD, D_HD // 4)

    w_p = _layout_w_in(w_in)
    wbr, wo = w_branch.astype(BF16), w_o.astype(BF16)
    wb_b = jax.vmap(_layout_w_branch_b)(w_branch[:, 1]).astype(BF16)
    w1, w2 = mlp_w1.astype(BF16), mlp_w2.astype(BF16)
    wq, wk, wv = jax.vmap(_layout_mla)(mla_w_uq, mla_w_ukv)
    wdec, bdec = jax.vmap(_layout_decay)(gla_w_decay), gla_b_decay[:, :, None, :]

    def rows(w):
        return w[:, None, :]

    prep_wts = ([(rows(mla_q_norm_w), True), (rows(mla_kv_norm_w), True), (wq, True), (wk, True), (wv, True)]
                + [(pm, False) for pm in perms]
                + [(rows(gqa_q_norm_w), True), (rows(gqa_q_norm_w[:, d_partner]), True),
                   (rows(gqa_k_norm_w), True), (rows(gqa_k_norm_w[:, d_partner]), True)])
    n1, n2, gnw = rows(norm1_w), rows(norm2_w), rows(gla_norm_w)

    xs = (x, ctx)
    for l in range(depth):
        last = l == depth - 1
        p = _inproj(xs, modsel, n1, w_p, l, nlat)
        qa, ka, va, qb, kb, qd, kd = _prep(p, tabs, prep_wts, l)
        ya = _attention(qa, ka, va, heads=A_HEADS, kv_group=1, dk=A_HEAD_PAD, dv=A_V, vcol0=0,
                        t=t, cn=cn, with_ctx=not last, name="attn_a")
        yd = _attention(qd, kd, p, heads=D_HEADS, kv_group=D_HEADS // D_KV_HEADS, dk=D_HD, dv=D_HD,
                        vcol0=SEG_DV[0] // SEG_DV[1], t=t, cn=cn, with_ctx=not last, name="attn_d")
        yb = _swa(swa_sink, qb, kb, p, l, t=t, cn=cn,
                  n_blocks=(t if last else t + cn) // SWA_BLOCK)
        o_f, o_b = _gla(p, wdec, bdec, sel, l, t=t, cn=cn)
        if last and isinstance(xs, tuple):
            xs = xs[0]
        tile = dict(tm=LAST_TILE, n_blocks=t // LAST_TILE, nlat_blocks=t // LAST_TILE) if last else dict(
            tm=tm, n_blocks=nall, nlat_blocks=nlat)
        xs = _merge(xs, modsel, ya, yb, o_f, o_b, p, yd, gnw, wbr, wb_b, wo, l, **tile)
        xs = _mlp(xs, modsel, n2, w1, w2, final_norm_w[None], l,
                  out_rows=t if last else t + cn, final=last, **tile)
    return xs
```

```python
import functools

import numpy as np
import jax
import jax.numpy as jnp
from jax import lax
from jax.experimental import pallas as pl
from jax.experimental.pallas import tpu as pltpu

F32 = jnp.float32
BF16 = jnp.bfloat16

GRID_W = 64
EPS = 1e-6
ROPE_BASE = 10000.0
NEG_INF = -1e30

A_HEADS, A_NOPE, A_ROPE, A_V, A_Q_RANK, A_KV_RANK = 4, 128, 64, 128, 256, 128
A_HEAD_PAD = 256
B_HEADS, B_KV_HEADS, B_HD, WINDOW = 8, 2, 64, 128
SWA_BLOCK = 256
SWA_ROW_TILE = 32
C_HEADS, C_DK, C_DV, C_DECAY_RANK, C_DECAY_TEMP, C_CHUNK = 4, 64, 128, 16, 16.0, 64
D_HEADS, D_KV_HEADS, D_HD = 4, 2, 128
N_BRANCH, BRANCH_W = 4, 512

LANE = 128
TOKEN_TILE = 256
LAST_TILE = 512
PAIR_TILE = 2 * TOKEN_TILE
ATTN_TQ = 1024
ATTN_CHAIN_ROWS = 512
ATTN_HEADS_PER_STEP = 4
INPROJ_TN = 2048
MERGE_TN = 256
VMEM_LIMIT = 56 * 1024 * 1024

P_COLS = 8192
SEG_GATE = (0, 4096)
SEG_DQ, SEG_DK, SEG_DV = (4096, 512), (4608, 256), (4864, 256)
SEG_BQ, SEG_BK, SEG_BV = (5120, 512), (5632, 128), (5760, 128)
SEG_ACQ, SEG_ACKV, SEG_AKR = (5888, 256), (6144, 128), (6272, 128)
SEG_CQ, SEG_CV, SEG_CR, SEG_CK, SEG_CLR = (6400, 256), (6656, 512), (7168, 512), (7680, 256), (7936, 128)


def _cparams(sem):
    return pltpu.CompilerParams(dimension_semantics=sem, vmem_limit_bytes=VMEM_LIMIT)


def _rms(xf, w):
    return xf * lax.rsqrt(jnp.mean(xf * xf, axis=-1, keepdims=True) + EPS) * w


def _dot(a, b):
    return jnp.dot(a, b, preferred_element_type=F32)


def _dot_nt(a, b):
    return lax.dot_general(a, b, (((1,), (1,)), ((), ())), preferred_element_type=F32)


def _dot_tn(a, b):
    return lax.dot_general(a, b, (((0,), (0,)), ((), ())), preferred_element_type=F32)


def _rope_tables(t, cn, width, offset, rot):
    half = rot // 2
    q = half // 2
    inv = ROPE_BASE ** (-np.arange(q, dtype=np.float64) / q)
    pos = np.arange(t)
    cos = np.ones((t + cn, width), np.float64)
    sin = np.zeros((t + cn, width), np.float64)
    for part, p in enumerate((pos // GRID_W, pos % GRID_W)):
        ang = p[:, None] * inv[None, :]
        lo = offset + part * half
        cos[:t, lo:lo + half] = np.concatenate([np.cos(ang), np.cos(ang)], axis=1)
        sin[:t, lo:lo + half] = np.concatenate([-np.sin(ang), np.sin(ang)], axis=1)
    return jnp.asarray(cos, F32), jnp.asarray(sin, F32)


def _ada_kernel(cs_ref, w_ref, b_ref, o_ref):
    cs = cs_ref[...]
    s = cs * jax.nn.sigmoid(cs)
    o_ref[0] = _dot(s.astype(BF16), w_ref[0].astype(BF16)) + b_ref[0]


def _ada(cs, ada_w, ada_b):
    nl, d, n = ada_w.shape
    tn = 1536
    rows = cs.shape[0]
    return pl.pallas_call(
        _ada_kernel,
        grid=(nl, n // tn),
        in_specs=[
            pl.BlockSpec((rows, d), lambda l, j: (0, 0)),
            pl.BlockSpec((1, d, tn), lambda l, j: (l, 0, j)),
            pl.BlockSpec((1, 1, tn), lambda l, j: (l, 0, j)),
        ],
        out_specs=pl.BlockSpec((1, rows, tn), lambda l, j: (l, 0, j)),
        out_shape=jax.ShapeDtypeStruct((nl, rows, n), F32),
        compiler_params=_cparams(("parallel", "parallel")),
        name="ada",
    )(cs, ada_w, ada_b.reshape(nl, 1, n))


def _token_specs(xs, tm, nlat_blocks):
    if not isinstance(xs, tuple):
        return [pl.BlockSpec((1, tm, xs.shape[2]), lambda bb, i: (bb, i, 0))]
    x, ctx = xs
    return [pl.BlockSpec((1, tm, x.shape[2]), lambda bb, i: (bb, jnp.minimum(i, nlat_blocks - 1), 0)),
            pl.BlockSpec((1, tm, ctx.shape[2]), lambda bb, i: (bb, jnp.maximum(i - nlat_blocks, 0), 0))]


def _token_rows(x_refs, nlat_blocks):
    if len(x_refs) == 1:
        return x_refs[0][0]
    return jnp.where(pl.program_id(1) >= nlat_blocks, x_refs[1][0], x_refs[0][0])


def _norm_modulate(x, nw, mods, k, d):
    rows = x.shape[0] // len(mods)
    parts = [_rms(x[j * rows:(j + 1) * rows], nw) * (1.0 + m[:, (k + 1) * d:(k + 2) * d]) + m[:, k * d:(k + 1) * d]
             for j, m in enumerate(mods)]
    return parts[0] if len(parts) == 1 else jnp.concatenate(parts, axis=0)


def _inproj_kernel(*refs, d, nsrc, nmod, nlat_blocks):
    mod_refs = refs[nsrc:nsrc + nmod]
    nw_ref, w_ref, o_ref = refs[nsrc + nmod:]
    x = _token_rows(refs[:nsrc], nlat_blocks)
    h = _norm_modulate(x, nw_ref[...], [m[0] for m in mod_refs], 0, d).astype(BF16)
    for j in range(P_COLS // INPROJ_TN):
        cs = slice(j * INPROJ_TN, (j + 1) * INPROJ_TN)
        res = _dot(h, w_ref[:, cs])
        if cs.stop <= SEG_GATE[0] + SEG_GATE[1]:
            res = res * 0.5
        o_ref[0, :, cs] = res.astype(BF16)


def _resident(shape):
    return pl.BlockSpec(shape, lambda *_: (0,) * len(shape), pipeline_mode=pl.Buffered(1))


def _layer_resident(arr, *lead):
    rest = arr.shape[len(lead):]
    return pl.BlockSpec((None,) * len(lead) + rest, lambda *_: tuple(lead) + (0,) * len(rest),
                        pipeline_mode=pl.Buffered(1))


def _mod_spec(modsel, l, nlat_blocks):
    return pl.BlockSpec((None, 1) + modsel.shape[2:],
                        lambda bb, i: (l, bb * 2 + jnp.where(i >= nlat_blocks, 1, 0), 0, 0))


def _pair_mod_specs(modsel, l, tiles_per_batch, nlat_blocks):
    def spec(j):
        def index(_, i):
            tile = 2 * i + j
            row = (tile // tiles_per_batch) * 2 + jnp.where(tile % tiles_per_batch >= nlat_blocks, 1, 0)
            return (l, row, 0, 0)
        return pl.BlockSpec((None, 1) + modsel.shape[2:], index)
    return [spec(0), spec(1)]


def _can_pair(b, s):
    return (b * (s // TOKEN_TILE)) % 2 == 0


def _inproj(xs, modsel, nw, w_p, l, nlat_blocks):
    srcs = xs if isinstance(xs, tuple) else (xs,)
    b, d = srcs[0].shape[0], srcs[0].shape[2]
    s = sum(a.shape[1] for a in srcs)
    tm = TOKEN_TILE
    if len(srcs) == 1 and _can_pair(b, s):
        srcs, grid, tm = (xs.reshape(1, b * s, d),), (1, b * s // PAIR_TILE), PAIR_TILE
        tok_specs = [pl.BlockSpec((1, tm, d), lambda bb, i: (bb, i, 0))]
        mod_specs = _pair_mod_specs(modsel, l, s // TOKEN_TILE, nlat_blocks)
    else:
        grid = (b, s // tm)
        tok_specs = _token_specs(xs, tm, nlat_blocks)
        mod_specs = [_mod_spec(modsel, l, nlat_blocks)]
    p = pl.pallas_call(
        functools.partial(_inproj_kernel, d=d, nsrc=len(srcs), nmod=len(mod_specs), nlat_blocks=nlat_blocks),
        grid=grid,
        in_specs=tok_specs + mod_specs + [_layer_resident(nw, l), _layer_resident(w_p, l)],
        out_specs=pl.BlockSpec((1, tm, P_COLS), lambda bb, i: (bb, i, 0)),
        out_shape=jax.ShapeDtypeStruct((grid[0], b * s // grid[0], P_COLS), BF16),
        compiler_params=_cparams(("parallel", "parallel")),
        name="inproj",
    )(*srcs, *([modsel] * len(mod_specs)), nw, w_p)
    return p.reshape(b, s, P_COLS)


def _prep_kernel(acq_ref, ackv_ref, akr_ref, bq_ref, bk_ref, dq_ref, dk_ref,
                 cosa_ref, sina_ref, cosb_ref, sinb_ref, cosd_ref, sind_ref,
                 aqn_ref, akvn_ref, wq_ref, wk_ref, wv_ref, permbq_ref, permbk_ref,
                 permdq_ref, permdk_ref, dqn_ref, dqnp_ref, dkn_ref, dknp_ref,
                 qa_ref, ka_ref, va_ref, qb_ref, kb_ref, qd_ref, kd_ref):
    na = A_HEADS * A_HEAD_PAD
    cos_r, sin_r = cosa_ref[:, A_NOPE:], sina_ref[:, A_NOPE:]

    def rope_heads(x, w_ref, o_ref, scale):
        for h in range(A_HEADS):
            c0 = h * A_HEAD_PAD
            main = _dot(x, w_ref[:, c0:c0 + A_HEAD_PAD])
            part = _dot(x, w_ref[:, na + c0 + A_NOPE:na + c0 + A_HEAD_PAD])
            out = jnp.concatenate([main[:, :A_NOPE], main[:, A_NOPE:] * cos_r + part * sin_r], axis=1)
            o_ref[0, :, c0:c0 + A_HEAD_PAD] = (out * scale).astype(BF16)

    cq = _rms(acq_ref[0].astype(F32), aqn_ref[...]).astype(BF16)
    rope_heads(cq, wq_ref, qa_ref, (A_NOPE + A_ROPE) ** -0.5)
    ckv = _rms(ackv_ref[0].astype(F32), akvn_ref[...]).astype(BF16)
    rope_heads(jnp.concatenate([ckv, akr_ref[0]], axis=1), wk_ref, ka_ref, 1.0)
    va_ref[0] = _dot(ckv, wv_ref[...]).astype(BF16)

    nrep = B_HEADS * B_HD // LANE
    cosb, sinb = cosb_ref[...], sinb_ref[...]
    bq = bq_ref[0]
    qb = (bq.astype(F32) * jnp.concatenate([cosb] * nrep, axis=1)
          + _dot(bq, permbq_ref[...]) * jnp.concatenate([sinb] * nrep, axis=1))
    qb_ref[0] = (qb * (B_HD ** -0.5)).astype(BF16)
    bk = bk_ref[0]
    kb_ref[0] = (bk.astype(F32) * cosb + _dot(bk, permbk_ref[...]) * sinb).astype(BF16)

    cosd, sind = cosd_ref[...], sind_ref[...]

    def norm_rope(x_ref, perm_ref, nw_ref, nwp_ref, heads):
        x = x_ref[0]
        xp = _dot(x, perm_ref[...])
        xf = x.astype(F32)
        wc, ws = nw_ref[...] * cosd, nwp_ref[...] * sind
        parts = []
        for h in range(heads):
            hs = slice(h * D_HD, (h + 1) * D_HD)
            xh = xf[:, hs]
            inv = lax.rsqrt(jnp.mean(xh * xh, axis=-1, keepdims=True) + EPS)
            parts.append(inv * (xh * wc + xp[:, hs] * ws))
        return jnp.concatenate(parts, axis=1)

    qd_ref[0] = (norm_rope(dq_ref, permdq_ref, dqn_ref, dqnp_ref, D_HEADS) * (D_HD ** -0.5)).astype(BF16)
    kd_ref[0] = norm_rope(dk_ref, permdk_ref, dkn_ref, dknp_ref, D_KV_HEADS).astype(BF16)


def _pspec(seg, tm):
    off, width = seg
    return pl.BlockSpec((1, tm, width), lambda bb, i: (bb, i, off // width))


def _prep(p, tabs, wts, l):
    b, s, _ = p.shape
    tm = TOKEN_TILE
    segs = [SEG_ACQ, SEG_ACKV, SEG_AKR, SEG_BQ, SEG_BK, SEG_DQ, SEG_DK]
    tab_specs = [pl.BlockSpec((tm, t.shape[1]), lambda bb, i: (i, 0)) for t in tabs]
    wt_specs = [_layer_resident(w, l) if stacked else _resident(w.shape) for w, stacked in wts]
    wts = [w for w, _ in wts]
    widths = [A_HEADS * A_HEAD_PAD, A_HEADS * A_HEAD_PAD, A_HEADS * A_V,
              B_HEADS * B_HD, B_KV_HEADS * B_HD, D_HEADS * D_HD, D_KV_HEADS * D_HD]
    return pl.pallas_call(
        _prep_kernel,
        grid=(b, s // tm),
        in_specs=[_pspec(sg, tm) for sg in segs] + tab_specs + wt_specs,
        out_specs=[pl.BlockSpec((1, tm, w), lambda bb, i: (bb, i, 0)) for w in widths],
        out_shape=[jax.ShapeDtypeStruct((b, s, w), BF16) for w in widths],
        compiler_params=_cparams(("parallel", "parallel")),
        name="prep",
    )(*([p] * len(segs)), *tabs, *wts)


def _attn_kernel(q_ref, k_ref, v_ref, o_ref, *, dk, dv, kv_group):
    rows = min(ATTN_CHAIN_ROWS, q_ref.shape[1])
    for r in range(q_ref.shape[1] // rows):
        rs = slice(r * rows, (r + 1) * rows)
        for c in range(ATTN_HEADS_PER_STEP):
            kc = c // kv_group
            s = _dot_nt(q_ref[0, rs, c * dk:(c + 1) * dk], k_ref[0, :, kc * dk:(kc + 1) * dk])
            m = jnp.max(s, axis=-1, keepdims=True)
            p = jnp.exp(s - m)
            l = jnp.sum(p, axis=-1, keepdims=True)
            o = _dot(p.astype(BF16), v_ref[0, :, kc * dv:(kc + 1) * dv])
            o_ref[0, rs, c * dv:(c + 1) * dv] = (o / l).astype(o_ref.dtype)


def _attention(q, k, v, *, heads, kv_group, dk, dv, vcol0, t, cn, with_ctx, name):
    b, s, _ = q.shape
    hps = ATTN_HEADS_PER_STEP
    nkv = hps // kv_group
    kern = functools.partial(_attn_kernel, dk=dk, dv=dv, kv_group=kv_group)
    sem = _cparams(("parallel", "parallel", "parallel"))

    def call(tq, nq, row0, krows, krow0, nm):
        return pl.pallas_call(
            kern,
            grid=(b, heads // hps, nq // tq),
            in_specs=[
                pl.BlockSpec((1, tq, hps * dk), lambda bb, h, i: (bb, row0 + i, h)),
                pl.BlockSpec((1, krows, nkv * dk), lambda bb, h, i: (bb, krow0, h)),
                pl.BlockSpec((1, krows, nkv * dv), lambda bb, h, i: (bb, krow0, vcol0 + h)),
            ],
            out_specs=pl.BlockSpec((1, tq, hps * dv), lambda bb, h, i: (bb, i, h)),
            out_shape=jax.ShapeDtypeStruct((b, nq, heads * dv), BF16),
            compiler_params=sem,
            name=nm,
        )(q, k, v)

    y = call(ATTN_TQ, t, 0, s, 0, name)
    if with_ctx:
        return y, call(cn, cn, t // cn, cn, t // cn, name + "_ctx")
    return y


def _swa_bias():
    band = SWA_BLOCK + 2 * WINDOW
    r = np.arange(SWA_BLOCK)[:, None]
    j = np.arange(band)[None, :]
    out = []
    for off in (0, WINDOW, band - SWA_BLOCK):
        out.append(np.where(np.abs(off + r - j) <= WINDOW, 0.0, NEG_INF))
    return np.stack(out).astype(np.float32)


def _swa_kernel(sink_ref, q_ref, k_ref, v_ref, bias_ref, o_ref, s_ref, p_ref, e_ref, *, t, cn, layer):
    n = pl.program_id(1)
    blk = SWA_BLOCK
    nlat = t // blk
    grp = B_HEADS // B_KV_HEADS
    band = blk + 2 * WINDOW
    rt = SWA_ROW_TILE

    def run(is_ctx):
        q = q_ref[0]
        k, v = k_ref[0, t:t + cn, :], v_ref[0, t:t + cn, :]
        if not is_ctx:
            start = pl.multiple_of(jnp.clip(n * blk - WINDOW, 0, t - band), WINDOW)
            k = jnp.concatenate([k, k_ref[0, pl.ds(start, band), :]], axis=0)
            v = jnp.concatenate([v, v_ref[0, pl.ds(start, band), :]], axis=0)
        nk = k.shape[0]
        halves = []
        qlane = lax.broadcasted_iota(jnp.int32, (blk, LANE), 1)
        for hk in range(B_KV_HEADS):
            mine = ((qlane // B_HD) == hk).astype(BF16)
            q4 = jnp.concatenate([q[:, gi * LANE:(gi + 1) * LANE] * mine for gi in range(grp)], axis=0)
            s_ref[hk, :, :nk] = _dot_nt(q4, k)

            for i in range(grp * blk // rt):
                rs = slice(i * rt, (i + 1) * rt)
                sink = sink_ref[layer, hk * grp + i // (blk // rt)]
                s = s_ref[hk, rs, :nk]
                if not is_ctx:
                    br = (i % (blk // rt)) * rt
                    s = s + bias_ref[0, br:br + rt, :]
                m = jnp.maximum(jnp.max(s, axis=-1, keepdims=True), sink)
                p = jnp.exp(s - m)
                l = jnp.sum(p, axis=-1, keepdims=True) + jnp.exp(sink - m)
                p_ref[hk, rs, :nk] = p.astype(BF16)
                e_ref[hk, rs, :] = jnp.broadcast_to(l, (rt, LANE))
            halves.append(_dot(p_ref[hk, :, :nk], v) / e_ref[hk])
        lane = lax.broadcasted_iota(jnp.int32, (blk, LANE), 1)
        outs = [jnp.where(lane < B_HD, halves[0][gi * blk:(gi + 1) * blk], halves[1][gi * blk:(gi + 1) * blk])
                for gi in range(grp)]
        o_ref[0] = jnp.concatenate(outs, axis=1).astype(o_ref.dtype)

    @pl.when(n < nlat)
    def _():
        run(False)

    @pl.when(n >= nlat)
    def _():
        run(True)


def _swa(sink, q, k, p, l, *, t, cn, n_blocks):
    b, s, _ = q.shape
    nlat = t // SWA_BLOCK
    rows = (B_HEADS // B_KV_HEADS) * SWA_BLOCK
    band_bias = _swa_bias()
    bias = jnp.asarray(np.concatenate([np.zeros(band_bias.shape[:2] + (cn,), np.float32), band_bias], axis=2))
    return pl.pallas_call(
        functools.partial(_swa_kernel, t=t, cn=cn, layer=l),
        grid=(b, n_blocks),
        in_specs=[
            pl.BlockSpec(memory_space=pltpu.SMEM),
            pl.BlockSpec((1, SWA_BLOCK, B_HEADS * B_HD), lambda bb, n: (bb, n, 0)),
            pl.BlockSpec((1, s, B_KV_HEADS * B_HD), lambda bb, n: (bb, 0, 0)),
            pl.BlockSpec((1, s, SEG_BV[1]), lambda bb, n: (bb, 0, SEG_BV[0] // SEG_BV[1])),
            pl.BlockSpec((1,) + bias.shape[1:],
                         lambda bb, n: (jnp.where(n == 0, 0, jnp.where(n >= nlat - 1, 2, 1)), 0, 0)),
        ],
        out_specs=pl.BlockSpec((1, SWA_BLOCK, B_HEADS * B_HD), lambda bb, n: (bb, n, 0)),
        out_shape=jax.ShapeDtypeStruct((b, n_blocks * SWA_BLOCK, B_HEADS * B_HD), BF16),
        scratch_shapes=[pltpu.VMEM((B_KV_HEADS, rows, bias.shape[2]), F32),
                        pltpu.VMEM((B_KV_HEADS, rows, bias.shape[2]), BF16),
                        pltpu.VMEM((B_KV_HEADS, rows, LANE), F32)],
        compiler_params=_cparams(("parallel", "parallel")),
        name="swa",
    )(sink, q, k, p, bias)


GLA_BLOCK = 256
GLA_SAFE_DECAY = 60.0


def _log_decay(lr, w, bias):
    z = _dot(lr, w) + bias
    return (jnp.minimum(z, 0.0) - jnp.log(1.0 + jnp.exp(-jnp.abs(z)))) / C_DECAY_TEMP


def _cum_decay(la, tri):
    hi = la.astype(BF16)
    lo = (la - hi.astype(F32)).astype(BF16)
    return _dot(tri, hi) + _dot(tri, lo)


def _gla_kernel(qf_ref, kf_ref, vf_ref, lrf_ref, qb_ref, kb_ref, vb_ref, lrb_ref,
                wdec_ref, bdec_ref, sel_ref, of_ref, ob_ref, st_ref, e_ref):
    n = GLA_BLOCK
    pair = LANE // C_DK
    npair = C_HEADS // pair
    qscale = C_DK ** -0.5
    refs = ((qf_ref, kf_ref, vf_ref, lrf_ref), (qb_ref, kb_ref, vb_ref, lrb_ref))
    outs = (of_ref, ob_ref)

    @pl.when(pl.program_id(1) == 0)
    def _():
        st_ref[...] = jnp.zeros_like(st_ref)

    r_i = lax.broadcasted_iota(jnp.int32, (n, n), 0)
    c_i = lax.broadcasted_iota(jnp.int32, (n, n), 1)
    keep = (c_i <= r_i, c_i >= r_i)
    lane = lax.broadcasted_iota(jnp.int32, (1, LANE), 1)
    head_mask = [(lane // C_DK) == hh for hh in range(pair)]

    cums, tots = [], []
    for d in range(2):
        la = _log_decay(refs[d][3][0], wdec_ref[d], bdec_ref[d])
        cum = _cum_decay(la, keep[d].astype(BF16))
        cums.append(cum)
        tots.append(cum[n - 1:n] if d == 0 else cum[0:1])
    safe = jnp.maximum(jnp.max(-tots[0]), jnp.max(-tots[1])) <= GLA_SAFE_DECAY

    @pl.when(safe)
    def _():
        for d in range(2):
            q_ref, k_ref, v_ref, _ = refs[d]
            cum, tot = cums[d], tots[d]
            half = 0.5 * tot
            a = cum - half
            eh = jnp.exp(half)
            qr = q_ref[0].astype(F32) * qscale * jnp.exp(a)
            kr = k_ref[0].astype(F32) * jnp.exp(-a)
            kr_b, kd_b = kr.astype(BF16), (kr * eh).astype(BF16)
            qe = qr * eh
            v, st = v_ref[0], st_ref[d]
            o_parts, u_parts = [], []
            for p in range(npair):
                ls = slice(p * LANE, (p + 1) * LANE)
                st2 = st[:, ls].astype(BF16)
                upd = []
                for hh in range(pair):
                    h = p * pair + hh
                    vh = v[:, h * C_DV:(h + 1) * C_DV]
                    qr_h = jnp.where(head_mask[hh], qr[:, ls], 0.0).astype(BF16)
                    qe_h = jnp.where(head_mask[hh], qe[:, ls], 0.0).astype(BF16)
                    pm = jnp.where(keep[d], _dot_nt(qr_h, kr_b[:, ls]), 0.0).astype(BF16)
                    o_parts.append(_dot(pm, vh) + _dot_nt(qe_h, st2))
                    upd.append(_dot_tn(vh, kd_b[:, ls]))
                u_parts.append(jnp.where(head_mask[0], upd[0], upd[1]))
            outs[d][0] = jnp.concatenate(o_parts, axis=1)
            st_ref[d] = st * jnp.exp(tot) + jnp.concatenate(u_parts, axis=1)

    @pl.when(jnp.logical_not(safe))
    def _():
        c = C_CHUNK
        tile = 16
        nsub = n // c
        r64 = lax.broadcasted_iota(jnp.int32, (c, c), 0)
        c64 = lax.broadcasted_iota(jnp.int32, (c, c), 1)
        tri64 = ((c64 <= r64).astype(BF16), (c64 >= r64).astype(BF16))
        for d in range(2):
            q_ref, k_ref, v_ref, lr_ref = refs[d]
            e_ref[...] = jnp.zeros_like(e_ref)

            def sub(i, carry, d=d, q_ref=q_ref, k_ref=k_ref, v_ref=v_ref, lr_ref=lr_ref):
                rows = pl.ds(pl.multiple_of((i if d == 0 else nsub - 1 - i) * c, c), c)
                q = q_ref[0, rows, :].astype(F32) * qscale
                k = k_ref[0, rows, :].astype(F32)
                v = v_ref[0, rows, :]
                cum = _cum_decay(_log_decay(lr_ref[0, rows, :], wdec_ref[d], bdec_ref[d]), tri64[d])
                tot = cum[c - 1:c] if d == 0 else cum[0:1]
                qe = (q * jnp.exp(cum)).astype(BF16)
                kd = (k * jnp.exp(tot - cum)).astype(BF16)
                for p in range(npair):
                    ls = slice(p * LANE, (p + 1) * LANE)
                    q2, k2, c2 = q[:, ls], k[:, ls], cum[:, ls]
                    for s in range(c):
                        bt = (s // tile) * tile
                        t0, t1 = (bt, c) if d == 0 else (0, bt + tile)
                        e = q2[t0:t1] * k2[s:s + 1] * jnp.exp(c2[t0:t1] - c2[s:s + 1])
                        ridx = lax.broadcasted_iota(jnp.int32, (t1 - t0, LANE), 0) + t0
                        causal = (ridx >= s) if d == 0 else (ridx <= s)
                        e_ref[p * c + t0:p * c + t1, s * LANE:(s + 1) * LANE] = (
                            jnp.where(causal, e, 0.0).astype(BF16))
                pm = _dot(e_ref[...], sel_ref[...])
                st = st_ref[d]
                o_parts, u_parts = [], []
                for h in range(C_HEADS):
                    p, hh = divmod(h, pair)
                    ph = pm[p * c:(p + 1) * c, hh * c:(hh + 1) * c].astype(BF16)
                    vh = v[:, h * C_DV:(h + 1) * C_DV]
                    hs = slice(h * C_DK, (h + 1) * C_DK)
                    o_parts.append(_dot(ph, vh) + _dot_nt(qe[:, hs], st[:, hs].astype(BF16)))
                    u_parts.append(_dot_tn(vh, kd[:, hs]))
                outs[d][0, rows, :] = jnp.concatenate(o_parts, axis=1)
                st_ref[d] = st * jnp.exp(tot) + jnp.concatenate(u_parts, axis=1)
                return carry

            lax.fori_loop(0, nsub, sub, 0)


def _gla_sel():
    pair = LANE // C_DK
    sel = np.zeros((C_CHUNK, pair, C_DK, pair, C_CHUNK), np.float32)
    for s in range(C_CHUNK):
        for h in range(pair):
            sel[s, h, :, h, s] = 1.0
    return jnp.asarray(sel.reshape(C_CHUNK * LANE, pair * C_CHUNK), BF16)


def _gla(p, wdec, bdec, sel, l, *, t, cn):
    b, s, _ = p.shape
    n = GLA_BLOCK
    nbl, nbc = t // n, cn // n

    def fwd(j):
        return jnp.where(j < nbc, nbl + j, j - nbc)

    def bwd(j):
        return nbl + nbc - 1 - j

    def specs(order):
        return [pl.BlockSpec((1, n, sg[1]), lambda bb, j, sg=sg: (bb, order(j), sg[0] // sg[1]))
                for sg in (SEG_CQ, SEG_CK, SEG_CV, SEG_CLR)]

    width = C_HEADS * C_DV
    return pl.pallas_call(
        _gla_kernel,
        grid=(b, nbl + nbc),
        in_specs=specs(fwd) + specs(bwd) + [
            _layer_resident(wdec, l), _layer_resident(bdec, l), _resident(sel.shape),
        ],
        out_specs=[pl.BlockSpec((1, n, width), lambda bb, j: (bb, fwd(j), 0)),
                   pl.BlockSpec((1, n, width), lambda bb, j: (bb, bwd(j), 0))],
        out_shape=[jax.ShapeDtypeStruct((b, s, width), F32)] * 2,
        scratch_shapes=[pltpu.VMEM((2, C_DV, C_HEADS * C_DK), F32),
                        pltpu.VMEM(((C_HEADS * C_DK // LANE) * C_CHUNK, C_CHUNK * LANE), BF16)],
        compiler_params=_cparams(("parallel", "arbitrary")),
        name="gla",
    )(*([p] * 8), wdec, bdec, sel)


def _merge_kernel(*refs, d, nsrc, nlat_blocks):
    x_refs, ya_refs, yd_refs = (refs[i * nsrc:(i + 1) * nsrc] for i in range(3))
    (mod_ref, yb_ref, of_ref, ob_ref, r_ref, g_ref,
     gnw_ref, wba_ref, wbb_ref, wbc_ref, wbd_ref, wo_ref, o_ref) = refs[3 * nsrc:]
    o = of_ref[0] + ob_ref[0]
    r = r_ref[0].astype(F32)
    yg = jnp.concatenate([_rms(o[:, h * C_DV:(h + 1) * C_DV], gnw_ref[...]) for h in range(C_HEADS)], axis=1)
    yg = (yg * (r * jax.nn.sigmoid(r))).astype(BF16)
    branches = ((_token_rows(ya_refs, nlat_blocks), wba_ref), (yb_ref[0], wbb_ref), (yg, wbc_ref),
                (_token_rows(yd_refs, nlat_blocks), wbd_ref))
    mixed = []
    for c in range(d // MERGE_TN):
        cs = slice(c * MERGE_TN, (c + 1) * MERGE_TN)
        acc_z = acc_t = None
        for n, (y, w_ref) in enumerate(branches):
            z = _dot(y, w_ref[:, cs])
            tz = jnp.tanh(g_ref[0, :, n * d + c * MERGE_TN:n * d + (c + 1) * MERGE_TN].astype(F32)) * z
            acc_z = z if acc_z is None else acc_z + z
            acc_t = tz if acc_t is None else acc_t + tz
        mixed.append((0.5 * (acc_z + acc_t)).astype(BF16))
    out = _dot(jnp.concatenate(mixed, axis=1), wo_ref[...])
    o_ref[0] = _token_rows(x_refs, nlat_blocks) + mod_ref[0][:, 2 * d:3 * d] * out


def _merge(xs, modsel, ya, yb, o_f, o_b, p, yd, gnw, w_branch, wb_b, wo, l, *, tm, n_blocks, nlat_blocks):
    assert isinstance(xs, tuple) == isinstance(ya, tuple) == isinstance(yd, tuple)
    streams = [s if isinstance(s, tuple) else (s,) for s in (xs, ya, yd)]
    srcs = [a for s in streams for a in s]
    b, d = srcs[0].shape[0], srcs[0].shape[2]

    def tok(w, col=0):
        return pl.BlockSpec((1, tm, w), lambda bb, i: (bb, i, col))

    w_specs = [_layer_resident(w_branch, l, 0), _layer_resident(wb_b, l),
               _layer_resident(w_branch, l, 2), _layer_resident(w_branch, l, 3), _layer_resident(wo, l)]
    return pl.pallas_call(
        functools.partial(_merge_kernel, d=d, nsrc=len(streams[0]), nlat_blocks=nlat_blocks),
        grid=(b, n_blocks),
        in_specs=[sp for s in (xs, ya, yd) for sp in _token_specs(s, tm, nlat_blocks)] + [
            _mod_spec(modsel, l, nlat_blocks),
            tok(yb.shape[2]), tok(o_f.shape[2]), tok(o_b.shape[2]),
            tok(SEG_CR[1], SEG_CR[0] // SEG_CR[1]),
            tok(SEG_GATE[1], 0),
            _layer_resident(gnw, l), *w_specs,
        ],
        out_specs=tok(d),
        out_shape=jax.ShapeDtypeStruct((b, n_blocks * tm, d), F32),
        compiler_params=_cparams(("parallel", "parallel")),
        name="merge",
    )(*srcs, modsel, yb, o_f, o_b, p, p, gnw, w_branch, wb_b, w_branch, w_branch, wo)


def _mlp_kernel(x_ref, *refs, d, nmod, final):
    mods = [m[0] for m in refs[:nmod]]
    nw_ref, w1_ref, w2_ref, fnw_ref, o_ref = refs[nmod:]
    x = x_ref[0]
    h = _norm_modulate(x, nw_ref[...], mods, 3, d)
    a = jnp.maximum(_dot(h.astype(BF16), w1_ref[...]), 0.0)
    out = _dot((a * a).astype(BF16), w2_ref[...])
    rows = x.shape[0] // nmod
    ys = [x[j * rows:(j + 1) * rows] + m[:, 5 * d:6 * d] * out[j * rows:(j + 1) * rows] for j, m in enumerate(mods)]
    y = ys[0] if nmod == 1 else jnp.concatenate(ys, axis=0)
    o_ref[0] = _rms(y, fnw_ref[...]) if final else y


def _mlp(xs, modsel, nw, w1, w2, fnw, l, *, tm, n_blocks, nlat_blocks, out_rows, final):
    b, s, d = xs.shape
    out_b = b
    if out_rows == s and tm == TOKEN_TILE and _can_pair(b, s):
        xs, out_b, out_rows = xs.reshape(1, b * s, d), 1, b * s
        grid, tm = (1, b * s // PAIR_TILE), PAIR_TILE
        mod_specs = _pair_mod_specs(modsel, l, s // TOKEN_TILE, nlat_blocks)
    else:
        grid = (b, n_blocks)
        mod_specs = [_mod_spec(modsel, l, nlat_blocks)]
    y = pl.pallas_call(
        functools.partial(_mlp_kernel, d=d, nmod=len(mod_specs), final=final),
        grid=grid,
        in_specs=[pl.BlockSpec((1, tm, d), lambda bb, i: (bb, i, 0))] + mod_specs + [
            _layer_resident(nw, l), _layer_resident(w1, l), _layer_resident(w2, l), _resident(fnw.shape)],
        out_specs=pl.BlockSpec((1, tm, d), lambda bb, i: (bb, i, 0)),
        out_shape=jax.ShapeDtypeStruct((out_b, out_rows, d), F32),
        compiler_params=_cparams(("parallel", "parallel")),
        name="mlp",
    )(xs, *([modsel] * len(mod_specs)), nw, w1, w2, fnw)
    return y.reshape(b, -1, d)


def _pad_cols(w, n):
    return jnp.concatenate([w, jnp.zeros(w.shape[:-1] + (n,), w.dtype)], axis=-1)


def _layout_w_in(w):
    a0 = 0
    b0 = a0 + A_Q_RANK + A_KV_RANK + A_ROPE
    c0 = b0 + (B_HEADS + 2 * B_KV_HEADS) * B_HD
    d0 = c0 + 2 * C_HEADS * C_DK + 2 * C_HEADS * C_DV + 2 * C_DECAY_RANK
    g0 = d0 + (D_HEADS + 2 * D_KV_HEADS) * D_HD

    def cols(lo, n):
        return w[..., lo:lo + n]

    cq = c0
    ck = cq + C_HEADS * C_DK
    cv = ck + C_HEADS * C_DK
    cr = cv + C_HEADS * C_DV
    clr = cr + C_HEADS * C_DV
    parts = [
        cols(g0, SEG_GATE[1]),
        cols(d0, SEG_DQ[1] + SEG_DK[1] + SEG_DV[1]),
        *[cols(b0 + h * B_HD, B_HD) for h in _swa_head_order()],
        cols(b0 + SEG_BQ[1], SEG_BK[1] + SEG_BV[1]),
        cols(a0, A_Q_RANK + A_KV_RANK),
        _pad_cols(cols(a0 + A_Q_RANK + A_KV_RANK, A_ROPE), SEG_AKR[1] - A_ROPE),
        cols(cq, SEG_CQ[1]), cols(cv, SEG_CV[1]), cols(cr, SEG_CR[1]), cols(ck, SEG_CK[1]),
        _pad_cols(cols(clr, 2 * C_DECAY_RANK), P_COLS - SEG_CLR[0] - 2 * C_DECAY_RANK),
    ]
    return jnp.concatenate(parts, axis=-1).astype(BF16)


def _layout_mla(w_uq, w_ukv):
    zq = jnp.zeros((A_Q_RANK, A_HEAD_PAD - A_NOPE - A_ROPE), w_uq.dtype)
    hq = A_NOPE + A_ROPE
    wq = jnp.concatenate([jnp.concatenate([w_uq[:, h * hq:(h + 1) * hq], zq], axis=1)
                          for h in range(A_HEADS)], axis=1)
    hk = A_NOPE + A_V
    zk = jnp.zeros((A_KV_RANK, A_HEAD_PAD - A_NOPE), w_ukv.dtype)
    wk_top = jnp.concatenate([jnp.concatenate([w_ukv[:, h * hk:h * hk + A_NOPE], zk], axis=1)
                              for h in range(A_HEADS)], axis=1)
    place = np.zeros((SEG_AKR[1], A_HEADS * A_HEAD_PAD), np.float32)
    for h in range(A_HEADS):
        place[np.arange(A_ROPE), h * A_HEAD_PAD + A_NOPE + np.arange(A_ROPE)] = 1.0
    wk = jnp.concatenate([wk_top, jnp.asarray(place)], axis=0)
    wv = jnp.concatenate([w_ukv[:, h * hk + A_NOPE:(h + 1) * hk] for h in range(A_HEADS)], axis=1)
    lanes = np.arange(A_HEADS * A_HEAD_PAD)
    rot = ((lanes % A_HEAD_PAD) >= A_NOPE) & ((lanes % A_HEAD_PAD) < A_NOPE + A_ROPE)
    idx = _partner_lanes(lanes.size, A_ROPE // 4)

    def with_partner(w):
        return jnp.concatenate([w, jnp.where(jnp.asarray(rot)[None, :], w[:, idx], 0.0)], axis=1)

    return with_partner(wq).astype(BF16), with_partner(wk).astype(BF16), wv.astype(BF16)


def _partner_lanes(width, blk):
    j = np.arange(width)
    return np.where((j % (2 * blk)) < blk, j + blk, j - blk)


def _perm_matrix(width, blk):
    m = np.zeros((width, width), np.float32)
    m[_partner_lanes(width, blk), np.arange(width)] = 1.0
    return m


def _swa_head_order():
    grp = B_HEADS // B_KV_HEADS
    return [h for g in range(grp) for h in (g, grp + g)]


def _layout_w_branch_b(w):
    return jnp.concatenate([w[h * B_HD:(h + 1) * B_HD] for h in _swa_head_order()], axis=0)


def _layout_decay(w_decay):
    z = jnp.zeros((C_DECAY_RANK, w_decay.shape[-1]), w_decay.dtype)
    tail = jnp.zeros((SEG_CLR[1] - 2 * C_DECAY_RANK, w_decay.shape[-1]), w_decay.dtype)
    return jnp.stack([jnp.concatenate([w_decay[0], z, tail], axis=0),
                      jnp.concatenate([z, w_decay[1], tail], axis=0)]).astype(BF16)


def kernel(x, c, ctx, c_ctx, ada_w, ada_b, norm1_w, norm2_w, w_in, mla_q_norm_w, mla_w_uq, mla_kv_norm_w, mla_w_ukv, swa_sink, gla_w_decay, gla_b_decay, gla_norm_w, gqa_q_norm_w, gqa_k_norm_w, w_branch, w_o, mlp_w1, mlp_w2, final_norm_w):
    b, t, d = x.shape
    cn = ctx.shape[1]
    depth = ada_w.shape[0]
    tm = TOKEN_TILE
    assert t % tm == 0 and cn % tm == 0 and t % GRID_W == 0 and t >= SWA_BLOCK + 2 * WINDOW
    assert t % ATTN_TQ == 0 and t % cn == 0 and t % GLA_BLOCK == 0 and cn % GLA_BLOCK == 0
    nlat, nall = t // tm, (t + cn) // tm

    rows = 16
    cs = jnp.concatenate([c, c_ctx[None], jnp.zeros((rows - b - 1, d), F32)], axis=0)
    mod = _ada(cs, ada_w, ada_b)
    modsel = jnp.stack([mod[:, :b], jnp.broadcast_to(mod[:, b:b + 1], (depth, b, 6 * d))], axis=2)
    modsel = modsel.reshape(depth, b * 2, 1, 6 * d)

    tabs = (_rope_tables(t, cn, A_HEAD_PAD, A_NOPE, A_ROPE)
            + tuple(jnp.tile(tb, (1, LANE // B_HD)) for tb in _rope_tables(t, cn, B_HD, 0, B_HD))
            + _rope_tables(t, cn, D_HD, 0, D_HD))
    sel = _gla_sel()
    perms = (jnp.asarray(_perm_matrix(B_HEADS * B_HD, B_HD // 4), BF16),
             jnp.asarray(_perm_matrix(B_KV_HEADS * B_HD, B_HD // 4), BF16),
             jnp.asarray(_perm_matrix(D_HEADS * D_HD, D_HD // 4), BF16),
             jnp.asarray(_perm_matrix(D_KV_HEADS * D_HD, D_HD // 4), BF16))
    d_partner = _partner_lanes(D_HD, D_HD // 4)

    w_p = _layout_w_in(w_in)
    wbr, wo = w_branch.astype(BF16), w_o.astype(BF16)
    wb_b = jax.vmap(_layout_w_branch_b)(w_branch[:, 1]).astype(BF16)
    w1, w2 = mlp_w1.astype(BF16), mlp_w2.astype(BF16)
    wq, wk, wv = jax.vmap(_layout_mla)(mla_w_uq, mla_w_ukv)
    wdec, bdec = jax.vmap(_layout_decay)(gla_w_decay), gla_b_decay[:, :, None, :]

    def rows(w):
        return w[:, None, :]

    prep_wts = ([(rows(mla_q_norm_w), True), (rows(mla_kv_norm_w), True), (wq, True), (wk, True), (wv, True)]
                + [(pm, False) for pm in perms]
                + [(rows(gqa_q_norm_w), True), (rows(gqa_q_norm_w[:, d_partner]), True),
                   (rows(gqa_k_norm_w), True), (rows(gqa_k_norm_w[:, d_partner]), True)])
    n1, n2, gnw = rows(norm1_w), rows(norm2_w), rows(gla_norm_w)

    xs = (x, ctx)
    for l in range(depth):
        last = l == depth - 1
        p = _inproj(xs, modsel, n1, w_p, l, nlat)
        qa, ka, va, qb, kb, qd, kd = _prep(p, tabs, prep_wts, l)
        ya = _attention(qa, ka, va, heads=A_HEADS, kv_group=1, dk=A_HEAD_PAD, dv=A_V, vcol0=0,
                        t=t, cn=cn, with_ctx=not last, name="attn_a")
        yd = _attention(qd, kd, p, heads=D_HEADS, kv_group=D_HEADS // D_KV_HEADS, dk=D_HD, dv=D_HD,
                        vcol0=SEG_DV[0] // SEG_DV[1], t=t, cn=cn, with_ctx=not last, name="attn_d")
        yb = _swa(swa_sink, qb, kb, p, l, t=t, cn=cn,
                  n_blocks=(t if last else t + cn) // SWA_BLOCK)
        o_f, o_b = _gla(p, wdec, bdec, sel, l, t=t, cn=cn)
        if last and isinstance(xs, tuple):
            xs = xs[0]
        tile = dict(tm=LAST_TILE, n_blocks=t // LAST_TILE, nlat_blocks=t // LAST_TILE) if last else dict(
            tm=tm, n_blocks=nall, nlat_blocks=nlat)
        xs = _merge(xs, modsel, ya, yb, o_f, o_b, p, yd, gnw, wbr, wb_b, wo, l, **tile)
        xs = _mlp(xs, modsel, n2, w1, w2, final_norm_w[None], l,
                  out_rows=t if last else t + cn, final=last, **tile)
    return xs
```

```python
import functools

import numpy as np
import jax
import jax.numpy as jnp
from jax import lax
from jax.experimental import pallas as pl
from jax.experimental.pallas import tpu as pltpu

F32 = jnp.float32
BF16 = jnp.bfloat16

GRID_W = 64
EPS = 1e-6
ROPE_BASE = 10000.0
NEG_INF = -1e30

A_HEADS, A_NOPE, A_ROPE, A_V, A_Q_RANK, A_KV_RANK = 4, 128, 64, 128, 256, 128
A_HEAD_PAD = 256
B_HEADS, B_KV_HEADS, B_HD, WINDOW = 8, 2, 64, 128
SWA_BLOCK = 256
SWA_ROW_TILE = 32
C_HEADS, C_DK, C_DV, C_DECAY_RANK, C_DECAY_TEMP, C_CHUNK = 4, 64, 128, 16, 16.0, 64
D_HEADS, D_KV_HEADS, D_HD = 4, 2, 128
N_BRANCH, BRANCH_W = 4, 512

LANE = 128
TOKEN_TILE = 256
LAST_TILE = 512
PAIR_TILE = 2 * TOKEN_TILE
ATTN_TQ = 1024
ATTN_CHAIN_ROWS = 512
ATTN_HEADS_PER_STEP = 4
INPROJ_TN = 2048
MERGE_TN = 256
VMEM_LIMIT = 56 * 1024 * 1024

P_COLS = 8192
SEG_GATE = (0, 4096)
SEG_DQ, SEG_DK, SEG_DV = (4096, 512), (4608, 256), (4864, 256)
SEG_BQ, SEG_BK, SEG_BV = (5120, 512), (5632, 128), (5760, 128)
SEG_ACQ, SEG_ACKV, SEG_AKR = (5888, 256), (6144, 128), (6272, 128)
SEG_CQ, SEG_CV, SEG_CR, SEG_CK, SEG_CLR = (6400, 256), (6656, 512), (7168, 512), (7680, 256), (7936, 128)


def _cparams(sem):
    return pltpu.CompilerParams(dimension_semantics=sem, vmem_limit_bytes=VMEM_LIMIT)


def _rms(xf, w):
    return xf * lax.rsqrt(jnp.mean(xf * xf, axis=-1, keepdims=True) + EPS) * w


def _dot(a, b):
    return jnp.dot(a, b, preferred_element_type=F32)


def _dot_nt(a, b):
    return lax.dot_general(a, b, (((1,), (1,)), ((), ())), preferred_element_type=F32)


def _dot_tn(a, b):
    return lax.dot_general(a, b, (((0,), (0,)), ((), ())), preferred_element_type=F32)


def _rope_tables(t, cn, width, offset, rot):
    half = rot // 2
    q = half // 2
    inv = ROPE_BASE ** (-np.arange(q, dtype=np.float64) / q)
    pos = np.arange(t)
    cos = np.ones((t + cn, width), np.float64)
    sin = np.zeros((t + cn, width), np.float64)
    for part, p in enumerate((pos // GRID_W, pos % GRID_W)):
        ang = p[:, None] * inv[None, :]
        lo = offset + part * half
        cos[:t, lo:lo + half] = np.concatenate([np.cos(ang), np.cos(ang)], axis=1)
        sin[:t, lo:lo + half] = np.concatenate([-np.sin(ang), np.sin(ang)], axis=1)
    return jnp.asarray(cos, F32), jnp.asarray(sin, F32)


def _ada_kernel(cs_ref, w_ref, b_ref, o_ref):
    cs = cs_ref[...]
    s = cs * jax.nn.sigmoid(cs)
    o_ref[0] = _dot(s.astype(BF16), w_ref[0].astype(BF16)) + b_ref[0]


def _ada(cs, ada_w, ada_b):
    nl, d, n = ada_w.shape
    tn = 1536
    rows = cs.shape[0]
    return pl.pallas_call(
        _ada_kernel,
        grid=(nl, n // tn),
        in_specs=[
            pl.BlockSpec((rows, d), lambda l, j: (0, 0)),
            pl.BlockSpec((1, d, tn), lambda l, j: (l, 0, j)),
            pl.BlockSpec((1, 1, tn), lambda l, j: (l, 0, j)),
        ],
        out_specs=pl.BlockSpec((1, rows, tn), lambda l, j: (l, 0, j)),
        out_shape=jax.ShapeDtypeStruct((nl, rows, n), F32),
        compiler_params=_cparams(("parallel", "parallel")),
        name="ada",
    )(cs, ada_w, ada_b.reshape(nl, 1, n))


def _token_specs(xs, tm, nlat_blocks):
    if not isinstance(xs, tuple):
        return [pl.BlockSpec((1, tm, xs.shape[2]), lambda bb, i: (bb, i, 0))]
    x, ctx = xs
    return [pl.BlockSpec((1, tm, x.shape[2]), lambda bb, i: (bb, jnp.minimum(i, nlat_blocks - 1), 0)),
            pl.BlockSpec((1, tm, ctx.shape[2]), lambda bb, i: (bb, jnp.maximum(i - nlat_blocks, 0), 0))]


def _token_rows(x_refs, nlat_blocks):
    if len(x_refs) == 1:
        return x_refs[0][0]
    return jnp.where(pl.program_id(1) >= nlat_blocks, x_refs[1][0], x_refs[0][0])


def _norm_modulate(x, nw, mods, k, d):
    rows = x.shape[0] // len(mods)
    parts = [_rms(x[j * rows:(j + 1) * rows], nw) * (1.0 + m[:, (k + 1) * d:(k + 2) * d]) + m[:, k * d:(k + 1) * d]
             for j, m in enumerate(mods)]
    return parts[0] if len(parts) == 1 else jnp.concatenate(parts, axis=0)


N_PREP_TABLES, N_PREP_WEIGHTS, N_PREP_OUTS = 6, 13, 7


def _inproj_kernel(*refs, d, nsrc, nlat_blocks):
    mod_ref, nw_ref, w_ref = refs[nsrc:nsrc + 3]
    prep_refs = refs[nsrc + 3:nsrc + 3 + N_PREP_TABLES + N_PREP_WEIGHTS]
    o_ref = refs[nsrc + 3 + N_PREP_TABLES + N_PREP_WEIGHTS]
    prep_outs = refs[-N_PREP_OUTS:]
    x = _token_rows(refs[:nsrc], nlat_blocks)
    h = _norm_modulate(x, nw_ref[...], [mod_ref[0]], 0, d).astype(BF16)
    chunks = []
    for j in range(P_COLS // INPROJ_TN):
        cs = slice(j * INPROJ_TN, (j + 1) * INPROJ_TN)
        res = _dot(h, w_ref[:, cs])
        if cs.stop <= SEG_GATE[0] + SEG_GATE[1]:
            res = res * 0.5
        chunks.append(res.astype(BF16))
        o_ref[0, :, cs] = chunks[-1]

    def seg(sg):
        j, lo = divmod(sg[0], INPROJ_TN)
        assert lo + sg[1] <= INPROJ_TN
        return chunks[j][:, lo:lo + sg[1]]

    _prep_body(*[seg(sg) for sg in (SEG_ACQ, SEG_ACKV, SEG_AKR, SEG_BQ, SEG_BK, SEG_DQ, SEG_DK)],
               *prep_refs, *prep_outs)


def _resident(shape):
    return pl.BlockSpec(shape, lambda *_: (0,) * len(shape), pipeline_mode=pl.Buffered(1))


def _layer_resident(arr, *lead):
    rest = arr.shape[len(lead):]
    return pl.BlockSpec((None,) * len(lead) + rest, lambda *_: tuple(lead) + (0,) * len(rest),
                        pipeline_mode=pl.Buffered(1))


def _mod_spec(modsel, l, nlat_blocks):
    return pl.BlockSpec((None, 1) + modsel.shape[2:],
                        lambda bb, i: (l, bb * 2 + jnp.where(i >= nlat_blocks, 1, 0), 0, 0))


def _pair_mod_specs(modsel, l, tiles_per_batch, nlat_blocks):
    def spec(j):
        def index(_, i):
            tile = 2 * i + j
            row = (tile // tiles_per_batch) * 2 + jnp.where(tile % tiles_per_batch >= nlat_blocks, 1, 0)
            return (l, row, 0, 0)
        return pl.BlockSpec((None, 1) + modsel.shape[2:], index)
    return [spec(0), spec(1)]


def _can_pair(b, s):
    return (b * (s // TOKEN_TILE)) % 2 == 0


def _inproj(xs, modsel, nw, w_p, tabs, wts, l, nlat_blocks):
    srcs = xs if isinstance(xs, tuple) else (xs,)
    b, d = srcs[0].shape[0], srcs[0].shape[2]
    s = sum(a.shape[1] for a in srcs)
    tm = TOKEN_TILE
    assert len(tabs) == N_PREP_TABLES and len(wts) == N_PREP_WEIGHTS
    tab_specs = [pl.BlockSpec((tm, tb.shape[1]), lambda bb, i: (i, 0)) for tb in tabs]
    wt_specs = [_layer_resident(w, l) if stacked else _resident(w.shape) for w, stacked in wts]
    widths = [P_COLS, A_HEADS * A_HEAD_PAD, A_HEADS * A_HEAD_PAD, A_HEADS * A_V,
              B_HEADS * B_HD, B_KV_HEADS * B_HD, D_HEADS * D_HD, D_KV_HEADS * D_HD]
    return pl.pallas_call(
        functools.partial(_inproj_kernel, d=d, nsrc=len(srcs), nlat_blocks=nlat_blocks),
        grid=(b, s // tm),
        in_specs=(_token_specs(xs, tm, nlat_blocks)
                  + [_mod_spec(modsel, l, nlat_blocks), _layer_resident(nw, l), _layer_resident(w_p, l)]
                  + tab_specs + wt_specs),
        out_specs=[pl.BlockSpec((1, tm, w), lambda bb, i: (bb, i, 0)) for w in widths],
        out_shape=[jax.ShapeDtypeStruct((b, s, w), BF16) for w in widths],
        compiler_params=_cparams(("parallel", "parallel")),
        name="inproj",
    )(*srcs, modsel, nw, w_p, *tabs, *[w for w, _ in wts])


def _prep_body(acq, ackv, akr, bq, bk, dq, dk,
               cosa_ref, sina_ref, cosb_ref, sinb_ref, cosd_ref, sind_ref,
               aqn_ref, akvn_ref, wq_ref, wk_ref, wv_ref, permbq_ref, permbk_ref,
               permdq_ref, permdk_ref, dqn_ref, dqnp_ref, dkn_ref, dknp_ref,
               qa_ref, ka_ref, va_ref, qb_ref, kb_ref, qd_ref, kd_ref):
    na = A_HEADS * A_HEAD_PAD
    cos_r, sin_r = cosa_ref[:, A_NOPE:], sina_ref[:, A_NOPE:]

    def rope_heads(x, w_ref, o_ref, scale):
        for h in range(A_HEADS):
            c0 = h * A_HEAD_PAD
            main = _dot(x, w_ref[:, c0:c0 + A_HEAD_PAD])
            part = _dot(x, w_ref[:, na + c0 + A_NOPE:na + c0 + A_HEAD_PAD])
            out = jnp.concatenate([main[:, :A_NOPE], main[:, A_NOPE:] * cos_r + part * sin_r], axis=1)
            o_ref[0, :, c0:c0 + A_HEAD_PAD] = (out * scale).astype(BF16)

    cq = _rms(acq.astype(F32), aqn_ref[...]).astype(BF16)
    rope_heads(cq, wq_ref, qa_ref, (A_NOPE + A_ROPE) ** -0.5)
    ckv = _rms(ackv.astype(F32), akvn_ref[...]).astype(BF16)
    rope_heads(jnp.concatenate([ckv, akr], axis=1), wk_ref, ka_ref, 1.0)
    va_ref[0] = _dot(ckv, wv_ref[...]).astype(BF16)

    nrep = B_HEADS * B_HD // LANE
    cosb, sinb = cosb_ref[...], sinb_ref[...]
    qb = (bq.astype(F32) * jnp.concatenate([cosb] * nrep, axis=1)
          + _dot(bq, permbq_ref[...]) * jnp.concatenate([sinb] * nrep, axis=1))
    qb_ref[0] = (qb * (B_HD ** -0.5)).astype(BF16)
    kb_ref[0] =(bk.astype(F32) * cosb + _dot(bk, permbk_ref[...]) * sinb).astype(BF16)

    cosd, sind = cosd_ref[...], sind_ref[...]

    def norm_rope(x, perm_ref, nw_ref, nwp_ref, heads):
        xp = _dot(x, perm_ref[...])
        xf = x.astype(F32)
        wc, ws = nw_ref[...] * cosd, nwp_ref[...] * sind
        parts = []
        for h in range(heads):
            hs = slice(h * D_HD, (h + 1) * D_HD)
            xh = xf[:, hs]
            inv = lax.rsqrt(jnp.mean(xh * xh, axis=-1, keepdims=True) + EPS)
            parts.append(inv * (xh * wc + xp[:, hs] * ws))
        return jnp.concatenate(parts, axis=1)

    qd_ref[0] = (norm_rope(dq, permdq_ref, dqn_ref, dqnp_ref, D_HEADS) * (D_HD ** -0.5)).astype(BF16)
    kd_ref[0] = norm_rope(dk, permdk_ref, dkn_ref, dknp_ref, D_KV_HEADS).astype(BF16)


def _attn_kernel(q_ref, k_ref, v_ref, o_ref, *, dk, dv, kv_group):
    rows = min(ATTN_CHAIN_ROWS, q_ref.shape[1])
    for r in range(q_ref.shape[1] // rows):
        rs = slice(r * rows, (r + 1) * rows)
        for c in range(ATTN_HEADS_PER_STEP):
            kc = c // kv_group
            s = _dot_nt(q_ref[0, rs, c * dk:(c + 1) * dk], k_ref[0, :, kc * dk:(kc + 1) * dk])
            m = jnp.max(s, axis=-1, keepdims=True)
            p = jnp.exp(s - m)
            l = jnp.sum(p, axis=-1, keepdims=True)
            o = _dot(p.astype(BF16), v_ref[0, :, kc * dv:(kc + 1) * dv])
            o_ref[0, rs, c * dv:(c + 1) * dv] = (o / l).astype(o_ref.dtype)


def _attention(q, k, v, *, heads, kv_group, dk, dv, vcol0, t, cn, with_ctx, name):
    b, s, _ = q.shape
    hps = ATTN_HEADS_PER_STEP
    nkv = hps // kv_group
    kern = functools.partial(_attn_kernel, dk=dk, dv=dv, kv_group=kv_group)
    sem = _cparams(("parallel", "parallel", "parallel"))

    def call(tq, nq, row0, krows, krow0, nm):
        return pl.pallas_call(
            kern,
            grid=(b, heads // hps, nq // tq),
            in_specs=[
                pl.BlockSpec((1, tq, hps * dk), lambda bb, h, i: (bb, row0 + i, h)),
                pl.BlockSpec((1, krows, nkv * dk), lambda bb, h, i: (bb, krow0, h)),
                pl.BlockSpec((1, krows, nkv * dv), lambda bb, h, i: (bb, krow0, vcol0 + h)),
            ],
            out_specs=pl.BlockSpec((1, tq, hps * dv), lambda bb, h, i: (bb, i, h)),
            out_shape=jax.ShapeDtypeStruct((b, nq, heads * dv), BF16),
            compiler_params=sem,
            name=nm,
        )(q, k, v)

    y = call(ATTN_TQ, t, 0, s, 0, name)
    if with_ctx:
        return y, call(cn, cn, t // cn, cn, t // cn, name + "_ctx")
    return y


def _swa_bias():
    band = SWA_BLOCK + 2 * WINDOW
    r = np.arange(SWA_BLOCK)[:, None]
    j = np.arange(band)[None, :]
    out = []
    for off in (0, WINDOW, band - SWA_BLOCK):
        out.append(np.where(np.abs(off + r - j) <= WINDOW, 0.0, NEG_INF))
    return np.stack(out).astype(np.float32)


def _swa_kernel(sink_ref, q_ref, k_ref, v_ref, bias_ref, o_ref, s_ref, p_ref, e_ref, *, t, cn, layer):
    n = pl.program_id(1)
    blk = SWA_BLOCK
    nlat = t // blk
    grp = B_HEADS // B_KV_HEADS
    band = blk + 2 * WINDOW
    rt = SWA_ROW_TILE

    def run(is_ctx):
        q = q_ref[0]
        k, v = k_ref[0, t:t + cn, :], v_ref[0, t:t + cn, :]
        if not is_ctx:
            start = pl.multiple_of(jnp.clip(n * blk - WINDOW, 0, t - band), WINDOW)
            k = jnp.concatenate([k, k_ref[0, pl.ds(start, band), :]], axis=0)
            v = jnp.concatenate([v, v_ref[0, pl.ds(start, band), :]], axis=0)
        nk = k.shape[0]
        halves = []
        qlane = lax.broadcasted_iota(jnp.int32, (blk, LANE), 1)
        for hk in range(B_KV_HEADS):
            mine = ((qlane // B_HD) == hk).astype(BF16)
            q4 = jnp.concatenate([q[:, gi * LANE:(gi + 1) * LANE] * mine for gi in range(grp)], axis=0)
            s_ref[hk, :, :nk] = _dot_nt(q4, k)

            for i in range(grp * blk // rt):
                rs = slice(i * rt, (i + 1) * rt)
                sink = sink_ref[layer, hk * grp + i // (blk // rt)]
                s = s_ref[hk, rs, :nk]
                if not is_ctx:
                    br = (i % (blk // rt)) * rt
                    s = s + bias_ref[0, br:br + rt, :]
                m = jnp.maximum(jnp.max(s, axis=-1, keepdims=True), sink)
                p = jnp.exp(s - m)
                l = jnp.sum(p, axis=-1, keepdims=True) + jnp.exp(sink - m)
                p_ref[hk, rs, :nk] = p.astype(BF16)
                e_ref[hk, rs, :] = jnp.broadcast_to(l, (rt, LANE))
            halves.append(_dot(p_ref[hk, :, :nk], v) / e_ref[hk])
        lane = lax.broadcasted_iota(jnp.int32, (blk, LANE), 1)
        outs = [jnp.where(lane < B_HD, halves[0][gi * blk:(gi + 1) * blk], halves[1][gi * blk:(gi + 1) * blk])
                for gi in range(grp)]
        o_ref[0] = jnp.concatenate(outs, axis=1).astype(o_ref.dtype)

    @pl.when(n < nlat)
    def _():
        run(False)

    @pl.when(n >= nlat)
    def _():
        run(True)


def _swa(sink, q, k, p, l, *, t, cn, n_blocks):
    b, s, _ = q.shape
    nlat = t // SWA_BLOCK
    rows = (B_HEADS // B_KV_HEADS) * SWA_BLOCK
    band_bias = _swa_bias()
    bias = jnp.asarray(np.concatenate([np.zeros(band_bias.shape[:2] + (cn,), np.float32), band_bias], axis=2))
    return pl.pallas_call(
        functools.partial(_swa_kernel, t=t, cn=cn, layer=l),
        grid=(b, n_blocks),
        in_specs=[
            pl.BlockSpec(memory_space=pltpu.SMEM),
            pl.BlockSpec((1, SWA_BLOCK, B_HEADS * B_HD), lambda bb, n: (bb, n, 0)),
            pl.BlockSpec((1, s, B_KV_HEADS * B_HD), lambda bb, n: (bb, 0, 0)),
            pl.BlockSpec((1, s, SEG_BV[1]), lambda bb, n: (bb, 0, SEG_BV[0] // SEG_BV[1])),
            pl.BlockSpec((1,) + bias.shape[1:],
                         lambda bb, n: (jnp.where(n == 0, 0, jnp.where(n >= nlat - 1, 2, 1)), 0, 0)),
        ],
        out_specs=pl.BlockSpec((1, SWA_BLOCK, B_HEADS * B_HD), lambda bb, n: (bb, n, 0)),
        out_shape=jax.ShapeDtypeStruct((b, n_blocks * SWA_BLOCK, B_HEADS * B_HD), BF16),
        scratch_shapes=[pltpu.VMEM((B_KV_HEADS, rows, bias.shape[2]), F32),
                        pltpu.VMEM((B_KV_HEADS, rows, bias.shape[2]), BF16),
                        pltpu.VMEM((B_KV_HEADS, rows, LANE), F32)],
        compiler_params=_cparams(("parallel", "parallel")),
        name="swa",
    )(sink, q, k, p, bias)


GLA_BLOCK = 256
GLA_SAFE_DECAY = 60.0


def _log_decay(lr, w, bias):
    z = _dot(lr, w) + bias
    return (jnp.minimum(z, 0.0) - jnp.log(1.0 + jnp.exp(-jnp.abs(z)))) / C_DECAY_TEMP


def _cum_decay(la, tri):
    hi = la.astype(BF16)
    lo = (la - hi.astype(F32)).astype(BF16)
    return _dot(tri, hi) + _dot(tri, lo)


def _gla_kernel(qf_ref, kf_ref, vf_ref, lrf_ref, qb_ref, kb_ref, vb_ref, lrb_ref,
                wdec_ref, bdec_ref, sel_ref, of_ref, ob_ref, st_ref, e_ref):
    n = GLA_BLOCK
    pair = LANE // C_DK
    npair = C_HEADS // pair
    qscale = C_DK ** -0.5
    refs = ((qf_ref, kf_ref, vf_ref, lrf_ref), (qb_ref, kb_ref, vb_ref, lrb_ref))
    outs = (of_ref, ob_ref)

    @pl.when(pl.program_id(1) == 0)
    def _():
        st_ref[...] = jnp.zeros_like(st_ref)

    r_i = lax.broadcasted_iota(jnp.int32, (n, n), 0)
    c_i = lax.broadcasted_iota(jnp.int32, (n, n), 1)
    keep = (c_i <= r_i, c_i >= r_i)
    lane = lax.broadcasted_iota(jnp.int32, (1, LANE), 1)
    head_mask = [(lane // C_DK) == hh for hh in range(pair)]

    cums, tots = [], []
    for d in range(2):
        la = _log_decay(refs[d][3][0], wdec_ref[d], bdec_ref[d])
        cum = _cum_decay(la, keep[d].astype(BF16))
        cums.append(cum)
        tots.append(cum[n - 1:n] if d == 0 else cum[0:1])
    safe = jnp.maximum(jnp.max(-tots[0]), jnp.max(-tots[1])) <= GLA_SAFE_DECAY

    @pl.when(safe)
    def _():
        for d in range(2):
            q_ref, k_ref, v_ref, _ = refs[d]
            cum, tot = cums[d], tots[d]
            half = 0.5 * tot
            a = cum - half
            eh = jnp.exp(half)
            qr = q_ref[0].astype(F32) * qscale * jnp.exp(a)
            kr = k_ref[0].astype(F32) * jnp.exp(-a)
            kr_b, kd_b = kr.astype(BF16), (kr * eh).astype(BF16)
            qe = qr * eh
            v, st = v_ref[0], st_ref[d]
            o_parts, u_parts = [], []
            for p in range(npair):
                ls = slice(p * LANE, (p + 1) * LANE)
                st2 = st[:, ls].astype(BF16)
                upd = []
                for hh in range(pair):
                    h = p * pair + hh
                    vh = v[:, h * C_DV:(h + 1) * C_DV]
                    qr_h = jnp.where(head_mask[hh], qr[:, ls], 0.0).astype(BF16)
                    qe_h = jnp.where(head_mask[hh], qe[:, ls], 0.0).astype(BF16)
                    pm = jnp.where(keep[d], _dot_nt(qr_h, kr_b[:, ls]), 0.0).astype(BF16)
                    o_parts.append(_dot(pm, vh) + _dot_nt(qe_h, st2))
                    upd.append(_dot_tn(vh, kd_b[:, ls]))
                u_parts.append(jnp.where(head_mask[0], upd[0], upd[1]))
            outs[d][0] = jnp.concatenate(o_parts, axis=1)
            st_ref[d] = st * jnp.exp(tot) + jnp.concatenate(u_parts, axis=1)

    @pl.when(jnp.logical_not(safe))
    def _():
        c = C_CHUNK
        tile = 16
        nsub = n // c
        r64 = lax.broadcasted_iota(jnp.int32, (c, c), 0)
        c64 = lax.broadcasted_iota(jnp.int32, (c, c), 1)
        tri64 = ((c64 <= r64).astype(BF16), (c64 >= r64).astype(BF16))
        for d in range(2):
            q_ref, k_ref, v_ref, lr_ref = refs[d]
            e_ref[...] = jnp.zeros_like(e_ref)

            def sub(i, carry, d=d, q_ref=q_ref, k_ref=k_ref, v_ref=v_ref, lr_ref=lr_ref):
                rows = pl.ds(pl.multiple_of((i if d == 0 else nsub - 1 - i) * c, c), c)
                q = q_ref[0, rows, :].astype(F32) * qscale
                k = k_ref[0, rows, :].astype(F32)
                v = v_ref[0, rows, :]
                cum = _cum_decay(_log_decay(lr_ref[0, rows, :], wdec_ref[d], bdec_ref[d]), tri64[d])
                tot = cum[c - 1:c] if d == 0 else cum[0:1]
                qe = (q * jnp.exp(cum)).astype(BF16)
                kd = (k * jnp.exp(tot - cum)).astype(BF16)
                for p in range(npair):
                    ls = slice(p * LANE, (p + 1) * LANE)
                    q2, k2, c2 = q[:, ls], k[:, ls], cum[:, ls]
                    for s in range(c):
                        bt = (s // tile) * tile
                        t0, t1 = (bt, c) if d == 0 else (0, bt + tile)
                        e = q2[t0:t1] * k2[s:s + 1] * jnp.exp(c2[t0:t1] - c2[s:s + 1])
                        ridx = lax.broadcasted_iota(jnp.int32, (t1 - t0, LANE), 0) + t0
                        causal = (ridx >= s) if d == 0 else (ridx <= s)
                        e_ref[p * c + t0:p * c + t1, s * LANE:(s + 1) * LANE] = (
                            jnp.where(causal, e, 0.0).astype(BF16))
                pm = _dot(e_ref[...], sel_ref[...])
                st = st_ref[d]
                o_parts, u_parts = [], []
                for h in range(C_HEADS):
                    p, hh = divmod(h, pair)
                    ph = pm[p * c:(p + 1) * c, hh * c:(hh + 1) * c].astype(BF16)
                    vh = v[:, h * C_DV:(h + 1) * C_DV]
                    hs = slice(h * C_DK, (h + 1) * C_DK)
                    o_parts.append(_dot(ph, vh) + _dot_nt(qe[:, hs], st[:, hs].astype(BF16)))
                    u_parts.append(_dot_tn(vh, kd[:, hs]))
                outs[d][0, rows, :] = jnp.concatenate(o_parts, axis=1)
                st_ref[d] = st * jnp.exp(tot) + jnp.concatenate(u_parts, axis=1)
                return carry

            lax.fori_loop(0, nsub, sub, 0)


def _gla_sel():
    pair = LANE // C_DK
    sel = np.zeros((C_CHUNK, pair, C_DK, pair, C_CHUNK), np.float32)
    for s in range(C_CHUNK):
        for h in range(pair):
            sel[s, h, :, h, s] = 1.0
    return jnp.asarray(sel.reshape(C_CHUNK * LANE, pair * C_CHUNK), BF16)


def _gla(p, wdec, bdec, sel, l, *, t, cn):
    b, s, _ = p.shape
    n = GLA_BLOCK
    nbl, nbc = t // n, cn // n

    def fwd(j):
        return jnp.where(j < nbc, nbl + j, j - nbc)

    def bwd(j):
        return nbl + nbc - 1 - j

    def specs(order):
        return [pl.BlockSpec((1, n, sg[1]), lambda bb, j, sg=sg: (bb, order(j), sg[0] // sg[1]))
                for sg in (SEG_CQ, SEG_CK, SEG_CV, SEG_CLR)]

    width = C_HEADS * C_DV
    return pl.pallas_call(
        _gla_kernel,
        grid=(b, nbl + nbc),
        in_specs=specs(fwd) + specs(bwd) + [
            _layer_resident(wdec, l), _layer_resident(bdec, l), _resident(sel.shape),
        ],
        out_specs=[pl.BlockSpec((1, n, width), lambda bb, j: (bb, fwd(j), 0)),
                   pl.BlockSpec((1, n, width), lambda bb, j: (bb, bwd(j), 0))],
        out_shape=[jax.ShapeDtypeStruct((b, s, width), F32)] * 2,
        scratch_shapes=[pltpu.VMEM((2, C_DV, C_HEADS * C_DK), F32),
                        pltpu.VMEM(((C_HEADS * C_DK // LANE) * C_CHUNK, C_CHUNK * LANE), BF16)],
        compiler_params=_cparams(("parallel", "arbitrary")),
        name="gla",
    )(*([p] * 8), wdec, bdec, sel)


def _merge_kernel(*refs, d, nsrc, nlat_blocks):
    x_refs, ya_refs, yd_refs = (refs[i * nsrc:(i + 1) * nsrc] for i in range(3))
    (mod_ref, yb_ref, of_ref, ob_ref, r_ref, g_ref,
     gnw_ref, wba_ref, wbb_ref, wbc_ref, wbd_ref, wo_ref, o_ref) = refs[3 * nsrc:]
    o = of_ref[0] + ob_ref[0]
    r = r_ref[0].astype(F32)
    yg = jnp.concatenate([_rms(o[:, h * C_DV:(h + 1) * C_DV], gnw_ref[...]) for h in range(C_HEADS)], axis=1)
    yg = (yg * (r * jax.nn.sigmoid(r))).astype(BF16)
    branches = ((_token_rows(ya_refs, nlat_blocks), wba_ref), (yb_ref[0], wbb_ref), (yg, wbc_ref),
                (_token_rows(yd_refs, nlat_blocks), wbd_ref))
    mixed = []
    for c in range(d // MERGE_TN):
        cs = slice(c * MERGE_TN, (c + 1) * MERGE_TN)
        acc_z = acc_t = None
        for n, (y, w_ref) in enumerate(branches):
            z = _dot(y, w_ref[:, cs])
            tz = jnp.tanh(g_ref[0, :, n * d + c * MERGE_TN:n * d + (c + 1) * MERGE_TN].astype(F32)) * z
            acc_z = z if acc_z is None else acc_z + z
            acc_t = tz if acc_t is None else acc_t + tz
        mixed.append((0.5 * (acc_z + acc_t)).astype(BF16))
    out = _dot(jnp.concatenate(mixed, axis=1), wo_ref[...])
    o_ref[0] = _token_rows(x_refs, nlat_blocks) + mod_ref[0][:, 2 * d:3 * d] * out


def _merge(xs, modsel, ya, yb, o_f, o_b, p, yd, gnw, w_branch, wb_b, wo, l, *, tm, n_blocks, nlat_blocks):
    assert isinstance(xs, tuple) == isinstance(ya, tuple) == isinstance(yd, tuple)
    streams = [s if isinstance(s, tuple) else (s,) for s in (xs, ya, yd)]
    srcs = [a for s in streams for a in s]
    b, d = srcs[0].shape[0], srcs[0].shape[2]

    def tok(w, col=0):
        return pl.BlockSpec((1, tm, w), lambda bb, i: (bb, i, col))

    w_specs = [_layer_resident(w_branch, l, 0), _layer_resident(wb_b, l),
               _layer_resident(w_branch, l, 2), _layer_resident(w_branch, l, 3), _layer_resident(wo, l)]
    return pl.pallas_call(
        functools.partial(_merge_kernel, d=d, nsrc=len(streams[0]), nlat_blocks=nlat_blocks),
        grid=(b, n_blocks),
        in_specs=[sp for s in (xs, ya, yd) for sp in _token_specs(s, tm, nlat_blocks)] + [
            _mod_spec(modsel, l, nlat_blocks),
            tok(yb.shape[2]), tok(o_f.shape[2]), tok(o_b.shape[2]),
            tok(SEG_CR[1], SEG_CR[0] // SEG_CR[1]),
            tok(SEG_GATE[1], 0),
            _layer_resident(gnw, l), *w_specs,
        ],
        out_specs=tok(d),
        out_shape=jax.ShapeDtypeStruct((b, n_blocks * tm, d), F32),
        compiler_params=_cparams(("parallel", "parallel")),
        name="merge",
    )(*srcs, modsel, yb, o_f, o_b, p, p, gnw, w_branch, wb_b, w_branch, w_branch, wo)


def _mlp_kernel(x_ref, *refs, d, nmod, final):
    mods = [m[0] for m in refs[:nmod]]
    nw_ref, w1_ref, w2_ref, fnw_ref, o_ref = refs[nmod:]
    x = x_ref[0]
    h = _norm_modulate(x, nw_ref[...], mods, 3, d)
    a = jnp.maximum(_dot(h.astype(BF16), w1_ref[...]), 0.0)
    out = _dot((a * a).astype(BF16), w2_ref[...])
    rows = x.shape[0] // nmod
    ys = [x[j * rows:(j + 1) * rows] + m[:, 5 * d:6 * d] * out[j * rows:(j + 1) * rows] for j, m in enumerate(mods)]
    y = ys[0] if nmod == 1 else jnp.concatenate(ys, axis=0)
    o_ref[0] = _rms(y, fnw_ref[...]) if final else y


def _mlp(xs, modsel, nw, w1, w2, fnw, l, *, tm, n_blocks, nlat_blocks, out_rows, final):
    b, s, d = xs.shape
    out_b = b
    if out_rows == s and tm == TOKEN_TILE and _can_pair(b, s):
        xs, out_b, out_rows = xs.reshape(1, b * s, d), 1, b * s
        grid, tm = (1, b * s // PAIR_TILE), PAIR_TILE
        mod_specs = _pair_mod_specs(modsel, l, s // TOKEN_TILE, nlat_blocks)
    else:
        grid = (b, n_blocks)
        mod_specs = [_mod_spec(modsel, l, nlat_blocks)]
    y = pl.pallas_call(
        functools.partial(_mlp_kernel, d=d, nmod=len(mod_specs), final=final),
        grid=grid,
        in_specs=[pl.BlockSpec((1, tm, d), lambda bb, i: (bb, i, 0))] + mod_specs + [
            _layer_resident(nw, l), _layer_resident(w1, l), _layer_resident(w2, l), _resident(fnw.shape)],
        out_specs=pl.BlockSpec((1, tm, d), lambda bb, i: (bb, i, 0)),
        out_shape=jax.ShapeDtypeStruct((out_b, out_rows, d), F32),
        compiler_params=_cparams(("parallel", "parallel")),
        name="mlp",
    )(xs, *([modsel] * len(mod_specs)), nw, w1, w2, fnw)
    return y.reshape(b, -1, d)


def _pad_cols(w, n):
    return jnp.concatenate([w, jnp.zeros(w.shape[:-1] + (n,), w.dtype)], axis=-1)


def _layout_w_in(w):
    a0 = 0
    b0 = a0 + A_Q_RANK + A_KV_RANK + A_ROPE
    c0 = b0 + (B_HEADS + 2 * B_KV_HEADS) * B_HD
    d0 = c0 + 2 * C_HEADS * C_DK + 2 * C_HEADS * C_DV + 2 * C_DECAY_RANK
    g0 = d0 + (D_HEADS + 2 * D_KV_HEADS) * D_HD

    def cols(lo, n):
        return w[..., lo:lo + n]

    cq = c0
    ck = cq + C_HEADS * C_DK
    cv = ck + C_HEADS * C_DK
    cr = cv + C_HEADS * C_DV
    clr = cr + C_HEADS * C_DV
    parts = [
        cols(g0, SEG_GATE[1]),
        cols(d0, SEG_DQ[1] + SEG_DK[1] + SEG_DV[1]),
        *[cols(b0 + h * B_HD, B_HD) for h in _swa_head_order()],
        cols(b0 + SEG_BQ[1], SEG_BK[1] + SEG_BV[1]),
        cols(a0, A_Q_RANK + A_KV_RANK),
        _pad_cols(cols(a0 + A_Q_RANK + A_KV_RANK, A_ROPE), SEG_AKR[1] - A_ROPE),
        cols(cq, SEG_CQ[1]), cols(cv, SEG_CV[1]), cols(cr, SEG_CR[1]), cols(ck, SEG_CK[1]),
        _pad_cols(cols(clr, 2 * C_DECAY_RANK), P_COLS - SEG_CLR[0] - 2 * C_DECAY_RANK),
    ]
    return jnp.concatenate(parts, axis=-1).astype(BF16)


def _layout_mla(w_uq, w_ukv):
    zq = jnp.zeros((A_Q_RANK, A_HEAD_PAD - A_NOPE - A_ROPE), w_uq.dtype)
    hq = A_NOPE + A_ROPE
    wq = jnp.concatenate([jnp.concatenate([w_uq[:, h * hq:(h + 1) * hq], zq], axis=1)
                          for h in range(A_HEADS)], axis=1)
    hk = A_NOPE + A_V
    zk = jnp.zeros((A_KV_RANK, A_HEAD_PAD - A_NOPE), w_ukv.dtype)
    wk_top = jnp.concatenate([jnp.concatenate([w_ukv[:, h * hk:h * hk + A_NOPE], zk], axis=1)
                              for h in range(A_HEADS)], axis=1)
    place = np.zeros((SEG_AKR[1], A_HEADS * A_HEAD_PAD), np.float32)
    for h in range(A_HEADS):
        place[np.arange(A_ROPE), h * A_HEAD_PAD + A_NOPE + np.arange(A_ROPE)] = 1.0
    wk = jnp.concatenate([wk_top, jnp.asarray(place)], axis=0)
    wv = jnp.concatenate([w_ukv[:, h * hk + A_NOPE:(h + 1) * hk] for h in range(A_HEADS)], axis=1)
    lanes = np.arange(A_HEADS * A_HEAD_PAD)
    rot = ((lanes % A_HEAD_PAD) >= A_NOPE) & ((lanes % A_HEAD_PAD) < A_NOPE + A_ROPE)
    idx = _partner_lanes(lanes.size, A_ROPE // 4)

    def with_partner(w):
        return jnp.concatenate([w, jnp.where(jnp.asarray(rot)[None, :], w[:, idx], 0.0)], axis=1)

    return with_partner(wq).astype(BF16), with_partner(wk).astype(BF16), wv.astype(BF16)


def _partner_lanes(width, blk):
    j = np.arange(width)
    return np.where((j % (2 * blk)) < blk, j + blk, j - blk)


def _perm_matrix(width, blk):
    m = np.zeros((width, width), np.float32)
    m[_partner_lanes(width, blk), np.arange(width)] = 1.0
    return m


def _swa_head_order():
    grp = B_HEADS // B_KV_HEADS
    return [h for g in range(grp) for h in (g, grp + g)]


def _layout_w_branch_b(w):
    return jnp.concatenate([w[h * B_HD:(h + 1) * B_HD] for h in _swa_head_order()], axis=0)


def _layout_decay(w_decay):
    z = jnp.zeros((C_DECAY_RANK, w_decay.shape[-1]), w_decay.dtype)
    tail = jnp.zeros((SEG_CLR[1] - 2 * C_DECAY_RANK, w_decay.shape[-1]), w_decay.dtype)
    return jnp.stack([jnp.concatenate([w_decay[0], z, tail], axis=0),
                      jnp.concatenate([z, w_decay[1], tail], axis=0)]).astype(BF16)


def kernel(x, c, ctx, c_ctx, ada_w, ada_b, norm1_w, norm2_w, w_in, mla_q_norm_w, mla_w_uq, mla_kv_norm_w, mla_w_ukv, swa_sink, gla_w_decay, gla_b_decay, gla_norm_w, gqa_q_norm_w, gqa_k_norm_w, w_branch, w_o, mlp_w1, mlp_w2, final_norm_w):
    b, t, d = x.shape
    cn = ctx.shape[1]
    depth = ada_w.shape[0]
    tm = TOKEN_TILE
    assert t % tm == 0 and cn % tm == 0 and t % GRID_W == 0 and t >= SWA_BLOCK + 2 * WINDOW
    assert t % ATTN_TQ == 0 and t % cn == 0 and t % GLA_BLOCK == 0 and cn % GLA_BLOCK == 0
    nlat, nall = t // tm, (t + cn) // tm

    rows = 16
    cs = jnp.concatenate([c, c_ctx[None], jnp.zeros((rows - b - 1, d), F32)], axis=0)
    mod = _ada(cs, ada_w, ada_b)
    modsel = jnp.stack([mod[:, :b], jnp.broadcast_to(mod[:, b:b + 1], (depth, b, 6 * d))], axis=2)
    modsel = modsel.reshape(depth, b * 2, 1, 6 * d)

    tabs = (_rope_tables(t, cn, A_HEAD_PAD, A_NOPE, A_ROPE)
            + tuple(jnp.tile(tb, (1, LANE // B_HD)) for tb in _rope_tables(t, cn, B_HD, 0, B_HD))
            + _rope_tables(t, cn, D_HD, 0, D_HD))
    sel = _gla_sel()
    perms = (jnp.asarray(_perm_matrix(B_HEADS * B_HD, B_HD // 4), BF16),
             jnp.asarray(_perm_matrix(B_KV_HEADS * B_HD, B_HD // 4), BF16),
             jnp.asarray(_perm_matrix(D_HEADS * D_HD, D_HD // 4), BF16),
             jnp.asarray(_perm_matrix(D_KV_HEADS * D_HD, D_HD // 4), BF16))
    d_partner = _partner_lanes(D_HD, D_HD // 4)

    w_p = _layout_w_in(w_in)
    wbr, wo = w_branch.astype(BF16), w_o.astype(BF16)
    wb_b = jax.vmap(_layout_w_branch_b)(w_branch[:, 1]).astype(BF16)
    w1, w2 = mlp_w1.astype(BF16), mlp_w2.astype(BF16)
    wq, wk, wv = jax.vmap(_layout_mla)(mla_w_uq, mla_w_ukv)
    wdec, bdec = jax.vmap(_layout_decay)(gla_w_decay), gla_b_decay[:, :, None, :]

    def rows(w):
        return w[:, None, :]

    prep_wts = ([(rows(mla_q_norm_w), True), (rows(mla_kv_norm_w), True), (wq, True), (wk, True), (wv, True)]
                + [(pm, False) for pm in perms]
                + [(rows(gqa_q_norm_w), True), (rows(gqa_q_norm_w[:, d_partner]), True),
                   (rows(gqa_k_norm_w), True), (rows(gqa_k_norm_w[:, d_partner]), True)])
    n1, n2, gnw = rows(norm1_w), rows(norm2_w), rows(gla_norm_w)

    xs = (x, ctx)
    for l in range(depth):
        last = l == depth - 1
        p, qa, ka, va, qb, kb, qd, kd = _inproj(xs, modsel, n1, w_p, tabs, prep_wts, l, nlat)
        ya = _attention(qa, ka, va, heads=A_HEADS, kv_group=1, dk=A_HEAD_PAD, dv=A_V, vcol0=0,
                        t=t, cn=cn, with_ctx=not last, name="attn_a")
        yd = _attention(qd, kd, p, heads=D_HEADS, kv_group=D_HEADS // D_KV_HEADS, dk=D_HD, dv=D_HD,
                        vcol0=SEG_DV[0] // SEG_DV[1], t=t, cn=cn, with_ctx=not last, name="attn_d")
        yb = _swa(swa_sink, qb, kb, p, l, t=t, cn=cn,
                  n_blocks=(t if last else t + cn) // SWA_BLOCK)
        o_f, o_b = _gla(p, wdec, bdec, sel, l, t=t, cn=cn)
        if last and isinstance(xs, tuple):
            xs = xs[0]
        tile = dict(tm=LAST_TILE, n_blocks=t // LAST_TILE, nlat_blocks=t // LAST_TILE) if last else dict(
            tm=tm, n_blocks=nall, nlat_blocks=nlat)
        xs = _merge(xs, modsel, ya, yb, o_f, o_b, p, yd, gnw, wbr, wb_b, wo, l, **tile)
        xs = _mlp(xs, modsel, n2, w1, w2, final_norm_w[None], l,
                  out_rows=t if last else t + cn, final=last, **tile)
    return xs
```

```python
import functools

import numpy as np
import jax
import jax.numpy as jnp
from jax import lax
from jax.experimental import pallas as pl
from jax.experimental.pallas import tpu as pltpu

F32 = jnp.float32
BF16 = jnp.bfloat16

GRID_W = 64
EPS = 1e-6
ROPE_BASE = 10000.0
NEG_INF = -1e30

A_HEADS, A_NOPE, A_ROPE, A_V, A_Q_RANK, A_KV_RANK = 4, 128, 64, 128, 256, 128
A_HEAD_PAD = 256
B_HEADS, B_KV_HEADS, B_HD, WINDOW = 8, 2, 64, 128
SWA_BLOCK = 256
SWA_ROW_TILE = 32
C_HEADS, C_DK, C_DV, C_DECAY_RANK, C_DECAY_TEMP, C_CHUNK = 4, 64, 128, 16, 16.0, 64
D_HEADS, D_KV_HEADS, D_HD = 4, 2, 128
N_BRANCH, BRANCH_W = 4, 512

LANE = 128
TOKEN_TILE = 256
LAST_TILE = 512
ATTN_TQ = 1024
ATTN_CHAIN_ROWS = 512
ATTN_HEADS_PER_STEP = 4
INPROJ_TN = 2048
MERGE_TN = 256
VMEM_LIMIT = 56 * 1024 * 1024

P_COLS = 8192
SEG_GATE = (0, 4096)
SEG_DQ, SEG_DK, SEG_DV = (4096, 512), (4608, 256), (4864, 256)
SEG_BQ, SEG_BK, SEG_BV = (5120, 512), (5632, 128), (5760, 128)
SEG_ACQ, SEG_ACKV, SEG_AKR = (5888, 256), (6144, 128), (6272, 128)
SEG_CQ, SEG_CV, SEG_CR, SEG_CK, SEG_CLR = (6400, 256), (6656, 512), (7168, 512), (7680, 256), (7936, 128)


def _cparams(sem):
    return pltpu.CompilerParams(dimension_semantics=sem, vmem_limit_bytes=VMEM_LIMIT)


def _rms(xf, w):
    return xf * lax.rsqrt(jnp.mean(xf * xf, axis=-1, keepdims=True) + EPS) * w


def _dot(a, b):
    return jnp.dot(a, b, preferred_element_type=F32)


def _dot_nt(a, b):
    return lax.dot_general(a, b, (((1,), (1,)), ((), ())), preferred_element_type=F32)


def _dot_tn(a, b):
    return lax.dot_general(a, b, (((0,), (0,)), ((), ())), preferred_element_type=F32)


def _rope_tables(t, cn, width, offset, rot):
    half = rot // 2
    q = half // 2
    inv = ROPE_BASE ** (-np.arange(q, dtype=np.float64) / q)
    pos = np.arange(t)
    cos = np.ones((t + cn, width), np.float64)
    sin = np.zeros((t + cn, width), np.float64)
    for part, p in enumerate((pos // GRID_W, pos % GRID_W)):
        ang = p[:, None] * inv[None, :]
        lo = offset + part * half
        cos[:t, lo:lo + half] = np.concatenate([np.cos(ang), np.cos(ang)], axis=1)
        sin[:t, lo:lo + half] = np.concatenate([-np.sin(ang), np.sin(ang)], axis=1)
    return jnp.asarray(cos, F32), jnp.asarray(sin, F32)


def _ada_kernel(cs_ref, w_ref, b_ref, o_ref):
    cs = cs_ref[...]
    s = cs * jax.nn.sigmoid(cs)
    o_ref[0] = _dot(s.astype(BF16), w_ref[0].astype(BF16)) + b_ref[0]


def _ada(cs, ada_w, ada_b):
    nl, d, n = ada_w.shape
    tn = 1536
    rows = cs.shape[0]
    return pl.pallas_call(
        _ada_kernel,
        grid=(nl, n // tn),
        in_specs=[
            pl.BlockSpec((rows, d), lambda l, j: (0, 0)),
            pl.BlockSpec((1, d, tn), lambda l, j: (l, 0, j)),
            pl.BlockSpec((1, 1, tn), lambda l, j: (l, 0, j)),
        ],
        out_specs=pl.BlockSpec((1, rows, tn), lambda l, j: (l, 0, j)),
        out_shape=jax.ShapeDtypeStruct((nl, rows, n), F32),
        compiler_params=_cparams(("parallel", "parallel")),
        name="ada",
    )(cs, ada_w, ada_b.reshape(nl, 1, n))


def _token_specs(xs, tm, nlat_blocks):
    if not isinstance(xs, tuple):
        return [pl.BlockSpec((1, tm, xs.shape[2]), lambda bb, i: (bb, i, 0))]
    x, ctx = xs
    return [pl.BlockSpec((1, tm, x.shape[2]), lambda bb, i: (bb, jnp.minimum(i, nlat_blocks - 1), 0)),
            pl.BlockSpec((1, tm, ctx.shape[2]), lambda bb, i: (bb, jnp.maximum(i - nlat_blocks, 0), 0))]


def _token_rows(x_refs, nlat_blocks):
    if len(x_refs) == 1:
        return x_refs[0][0]
    return jnp.where(pl.program_id(1) >= nlat_blocks, x_refs[1][0], x_refs[0][0])


def _norm_modulate(x, nw, mods, k, d):
    rows = x.shape[0] // len(mods)
    parts = [_rms(x[j * rows:(j + 1) * rows], nw) * (1.0 + m[:, (k + 1) * d:(k + 2) * d]) + m[:, k * d:(k + 1) * d]
             for j, m in enumerate(mods)]
    return parts[0] if len(parts) == 1 else jnp.concatenate(parts, axis=0)


N_PREP_TABLES, N_PREP_WEIGHTS, N_PREP_OUTS = 6, 13, 7


def _inproj_kernel(*refs, d, nsrc, nlat_blocks):
    mod_ref, nw_ref, w_ref = refs[nsrc:nsrc + 3]
    prep_refs = refs[nsrc + 3:nsrc + 3 + N_PREP_TABLES + N_PREP_WEIGHTS]
    o_ref = refs[nsrc + 3 + N_PREP_TABLES + N_PREP_WEIGHTS]
    prep_outs = refs[-N_PREP_OUTS:]
    x = _token_rows(refs[:nsrc], nlat_blocks)
    h = _norm_modulate(x, nw_ref[...], [mod_ref[0]], 0, d).astype(BF16)
    chunks = []
    for j in range(P_COLS // INPROJ_TN):
        cs = slice(j * INPROJ_TN, (j + 1) * INPROJ_TN)
        res = _dot(h, w_ref[:, cs])
        if cs.stop <= SEG_GATE[0] + SEG_GATE[1]:
            res = res * 0.5
        chunks.append(res.astype(BF16))
        o_ref[0, :, cs] = chunks[-1]

    def seg(sg):
        j, lo = divmod(sg[0], INPROJ_TN)
        assert lo + sg[1] <= INPROJ_TN
        return chunks[j][:, lo:lo + sg[1]]

    _prep_body(*[seg(sg) for sg in (SEG_ACQ, SEG_ACKV, SEG_AKR, SEG_BQ, SEG_BK, SEG_DQ, SEG_DK)],
               *prep_refs, *prep_outs)


def _resident(shape):
    return pl.BlockSpec(shape, lambda *_: (0,) * len(shape), pipeline_mode=pl.Buffered(1))


def _layer_resident(arr, *lead):
    rest = arr.shape[len(lead):]
    return pl.BlockSpec((None,) * len(lead) + rest, lambda *_: tuple(lead) + (0,) * len(rest),
                        pipeline_mode=pl.Buffered(1))


def _mod_spec(modsel, l, nlat_blocks):
    return pl.BlockSpec((None, 1) + modsel.shape[2:],
                        lambda bb, i: (l, bb * 2 + jnp.where(i >= nlat_blocks, 1, 0), 0, 0))


def _inproj(xs, modsel, nw, w_p, tabs, wts, l, nlat_blocks):
    srcs = xs if isinstance(xs, tuple) else (xs,)
    b, d = srcs[0].shape[0], srcs[0].shape[2]
    s = sum(a.shape[1] for a in srcs)
    tm = TOKEN_TILE
    assert len(tabs) == N_PREP_TABLES and len(wts) == N_PREP_WEIGHTS
    tab_specs = [pl.BlockSpec((tm, tb.shape[1]), lambda bb, i: (i, 0)) for tb in tabs]
    wt_specs = [_layer_resident(w, l) if stacked else _resident(w.shape) for w, stacked in wts]
    widths = [P_COLS, A_HEADS * A_HEAD_PAD, A_HEADS * A_HEAD_PAD, A_HEADS * A_V,
              B_HEADS * B_HD, B_KV_HEADS * B_HD, D_HEADS * D_HD, D_KV_HEADS * D_HD]
    return pl.pallas_call(
        functools.partial(_inproj_kernel, d=d, nsrc=len(srcs), nlat_blocks=nlat_blocks),
        grid=(b, s // tm),
        in_specs=(_token_specs(xs, tm, nlat_blocks)
                  + [_mod_spec(modsel, l, nlat_blocks), _layer_resident(nw, l), _layer_resident(w_p, l)]
                  + tab_specs + wt_specs),
        out_specs=[pl.BlockSpec((1, tm, w), lambda bb, i: (bb, i, 0)) for w in widths],
        out_shape=[jax.ShapeDtypeStruct((b, s, w), BF16) for w in widths],
        compiler_params=_cparams(("parallel", "parallel")),
        name="inproj",
    )(*srcs, modsel, nw, w_p, *tabs, *[w for w, _ in wts])


def _prep_body(acq, ackv, akr, bq, bk, dq, dk,
               cosa_ref, sina_ref, cosb_ref, sinb_ref, cosd_ref, sind_ref,
               aqn_ref, akvn_ref, wq_ref, wk_ref, wv_ref, permbq_ref, permbk_ref,
               permdq_ref, permdk_ref, dqn_ref, dqnp_ref, dkn_ref, dknp_ref,
               qa_ref, ka_ref, va_ref, qb_ref, kb_ref, qd_ref, kd_ref):
    na = A_HEADS * A_HEAD_PAD
    cos_r, sin_r = cosa_ref[:, A_NOPE:], sina_ref[:, A_NOPE:]

    def rope_heads(x, w_ref, o_ref, scale):
        for h in range(A_HEADS):
            c0 = h * A_HEAD_PAD
            main = _dot(x, w_ref[:, c0:c0 + A_HEAD_PAD])
            part = _dot(x, w_ref[:, na + c0 + A_NOPE:na + c0 + A_HEAD_PAD])
            out = jnp.concatenate([main[:, :A_NOPE], main[:, A_NOPE:] * cos_r + part * sin_r], axis=1)
            o_ref[0, :, c0:c0 + A_HEAD_PAD] = (out * scale).astype(BF16)

    cq = _rms(acq.astype(F32), aqn_ref[...]).astype(BF16)
    rope_heads(cq, wq_ref, qa_ref, (A_NOPE + A_ROPE) ** -0.5)
    ckv = _rms(ackv.astype(F32), akvn_ref[...]).astype(BF16)
    rope_heads(jnp.concatenate([ckv, akr], axis=1), wk_ref, ka_ref, 1.0)
    va_ref[0] = _dot(ckv, wv_ref[...]).astype(BF16)

    nrep = B_HEADS * B_HD // LANE
    cosb, sinb = cosb_ref[...], sinb_ref[...]
    qb = (bq.astype(F32) * jnp.concatenate([cosb] * nrep, axis=1)
          + _dot(bq, permbq_ref[...]) * jnp.concatenate([sinb] * nrep, axis=1))
    qb_ref[0] = (qb * (B_HD ** -0.5)).astype(BF16)
    kb_ref[0] =(bk.astype(F32) * cosb + _dot(bk, permbk_ref[...]) * sinb).astype(BF16)

    cosd, sind = cosd_ref[...], sind_ref[...]

    def norm_rope(x, perm_ref, nw_ref, nwp_ref, heads):
        xp = _dot(x, perm_ref[...])
        xf = x.astype(F32)
        wc, ws = nw_ref[...] * cosd, nwp_ref[...] * sind
        parts = []
        for h in range(heads):
            hs = slice(h * D_HD, (h + 1) * D_HD)
            xh = xf[:, hs]
            inv = lax.rsqrt(jnp.mean(xh * xh, axis=-1, keepdims=True) + EPS)
            parts.append(inv * (xh * wc + xp[:, hs] * ws))
        return jnp.concatenate(parts, axis=1)

    qd_ref[0] = (norm_rope(dq, permdq_ref, dqn_ref, dqnp_ref, D_HEADS) * (D_HD ** -0.5)).astype(BF16)
    kd_ref[0] = norm_rope(dk, permdk_ref, dkn_ref, dknp_ref, D_KV_HEADS).astype(BF16)


def _attn_kernel(q_ref, k_ref, v_ref, o_ref, *, dk, dv, kv_group):
    rows = min(ATTN_CHAIN_ROWS, q_ref.shape[1])
    for r in range(q_ref.shape[1] // rows):
        rs = slice(r * rows, (r + 1) * rows)
        for c in range(ATTN_HEADS_PER_STEP):
            kc = c // kv_group
            s = _dot_nt(q_ref[0, rs, c * dk:(c + 1) * dk], k_ref[0, :, kc * dk:(kc + 1) * dk])
            m = jnp.max(s, axis=-1, keepdims=True)
            p = jnp.exp(s - m)
            l = jnp.sum(p, axis=-1, keepdims=True)
            o = _dot(p.astype(BF16), v_ref[0, :, kc * dv:(kc + 1) * dv])
            o_ref[0, rs, c * dv:(c + 1) * dv] = (o / l).astype(o_ref.dtype)


def _attention(q, k, v, *, heads, kv_group, dk, dv, vcol0, t, cn, with_ctx, name):
    b, s, _ = q.shape
    hps = ATTN_HEADS_PER_STEP
    nkv = hps // kv_group
    kern = functools.partial(_attn_kernel, dk=dk, dv=dv, kv_group=kv_group)
    sem = _cparams(("parallel", "parallel", "parallel"))

    def call(tq, nq, row0, krows, krow0, nm):
        return pl.pallas_call(
            kern,
            grid=(b, heads // hps, nq // tq),
            in_specs=[
                pl.BlockSpec((1, tq, hps * dk), lambda bb, h, i: (bb, row0 + i, h)),
                pl.BlockSpec((1, krows, nkv * dk), lambda bb, h, i: (bb, krow0, h)),
                pl.BlockSpec((1, krows, nkv * dv), lambda bb, h, i: (bb, krow0, vcol0 + h)),
            ],
            out_specs=pl.BlockSpec((1, tq, hps * dv), lambda bb, h, i: (bb, i, h)),
            out_shape=jax.ShapeDtypeStruct((b, nq, heads * dv), BF16),
            compiler_params=sem,
            name=nm,
        )(q, k, v)

    y = call(ATTN_TQ, t, 0, s, 0, name)
    if with_ctx:
        return y, call(cn, cn, t // cn, cn, t // cn, name + "_ctx")
    return y


def _swa_bias():
    band = SWA_BLOCK + 2 * WINDOW
    r = np.arange(SWA_BLOCK)[:, None]
    j = np.arange(band)[None, :]
    out = []
    for off in (0, WINDOW, band - SWA_BLOCK):
        out.append(np.where(np.abs(off + r - j) <= WINDOW, 0.0, NEG_INF))
    return np.stack(out).astype(np.float32)


def _swa_kernel(sink_ref, q_ref, k_ref, v_ref, bias_ref, o_ref, s_ref, p_ref, e_ref, *, t, cn, layer):
    n = pl.program_id(1)
    blk = SWA_BLOCK
    nlat = t // blk
    grp = B_HEADS // B_KV_HEADS
    band = blk + 2 * WINDOW
    rt = SWA_ROW_TILE

    def run(is_ctx):
        q = q_ref[0]
        k, v = k_ref[0, t:t + cn, :], v_ref[0, t:t + cn, :]
        if not is_ctx:
            start = pl.multiple_of(jnp.clip(n * blk - WINDOW, 0, t - band), WINDOW)
            k = jnp.concatenate([k, k_ref[0, pl.ds(start, band), :]], axis=0)
            v = jnp.concatenate([v, v_ref[0, pl.ds(start, band), :]], axis=0)
        nk = k.shape[0]
        halves = []
        qlane = lax.broadcasted_iota(jnp.int32, (blk, LANE), 1)
        for hk in range(B_KV_HEADS):
            mine = ((qlane // B_HD) == hk).astype(BF16)
            q4 = jnp.concatenate([q[:, gi * LANE:(gi + 1) * LANE] * mine for gi in range(grp)], axis=0)
            s_ref[hk, :, :nk] = _dot_nt(q4, k)

            for i in range(grp * blk // rt):
                rs = slice(i * rt, (i + 1) * rt)
                sink = sink_ref[layer, hk * grp + i // (blk // rt)]
                s = s_ref[hk, rs, :nk]
                if not is_ctx:
                    br = (i % (blk // rt)) * rt
                    s = s + bias_ref[0, br:br + rt, :]
                m = jnp.maximum(jnp.max(s, axis=-1, keepdims=True), sink)
                p = jnp.exp(s - m)
                l = jnp.sum(p, axis=-1, keepdims=True) + jnp.exp(sink - m)
                p_ref[hk, rs, :nk] = p.astype(BF16)
                e_ref[hk, rs, :] = jnp.broadcast_to(l, (rt, LANE))
            halves.append(_dot(p_ref[hk, :, :nk], v) / e_ref[hk])
        lane = lax.broadcasted_iota(jnp.int32, (blk, LANE), 1)
        outs = [jnp.where(lane < B_HD, halves[0][gi * blk:(gi + 1) * blk], halves[1][gi * blk:(gi + 1) * blk])
                for gi in range(grp)]
        o_ref[0] = jnp.concatenate(outs, axis=1).astype(o_ref.dtype)

    @pl.when(n < nlat)
    def _():
        run(False)

    @pl.when(n >= nlat)
    def _():
        run(True)


def _swa(sink, q, k, p, l, *, t, cn, n_blocks):
    b, s, _ = q.shape
    nlat = t // SWA_BLOCK
    rows = (B_HEADS // B_KV_HEADS) * SWA_BLOCK
    band_bias = _swa_bias()
    bias = jnp.asarray(np.concatenate([np.zeros(band_bias.shape[:2] + (cn,), np.float32), band_bias], axis=2))
    return pl.pallas_call(
        functools.partial(_swa_kernel, t=t, cn=cn, layer=l),
        grid=(b, n_blocks),
        in_specs=[
            pl.BlockSpec(memory_space=pltpu.SMEM),
            pl.BlockSpec((1, SWA_BLOCK, B_HEADS * B_HD), lambda bb, n: (bb, n, 0)),
            pl.BlockSpec((1, s, B_KV_HEADS * B_HD), lambda bb, n: (bb, 0, 0)),
            pl.BlockSpec((1, s, SEG_BV[1]), lambda bb, n: (bb, 0, SEG_BV[0] // SEG_BV[1])),
            pl.BlockSpec((1,) + bias.shape[1:],
                         lambda bb, n: (jnp.where(n == 0, 0, jnp.where(n >= nlat - 1, 2, 1)), 0, 0)),
        ],
        out_specs=pl.BlockSpec((1, SWA_BLOCK, B_HEADS * B_HD), lambda bb, n: (bb, n, 0)),
        out_shape=jax.ShapeDtypeStruct((b, n_blocks * SWA_BLOCK, B_HEADS * B_HD), BF16),
        scratch_shapes=[pltpu.VMEM((B_KV_HEADS, rows, bias.shape[2]), F32),
                        pltpu.VMEM((B_KV_HEADS, rows, bias.shape[2]), BF16),
                        pltpu.VMEM((B_KV_HEADS, rows, LANE), F32)],
        compiler_params=_cparams(("parallel", "parallel")),
        name="swa",
    )(sink, q, k, p, bias)


GLA_BLOCK = 256
GLA_SAFE_DECAY = 60.0


def _log_decay(lr, w, bias):
    z = _dot(lr, w) + bias
    return (jnp.minimum(z, 0.0) - jnp.log(1.0 + jnp.exp(-jnp.abs(z)))) / C_DECAY_TEMP


def _cum_decay(la, tri):
    hi = la.astype(BF16)
    lo = (la - hi.astype(F32)).astype(BF16)
    return _dot(tri, hi) + _dot(tri, lo)


def _gla_kernel(qf_ref, kf_ref, vf_ref, lrf_ref, qb_ref, kb_ref, vb_ref, lrb_ref,
                wdec_ref, bdec_ref, sel_ref, of_ref, ob_ref, st_ref, e_ref):
    n = GLA_BLOCK
    pair = LANE // C_DK
    npair = C_HEADS // pair
    qscale = C_DK ** -0.5
    refs = ((qf_ref, kf_ref, vf_ref, lrf_ref), (qb_ref, kb_ref, vb_ref, lrb_ref))
    outs = (of_ref, ob_ref)

    @pl.when(pl.program_id(1) == 0)
    def _():
        st_ref[...] = jnp.zeros_like(st_ref)

    r_i = lax.broadcasted_iota(jnp.int32, (n, n), 0)
    c_i = lax.broadcasted_iota(jnp.int32, (n, n), 1)
    keep = (c_i <= r_i, c_i >= r_i)
    lane = lax.broadcasted_iota(jnp.int32, (1, LANE), 1)
    head_mask = [(lane // C_DK) == hh for hh in range(pair)]

    cums, tots = [], []
    for d in range(2):
        la = _log_decay(refs[d][3][0], wdec_ref[d], bdec_ref[d])
        cum = _cum_decay(la, keep[d].astype(BF16))
        cums.append(cum)
        tots.append(cum[n - 1:n] if d == 0 else cum[0:1])
    safe = jnp.maximum(jnp.max(-tots[0]), jnp.max(-tots[1])) <= GLA_SAFE_DECAY

    @pl.when(safe)
    def _():
        for d in range(2):
            q_ref, k_ref, v_ref, _ = refs[d]
            cum, tot = cums[d], tots[d]
            half = 0.5 * tot
            a = cum - half
            eh = jnp.exp(half)
            qr = q_ref[0].astype(F32) * qscale * jnp.exp(a)
            kr = k_ref[0].astype(F32) * jnp.exp(-a)
            kr_b, kd_b = kr.astype(BF16), (kr * eh).astype(BF16)
            qe = qr * eh
            v, st = v_ref[0], st_ref[d]
            o_parts, u_parts = [], []
            for p in range(npair):
                ls = slice(p * LANE, (p + 1) * LANE)
                st2 = st[:, ls].astype(BF16)
                upd = []
                for hh in range(pair):
                    h = p * pair + hh
                    vh = v[:, h * C_DV:(h + 1) * C_DV]
                    qr_h = jnp.where(head_mask[hh], qr[:, ls], 0.0).astype(BF16)
                    qe_h = jnp.where(head_mask[hh], qe[:, ls], 0.0).astype(BF16)
                    pm = jnp.where(keep[d], _dot_nt(qr_h, kr_b[:, ls]), 0.0).astype(BF16)
                    o_parts.append(_dot(pm, vh) + _dot_nt(qe_h, st2))
                    upd.append(_dot_tn(vh, kd_b[:, ls]))
                u_parts.append(jnp.where(head_mask[0], upd[0], upd[1]))
            outs[d][0] = jnp.concatenate(o_parts, axis=1)
            st_ref[d] = st * jnp.exp(tot) + jnp.concatenate(u_parts, axis=1)

    @pl.when(jnp.logical_not(safe))
    def _():
        c = C_CHUNK
        tile = 16
        nsub = n // c
        r64 = lax.broadcasted_iota(jnp.int32, (c, c), 0)
        c64 = lax.broadcasted_iota(jnp.int32, (c, c), 1)
        tri64 = ((c64 <= r64).astype(BF16), (c64 >= r64).astype(BF16))
        for d in range(2):
            q_ref, k_ref, v_ref, lr_ref = refs[d]
            e_ref[...] = jnp.zeros_like(e_ref)

            def sub(i, carry, d=d, q_ref=q_ref, k_ref=k_ref, v_ref=v_ref, lr_ref=lr_ref):
                rows = pl.ds(pl.multiple_of((i if d == 0 else nsub - 1 - i) * c, c), c)
                q = q_ref[0, rows, :].astype(F32) * qscale
                k = k_ref[0, rows, :].astype(F32)
                v = v_ref[0, rows, :]
                cum = _cum_decay(_log_decay(lr_ref[0, rows, :], wdec_ref[d], bdec_ref[d]), tri64[d])
                tot = cum[c - 1:c] if d == 0 else cum[0:1]
                qe = (q * jnp.exp(cum)).astype(BF16)
                kd = (k * jnp.exp(tot - cum)).astype(BF16)
                for p in range(npair):
                    ls = slice(p * LANE, (p + 1) * LANE)
                    q2, k2, c2 = q[:, ls], k[:, ls], cum[:, ls]
                    for s in range(c):
                        bt = (s // tile) * tile
                        t0, t1 = (bt, c) if d == 0 else (0, bt + tile)
                        e = q2[t0:t1] * k2[s:s + 1] * jnp.exp(c2[t0:t1] - c2[s:s + 1])
                        ridx = lax.broadcasted_iota(jnp.int32, (t1 - t0, LANE), 0) + t0
                        causal = (ridx >= s) if d == 0 else (ridx <= s)
                        e_ref[p * c + t0:p * c + t1, s * LANE:(s + 1) * LANE] = (
                            jnp.where(causal, e, 0.0).astype(BF16))
                pm = _dot(e_ref[...], sel_ref[...])
                st = st_ref[d]
                o_parts, u_parts = [], []
                for h in range(C_HEADS):
                    p, hh = divmod(h, pair)
                    ph = pm[p * c:(p + 1) * c, hh * c:(hh + 1) * c].astype(BF16)
                    vh = v[:, h * C_DV:(h + 1) * C_DV]
                    hs = slice(h * C_DK, (h + 1) * C_DK)
                    o_parts.append(_dot(ph, vh) + _dot_nt(qe[:, hs], st[:, hs].astype(BF16)))
                    u_parts.append(_dot_tn(vh, kd[:, hs]))
                outs[d][0, rows, :] = jnp.concatenate(o_parts, axis=1)
                st_ref[d] = st * jnp.exp(tot) + jnp.concatenate(u_parts, axis=1)
                return carry

            lax.fori_loop(0, nsub, sub, 0)


def _gla_sel():
    pair = LANE // C_DK
    sel = np.zeros((C_CHUNK, pair, C_DK, pair, C_CHUNK), np.float32)
    for s in range(C_CHUNK):
        for h in range(pair):
            sel[s, h, :, h, s] = 1.0
    return jnp.asarray(sel.reshape(C_CHUNK * LANE, pair * C_CHUNK), BF16)


def _gla(p, wdec, bdec, sel, l, *, t, cn):
    b, s, _ = p.shape
    n = GLA_BLOCK
    nbl, nbc = t // n, cn // n

    def fwd(j):
        return jnp.where(j < nbc, nbl + j, j - nbc)

    def bwd(j):
        return nbl + nbc - 1 - j

    def specs(order):
        return [pl.BlockSpec((1, n, sg[1]), lambda bb, j, sg=sg: (bb, order(j), sg[0] // sg[1]))
                for sg in (SEG_CQ, SEG_CK, SEG_CV, SEG_CLR)]

    width = C_HEADS * C_DV
    return pl.pallas_call(
        _gla_kernel,
        grid=(b, nbl + nbc),
        in_specs=specs(fwd) + specs(bwd) + [
            _layer_resident(wdec, l), _layer_resident(bdec, l), _resident(sel.shape),
        ],
        out_specs=[pl.BlockSpec((1, n, width), lambda bb, j: (bb, fwd(j), 0)),
                   pl.BlockSpec((1, n, width), lambda bb, j: (bb, bwd(j), 0))],
        out_shape=[jax.ShapeDtypeStruct((b, s, width), F32)] * 2,
        scratch_shapes=[pltpu.VMEM((2, C_DV, C_HEADS * C_DK), F32),
                        pltpu.VMEM(((C_HEADS * C_DK // LANE) * C_CHUNK, C_CHUNK * LANE), BF16)],
        compiler_params=_cparams(("parallel", "arbitrary")),
        name="gla",
    )(*([p] * 8), wdec, bdec, sel)


def _merge_mlp_kernel(*refs, d, nsrc, nlat_blocks, final):
    x_refs, ya_refs, yd_refs = (refs[i * nsrc:(i + 1) * nsrc] for i in range(3))
    (mod_ref, yb_ref, of_ref, ob_ref, r_ref, g_ref, gnw_ref, wba_ref, wbb_ref, wbc_ref, wbd_ref, wo_ref,
     nw2_ref, w1_ref, w2_ref, fnw_ref, o_ref) = refs[3 * nsrc:]
    mod = mod_ref[0]
    o = of_ref[0] + ob_ref[0]
    r = r_ref[0].astype(F32)
    yg = jnp.concatenate([_rms(o[:, h * C_DV:(h + 1) * C_DV], gnw_ref[...]) for h in range(C_HEADS)], axis=1)
    yg = (yg * (r * jax.nn.sigmoid(r))).astype(BF16)
    branches = ((_token_rows(ya_refs, nlat_blocks), wba_ref), (yb_ref[0], wbb_ref), (yg, wbc_ref),
                (_token_rows(yd_refs, nlat_blocks), wbd_ref))
    mixed = []
    for c in range(d // MERGE_TN):
        cs = slice(c * MERGE_TN, (c + 1) * MERGE_TN)
        acc_z = acc_t = None
        for n, (y, w_ref) in enumerate(branches):
            z = _dot(y, w_ref[:, cs])
            tz = jnp.tanh(g_ref[0, :, n * d + c * MERGE_TN:n * d + (c + 1) * MERGE_TN].astype(F32)) * z
            acc_z = z if acc_z is None else acc_z + z
            acc_t = tz if acc_t is None else acc_t + tz
        mixed.append((0.5 * (acc_z + acc_t)).astype(BF16))
    x1 = _token_rows(x_refs, nlat_blocks) + mod[:, 2 * d:3 * d] * _dot(jnp.concatenate(mixed, axis=1), wo_ref[...])
    h = _norm_modulate(x1, nw2_ref[...], [mod], 3, d)
    a = jnp.maximum(_dot(h.astype(BF16), w1_ref[...]), 0.0)
    y = x1 + mod[:, 5 * d:6 * d] * _dot((a * a).astype(BF16), w2_ref[...])
    o_ref[0] = _rms(y, fnw_ref[...]) if final else y


def _merge_mlp(xs, modsel, ya, yb, o_f, o_b, p, yd, gnw, w_branch, wb_b, wo, nw2, w1, w2, fnw, l, *,
               tm, n_blocks, nlat_blocks, final):
    assert isinstance(xs, tuple) == isinstance(ya, tuple) == isinstance(yd, tuple)
    streams = [s if isinstance(s, tuple) else (s,) for s in (xs, ya, yd)]
    srcs = [a for s in streams for a in s]
    b, d = srcs[0].shape[0], srcs[0].shape[2]

    def tok(w, col=0):
        return pl.BlockSpec((1, tm, w), lambda bb, i: (bb, i, col))

    w_specs = [_layer_resident(w_branch, l, 0), _layer_resident(wb_b, l),
               _layer_resident(w_branch, l, 2), _layer_resident(w_branch, l, 3), _layer_resident(wo, l),
               _layer_resident(nw2, l), _layer_resident(w1, l), _layer_resident(w2, l), _resident(fnw.shape)]
    return pl.pallas_call(
        functools.partial(_merge_mlp_kernel, d=d, nsrc=len(streams[0]), nlat_blocks=nlat_blocks, final=final),
        grid=(b, n_blocks),
        in_specs=[sp for s in (xs, ya, yd) for sp in _token_specs(s, tm, nlat_blocks)] + [
            _mod_spec(modsel, l, nlat_blocks),
            tok(yb.shape[2]), tok(o_f.shape[2]), tok(o_b.shape[2]),
            tok(SEG_CR[1], SEG_CR[0] // SEG_CR[1]),
            tok(SEG_GATE[1], 0),
            _layer_resident(gnw, l), *w_specs,
        ],
        out_specs=tok(d),
        out_shape=jax.ShapeDtypeStruct((b, n_blocks * tm, d), F32),
        compiler_params=_cparams(("parallel", "parallel")),
        name="merge_mlp",
    )(*srcs, modsel, yb, o_f, o_b, p, p, gnw, w_branch, wb_b, w_branch, w_branch, wo, nw2, w1, w2, fnw)


def _pad_cols(w, n):
    return jnp.concatenate([w, jnp.zeros(w.shape[:-1] + (n,), w.dtype)], axis=-1)


def _layout_w_in(w):
    a0 = 0
    b0 = a0 + A_Q_RANK + A_KV_RANK + A_ROPE
    c0 = b0 + (B_HEADS + 2 * B_KV_HEADS) * B_HD
    d0 = c0 + 2 * C_HEADS * C_DK + 2 * C_HEADS * C_DV + 2 * C_DECAY_RANK
    g0 = d0 + (D_HEADS + 2 * D_KV_HEADS) * D_HD

    def cols(lo, n):
        return w[..., lo:lo + n]

    cq = c0
    ck = cq + C_HEADS * C_DK
    cv = ck + C_HEADS * C_DK
    cr = cv + C_HEADS * C_DV
    clr = cr + C_HEADS * C_DV
    parts = [
        cols(g0, SEG_GATE[1]),
        cols(d0, SEG_DQ[1] + SEG_DK[1] + SEG_DV[1]),
        *[cols(b0 + h * B_HD, B_HD) for h in _swa_head_order()],
        cols(b0 + SEG_BQ[1], SEG_BK[1] + SEG_BV[1]),
        cols(a0, A_Q_RANK + A_KV_RANK),
        _pad_cols(cols(a0 + A_Q_RANK + A_KV_RANK, A_ROPE), SEG_AKR[1] - A_ROPE),
        cols(cq, SEG_CQ[1]), cols(cv, SEG_CV[1]), cols(cr, SEG_CR[1]), cols(ck, SEG_CK[1]),
        _pad_cols(cols(clr, 2 * C_DECAY_RANK), P_COLS - SEG_CLR[0] - 2 * C_DECAY_RANK),
    ]
    return jnp.concatenate(parts, axis=-1).astype(BF16)


def _layout_mla(w_uq, w_ukv):
    zq = jnp.zeros((A_Q_RANK, A_HEAD_PAD - A_NOPE - A_ROPE), w_uq.dtype)
    hq = A_NOPE + A_ROPE
    wq = jnp.concatenate([jnp.concatenate([w_uq[:, h * hq:(h + 1) * hq], zq], axis=1)
                          for h in range(A_HEADS)], axis=1)
    hk = A_NOPE + A_V
    zk = jnp.zeros((A_KV_RANK, A_HEAD_PAD - A_NOPE), w_ukv.dtype)
    wk_top = jnp.concatenate([jnp.concatenate([w_ukv[:, h * hk:h * hk + A_NOPE], zk], axis=1)
                              for h in range(A_HEADS)], axis=1)
    place = np.zeros((SEG_AKR[1], A_HEADS * A_HEAD_PAD), np.float32)
    for h in range(A_HEADS):
        place[np.arange(A_ROPE), h * A_HEAD_PAD + A_NOPE + np.arange(A_ROPE)] = 1.0
    wk = jnp.concatenate([wk_top, jnp.asarray(place)], axis=0)
    wv = jnp.concatenate([w_ukv[:, h * hk + A_NOPE:(h + 1) * hk] for h in range(A_HEADS)], axis=1)
    lanes = np.arange(A_HEADS * A_HEAD_PAD)
    rot = ((lanes % A_HEAD_PAD) >= A_NOPE) & ((lanes % A_HEAD_PAD) < A_NOPE + A_ROPE)
    idx = _partner_lanes(lanes.size, A_ROPE // 4)

    def with_partner(w):
        return jnp.concatenate([w, jnp.where(jnp.asarray(rot)[None, :], w[:, idx], 0.0)], axis=1)

    return with_partner(wq).astype(BF16), with_partner(wk).astype(BF16), wv.astype(BF16)


def _partner_lanes(width, blk):
    j = np.arange(width)
    return np.where((j % (2 * blk)) < blk, j + blk, j - blk)


def _perm_matrix(width, blk):
    m = np.zeros((width, width), np.float32)
    m[_partner_lanes(width, blk), np.arange(width)] = 1.0
    return m


def _swa_head_order():
    grp = B_HEADS // B_KV_HEADS
    return [h for g in range(grp) for h in (g, grp + g)]


def _layout_w_branch_b(w):
    return jnp.concatenate([w[h * B_HD:(h + 1) * B_HD] for h in _swa_head_order()], axis=0)


def _layout_decay(w_decay):
    z = jnp.zeros((C_DECAY_RANK, w_decay.shape[-1]), w_decay.dtype)
    tail = jnp.zeros((SEG_CLR[1] - 2 * C_DECAY_RANK, w_decay.shape[-1]), w_decay.dtype)
    return jnp.stack([jnp.concatenate([w_decay[0], z, tail], axis=0),
                      jnp.concatenate([z, w_decay[1], tail], axis=0)]).astype(BF16)


def kernel(x, c, ctx, c_ctx, ada_w, ada_b, norm1_w, norm2_w, w_in, mla_q_norm_w, mla_w_uq, mla_kv_norm_w, mla_w_ukv, swa_sink, gla_w_decay, gla_b_decay, gla_norm_w, gqa_q_norm_w, gqa_k_norm_w, w_branch, w_o, mlp_w1, mlp_w2, final_norm_w):
    b, t, d = x.shape
    cn = ctx.shape[1]
    depth = ada_w.shape[0]
    tm = TOKEN_TILE
    assert t % tm == 0 and cn % tm == 0 and t % GRID_W == 0 and t >= SWA_BLOCK + 2 * WINDOW
    assert t % ATTN_TQ == 0 and t % cn == 0 and t % GLA_BLOCK == 0 and cn % GLA_BLOCK == 0
    nlat, nall = t // tm, (t + cn) // tm

    rows = 16
    cs = jnp.concatenate([c, c_ctx[None], jnp.zeros((rows - b - 1, d), F32)], axis=0)
    mod = _ada(cs, ada_w, ada_b)
    modsel = jnp.stack([mod[:, :b], jnp.broadcast_to(mod[:, b:b + 1], (depth, b, 6 * d))], axis=2)
    modsel = modsel.reshape(depth, b * 2, 1, 6 * d)

    tabs = (_rope_tables(t, cn, A_HEAD_PAD, A_NOPE, A_ROPE)
            + tuple(jnp.tile(tb, (1, LANE // B_HD)) for tb in _rope_tables(t, cn, B_HD, 0, B_HD))
            + _rope_tables(t, cn, D_HD, 0, D_HD))
    sel = _gla_sel()
    perms = (jnp.asarray(_perm_matrix(B_HEADS * B_HD, B_HD // 4), BF16),
             jnp.asarray(_perm_matrix(B_KV_HEADS * B_HD, B_HD // 4), BF16),
             jnp.asarray(_perm_matrix(D_HEADS * D_HD, D_HD // 4), BF16),
             jnp.asarray(_perm_matrix(D_KV_HEADS * D_HD, D_HD // 4), BF16))
    d_partner = _partner_lanes(D_HD, D_HD // 4)

    w_p = _layout_w_in(w_in)
    wbr, wo = w_branch.astype(BF16), w_o.astype(BF16)
    wb_b = jax.vmap(_layout_w_branch_b)(w_branch[:, 1]).astype(BF16)
    w1, w2 = mlp_w1.astype(BF16), mlp_w2.astype(BF16)
    wq, wk, wv = jax.vmap(_layout_mla)(mla_w_uq, mla_w_ukv)
    wdec, bdec = jax.vmap(_layout_decay)(gla_w_decay), gla_b_decay[:, :, None, :]

    def rows(w):
        return w[:, None, :]

    prep_wts = ([(rows(mla_q_norm_w), True), (rows(mla_kv_norm_w), True), (wq, True), (wk, True), (wv, True)]
                + [(pm, False) for pm in perms]
                + [(rows(gqa_q_norm_w), True), (rows(gqa_q_norm_w[:, d_partner]), True),
                   (rows(gqa_k_norm_w), True), (rows(gqa_k_norm_w[:, d_partner]), True)])
    n1, n2, gnw = rows(norm1_w), rows(norm2_w), rows(gla_norm_w)

    xs = (x, ctx)
    for l in range(depth):
        last = l == depth - 1
        p, qa, ka, va, qb, kb, qd, kd = _inproj(xs, modsel, n1, w_p, tabs, prep_wts, l, nlat)
        ya = _attention(qa, ka, va, heads=A_HEADS, kv_group=1, dk=A_HEAD_PAD, dv=A_V, vcol0=0,
                        t=t, cn=cn, with_ctx=not last, name="attn_a")
        yd = _attention(qd, kd, p, heads=D_HEADS, kv_group=D_HEADS // D_KV_HEADS, dk=D_HD, dv=D_HD,
                        vcol0=SEG_DV[0] // SEG_DV[1], t=t, cn=cn, with_ctx=not last, name="attn_d")
        yb = _swa(swa_sink, qb, kb, p, l, t=t, cn=cn,
                  n_blocks=(t if last else t + cn) // SWA_BLOCK)
        o_f, o_b = _gla(p, wdec, bdec, sel, l, t=t, cn=cn)
        if last and isinstance(xs, tuple):
            xs = xs[0]
        tile = dict(tm=LAST_TILE, n_blocks=t // LAST_TILE, nlat_blocks=t // LAST_TILE) if last else dict(
            tm=tm, n_blocks=nall, nlat_blocks=nlat)
        xs = _merge_mlp(xs, modsel, ya, yb, o_f, o_b, p, yd, gnw, wbr, wb_b, wo, n2, w1, w2,
                        final_norm_w[None], l, final=last, **tile)
    return xs
```

```python
import functools

import numpy as np
import jax
import jax.numpy as jnp
from jax import lax
from jax.experimental import pallas as pl
from jax.experimental.pallas import tpu as pltpu

F32 = jnp.float32
BF16 = jnp.bfloat16

GRID_W = 64
EPS = 1e-6
ROPE_BASE = 10000.0
NEG_INF = -1e30

A_HEADS, A_NOPE, A_ROPE, A_V, A_Q_RANK, A_KV_RANK = 4, 128, 64, 128, 256, 128
A_HEAD_PAD = 256
B_HEADS, B_KV_HEADS, B_HD, WINDOW = 8, 2, 64, 128
SWA_BLOCK = 256
SWA_ROW_TILE = 32
C_HEADS, C_DK, C_DV, C_DECAY_RANK, C_DECAY_TEMP, C_CHUNK = 4, 64, 128, 16, 16.0, 64
D_HEADS, D_KV_HEADS, D_HD = 4, 2, 128
N_BRANCH, BRANCH_W = 4, 512

LANE = 128
TOKEN_TILE = 256
LAST_TILE = 512
ATTN_TQ = 1024
ATTN_CHAIN_ROWS = 512
ATTN_HEADS_PER_STEP = 4
INPROJ_TN = 2048
MERGE_TN = 256
VMEM_LIMIT = 56 * 1024 * 1024

P_COLS = 8192
SEG_GATE = (0, 4096)
SEG_DQ, SEG_DK, SEG_DV = (4096, 512), (4608, 256), (4864, 256)
SEG_BQ, SEG_BK, SEG_BV = (5120, 512), (5632, 128), (5760, 128)
SEG_ACQ, SEG_ACKV, SEG_AKR = (5888, 256), (6144, 128), (6272, 128)
SEG_CQ, SEG_CV, SEG_CR, SEG_CK, SEG_CLR = (6400, 256), (6656, 512), (7168, 512), (7680, 256), (7936, 128)


def _cparams(sem):
    return pltpu.CompilerParams(dimension_semantics=sem, vmem_limit_bytes=VMEM_LIMIT)


def _rms(xf, w):
    return xf * lax.rsqrt(jnp.mean(xf * xf, axis=-1, keepdims=True) + EPS) * w


def _dot(a, b):
    return jnp.dot(a, b, preferred_element_type=F32)


def _dot_nt(a, b):
    return lax.dot_general(a, b, (((1,), (1,)), ((), ())), preferred_element_type=F32)


def _dot_tn(a, b):
    return lax.dot_general(a, b, (((0,), (0,)), ((), ())), preferred_element_type=F32)


def _rope_tables(t, cn, width, offset, rot):
    half = rot // 2
    q = half // 2
    inv = ROPE_BASE ** (-np.arange(q, dtype=np.float64) / q)
    pos = np.arange(t)
    cos = np.ones((t + cn, width), np.float64)
    sin = np.zeros((t + cn, width), np.float64)
    for part, p in enumerate((pos // GRID_W, pos % GRID_W)):
        ang = p[:, None] * inv[None, :]
        lo = offset + part * half
        cos[:t, lo:lo + half] = np.concatenate([np.cos(ang), np.cos(ang)], axis=1)
        sin[:t, lo:lo + half] = np.concatenate([-np.sin(ang), np.sin(ang)], axis=1)
    return jnp.asarray(cos, F32), jnp.asarray(sin, F32)


def _ada_kernel(cs_ref, w_ref, b_ref, o_ref):
    cs = cs_ref[...]
    s = cs * jax.nn.sigmoid(cs)
    o_ref[0] = _dot(s.astype(BF16), w_ref[0].astype(BF16)) + b_ref[0]


def _ada(cs, ada_w, ada_b):
    nl, d, n = ada_w.shape
    tn = 1536
    rows = cs.shape[0]
    return pl.pallas_call(
        _ada_kernel,
        grid=(nl, n // tn),
        in_specs=[
            pl.BlockSpec((rows, d), lambda l, j: (0, 0)),
            pl.BlockSpec((1, d, tn), lambda l, j: (l, 0, j)),
            pl.BlockSpec((1, 1, tn), lambda l, j: (l, 0, j)),
        ],
        out_specs=pl.BlockSpec((1, rows, tn), lambda l, j: (l, 0, j)),
        out_shape=jax.ShapeDtypeStruct((nl, rows, n), F32),
        compiler_params=_cparams(("parallel", "parallel")),
        name="ada",
    )(cs, ada_w, ada_b.reshape(nl, 1, n))


def _token_specs(xs, tm, nlat_blocks):
    if not isinstance(xs, tuple):
        return [pl.BlockSpec((1, tm, xs.shape[2]), lambda bb, i: (bb, i, 0))]
    x, ctx = xs
    return [pl.BlockSpec((1, tm, x.shape[2]), lambda bb, i: (bb, jnp.minimum(i, nlat_blocks - 1), 0)),
            pl.BlockSpec((1, tm, ctx.shape[2]), lambda bb, i: (bb, jnp.maximum(i - nlat_blocks, 0), 0))]


def _token_rows(x_refs, nlat_blocks):
    if len(x_refs) == 1:
        return x_refs[0][0]
    return jnp.where(pl.program_id(1) >= nlat_blocks, x_refs[1][0], x_refs[0][0])


def _norm_modulate(x, nw, mods, k, d):
    rows = x.shape[0] // len(mods)
    parts = [_rms(x[j * rows:(j + 1) * rows], nw) * (1.0 + m[:, (k + 1) * d:(k + 2) * d]) + m[:, k * d:(k + 1) * d]
             for j, m in enumerate(mods)]
    return parts[0] if len(parts) == 1 else jnp.concatenate(parts, axis=0)


N_PREP_TABLES, N_PREP_WEIGHTS, N_PREP_OUTS = 6, 13, 7


def _inproj_kernel(*refs, d, nsrc, nlat_blocks):
    mod_ref, nw_ref, w_ref = refs[nsrc:nsrc + 3]
    prep_refs = refs[nsrc + 3:nsrc + 3 + N_PREP_TABLES + N_PREP_WEIGHTS]
    o_ref = refs[nsrc + 3 + N_PREP_TABLES + N_PREP_WEIGHTS]
    prep_outs = refs[-N_PREP_OUTS:]
    x = _token_rows(refs[:nsrc], nlat_blocks)
    h = _norm_modulate(x, nw_ref[...], [mod_ref[0]], 0, d).astype(BF16)
    chunks = []
    for j in range(P_COLS // INPROJ_TN):
        cs = slice(j * INPROJ_TN, (j + 1) * INPROJ_TN)
        res = _dot(h, w_ref[:, cs])
        if cs.stop <= SEG_GATE[0] + SEG_GATE[1]:
            res = res * 0.5
        chunks.append(res.astype(BF16))
        o_ref[0, :, cs] = chunks[-1]

    def seg(sg):
        j, lo = divmod(sg[0], INPROJ_TN)
        assert lo + sg[1] <= INPROJ_TN
        return chunks[j][:, lo:lo + sg[1]]

    _prep_body(*[seg(sg) for sg in (SEG_ACQ, SEG_ACKV, SEG_AKR, SEG_BQ, SEG_BK, SEG_DQ, SEG_DK)],
               *prep_refs, *prep_outs)


def _resident(shape):
    return pl.BlockSpec(shape, lambda *_: (0,) * len(shape), pipeline_mode=pl.Buffered(1))


def _layer_resident(arr, *lead):
    rest = arr.shape[len(lead):]
    return pl.BlockSpec((None,) * len(lead) + rest, lambda *_: tuple(lead) + (0,) * len(rest),
                        pipeline_mode=pl.Buffered(1))


def _mod_spec(modsel, l, nlat_blocks):
    return pl.BlockSpec((None, 1) + modsel.shape[2:],
                        lambda bb, i: (l, bb * 2 + jnp.where(i >= nlat_blocks, 1, 0), 0, 0))


def _inproj(xs, modsel, nw, w_p, tabs, wts, l, nlat_blocks):
    srcs = xs if isinstance(xs, tuple) else (xs,)
    b, d = srcs[0].shape[0], srcs[0].shape[2]
    s = sum(a.shape[1] for a in srcs)
    tm = TOKEN_TILE
    assert len(tabs) == N_PREP_TABLES and len(wts) == N_PREP_WEIGHTS
    tab_specs = [pl.BlockSpec((tm, tb.shape[1]), lambda bb, i: (i, 0)) for tb in tabs]
    wt_specs = [_layer_resident(w, l) if stacked else _resident(w.shape) for w, stacked in wts]
    widths = [P_COLS, A_HEADS * A_HEAD_PAD, A_HEADS * A_HEAD_PAD, A_HEADS * A_V,
              B_HEADS * B_HD, B_KV_HEADS * B_HD, D_HEADS * D_HD, D_KV_HEADS * D_HD]
    return pl.pallas_call(
        functools.partial(_inproj_kernel, d=d, nsrc=len(srcs), nlat_blocks=nlat_blocks),
        grid=(b, s // tm),
        in_specs=(_token_specs(xs, tm, nlat_blocks)
                  + [_mod_spec(modsel, l, nlat_blocks), _layer_resident(nw, l), _layer_resident(w_p, l)]
                  + tab_specs + wt_specs),
        out_specs=[pl.BlockSpec((1, tm, w), lambda bb, i: (bb, i, 0)) for w in widths],
        out_shape=[jax.ShapeDtypeStruct((b, s, w), BF16) for w in widths],
        compiler_params=_cparams(("parallel", "parallel")),
        name="inproj",
    )(*srcs, modsel, nw, w_p, *tabs, *[w for w, _ in wts])


def _prep_body(acq, ackv, akr, bq, bk, dq, dk,
               cosa_ref, sina_ref, cosb_ref, sinb_ref, cosd_ref, sind_ref,
               aqn_ref, akvn_ref, wq_ref, wk_ref, wv_ref, permbq_ref, permbk_ref,
               permdq_ref, permdk_ref, dqn_ref, dqnp_ref, dkn_ref, dknp_ref,
               qa_ref, ka_ref, va_ref, qb_ref, kb_ref, qd_ref, kd_ref):
    na = A_HEADS * A_HEAD_PAD
    cos_r, sin_r = cosa_ref[:, A_NOPE:], sina_ref[:, A_NOPE:]

    def rope_heads(x, w_ref, o_ref, scale):
        for h in range(A_HEADS):
            c0 = h * A_HEAD_PAD
            main = _dot(x, w_ref[:, c0:c0 + A_HEAD_PAD])
            part = _dot(x, w_ref[:, na + c0 + A_NOPE:na + c0 + A_HEAD_PAD])
            out = jnp.concatenate([main[:, :A_NOPE], main[:, A_NOPE:] * cos_r + part * sin_r], axis=1)
            o_ref[0, :, c0:c0 + A_HEAD_PAD] = (out * scale).astype(BF16)

    cq = _rms(acq.astype(F32), aqn_ref[...]).astype(BF16)
    rope_heads(cq, wq_ref, qa_ref, (A_NOPE + A_ROPE) ** -0.5)
    ckv = _rms(ackv.astype(F32), akvn_ref[...]).astype(BF16)
    rope_heads(jnp.concatenate([ckv, akr], axis=1), wk_ref, ka_ref, 1.0)
    va_ref[0] = _dot(ckv, wv_ref[...]).astype(BF16)

    nrep = B_HEADS * B_HD // LANE
    cosb, sinb = cosb_ref[...], sinb_ref[...]
    qb = (bq.astype(F32) * jnp.concatenate([cosb] * nrep, axis=1)
          + _dot(bq, permbq_ref[...]) * jnp.concatenate([sinb] * nrep, axis=1))
    qb_ref[0] = (qb * (B_HD ** -0.5)).astype(BF16)
    kb_ref[0] =(bk.astype(F32) * cosb + _dot(bk, permbk_ref[...]) * sinb).astype(BF16)

    cosd, sind = cosd_ref[...], sind_ref[...]

    def norm_rope(x, perm_ref, nw_ref, nwp_ref, heads):
        xp = _dot(x, perm_ref[...])
        xf = x.astype(F32)
        wc, ws = nw_ref[...] * cosd, nwp_ref[...] * sind
        parts = []
        for h in range(heads):
            hs = slice(h * D_HD, (h + 1) * D_HD)
            xh = xf[:, hs]
            inv = lax.rsqrt(jnp.mean(xh * xh, axis=-1, keepdims=True) + EPS)
            parts.append(inv * (xh * wc + xp[:, hs] * ws))
        return jnp.concatenate(parts, axis=1)

    qd_ref[0] = (norm_rope(dq, permdq_ref, dqn_ref, dqnp_ref, D_HEADS) * (D_HD ** -0.5)).astype(BF16)
    kd_ref[0] = norm_rope(dk, permdk_ref, dkn_ref, dknp_ref, D_KV_HEADS).astype(BF16)


def _attn_kernel(q_ref, k_ref, v_ref, o_ref, *, dk, dv, kv_group):
    rows = min(ATTN_CHAIN_ROWS, q_ref.shape[1])
    for c in range(ATTN_HEADS_PER_STEP):
        kc = c // kv_group
        s_all = _dot_nt(q_ref[0, :, c * dk:(c + 1) * dk], k_ref[0, :, kc * dk:(kc + 1) * dk])
        for r in range(q_ref.shape[1] // rows):
            rs = slice(r * rows, (r + 1) * rows)
            s = s_all[rs]
            m = jnp.max(s, axis=-1, keepdims=True)
            p = jnp.exp(s - m)
            l = jnp.sum(p, axis=-1, keepdims=True)
            o = _dot(p.astype(BF16), v_ref[0, :, kc * dv:(kc + 1) * dv])
            o_ref[0, rs, c * dv:(c + 1) * dv] = (o / l).astype(o_ref.dtype)


def _attention(q, k, v, *, heads, kv_group, dk, dv, vcol0, t, cn, with_ctx, name):
    b, s, _ = q.shape
    hps = ATTN_HEADS_PER_STEP
    nkv = hps // kv_group
    kern = functools.partial(_attn_kernel, dk=dk, dv=dv, kv_group=kv_group)
    sem = _cparams(("parallel", "parallel", "parallel"))

    def call(tq, nq, row0, krows, krow0, nm):
        return pl.pallas_call(
            kern,
            grid=(b, heads // hps, nq // tq),
            in_specs=[
                pl.BlockSpec((1, tq, hps * dk), lambda bb, h, i: (bb, row0 + i, h)),
                pl.BlockSpec((1, krows, nkv * dk), lambda bb, h, i: (bb, krow0, h)),
                pl.BlockSpec((1, krows, nkv * dv), lambda bb, h, i: (bb, krow0, vcol0 + h)),
            ],
            out_specs=pl.BlockSpec((1, tq, hps * dv), lambda bb, h, i: (bb, i, h)),
            out_shape=jax.ShapeDtypeStruct((b, nq, heads * dv), BF16),
            compiler_params=sem,
            name=nm,
        )(q, k, v)

    y = call(ATTN_TQ, t, 0, s, 0, name)
    if with_ctx:
        return y, call(cn, cn, t // cn, cn, t // cn, name + "_ctx")
    return y


def _swa_bias():
    band = SWA_BLOCK + 2 * WINDOW
    r = np.arange(SWA_BLOCK)[:, None]
    j = np.arange(band)[None, :]
    out = []
    for off in (0, WINDOW, band - SWA_BLOCK):
        out.append(np.where(np.abs(off + r - j) <= WINDOW, 0.0, NEG_INF))
    return np.stack(out).astype(np.float32)


def _swa_kernel(sink_ref, q_ref, k_ref, v_ref, bias_ref, o_ref, s_ref, p_ref, e_ref, *, t, cn, layer):
    n = pl.program_id(1)
    blk = SWA_BLOCK
    nlat = t // blk
    grp = B_HEADS // B_KV_HEADS
    band = blk + 2 * WINDOW
    rt = SWA_ROW_TILE

    def run(is_ctx):
        q = q_ref[0]
        k, v = k_ref[0, t:t + cn, :], v_ref[0, t:t + cn, :]
        if not is_ctx:
            start = pl.multiple_of(jnp.clip(n * blk - WINDOW, 0, t - band), WINDOW)
            k = jnp.concatenate([k, k_ref[0, pl.ds(start, band), :]], axis=0)
            v = jnp.concatenate([v, v_ref[0, pl.ds(start, band), :]], axis=0)
        nk = k.shape[0]
        halves = []
        qlane = lax.broadcasted_iota(jnp.int32, (blk, LANE), 1)
        for hk in range(B_KV_HEADS):
            mine = ((qlane // B_HD) == hk).astype(BF16)
            q4 = jnp.concatenate([q[:, gi * LANE:(gi + 1) * LANE] * mine for gi in range(grp)], axis=0)
            s_ref[hk, :, :nk] = _dot_nt(q4, k)

            for i in range(grp * blk // rt):
                rs = slice(i * rt, (i + 1) * rt)
                sink = sink_ref[layer, hk * grp + i // (blk // rt)]
                s = s_ref[hk, rs, :nk]
                if not is_ctx:
                    br = (i % (blk // rt)) * rt
                    s = s + bias_ref[0, br:br + rt, :]
                m = jnp.maximum(jnp.max(s, axis=-1, keepdims=True), sink)
                p = jnp.exp(s - m)
                l = jnp.sum(p, axis=-1, keepdims=True) + jnp.exp(sink - m)
                p_ref[hk, rs, :nk] = p.astype(BF16)
                e_ref[hk, rs, :] = jnp.broadcast_to(l, (rt, LANE))
            halves.append(_dot(p_ref[hk, :, :nk], v) / e_ref[hk])
        lane = lax.broadcasted_iota(jnp.int32, (blk, LANE), 1)
        outs = [jnp.where(lane < B_HD, halves[0][gi * blk:(gi + 1) * blk], halves[1][gi * blk:(gi + 1) * blk])
                for gi in range(grp)]
        o_ref[0] = jnp.concatenate(outs, axis=1).astype(o_ref.dtype)

    @pl.when(n < nlat)
    def _():
        run(False)

    @pl.when(n >= nlat)
    def _():
        run(True)


def _swa(sink, q, k, p, l, *, t, cn, n_blocks):
    b, s, _ = q.shape
    nlat = t // SWA_BLOCK
    rows = (B_HEADS // B_KV_HEADS) * SWA_BLOCK
    band_bias = _swa_bias()
    bias = jnp.asarray(np.concatenate([np.zeros(band_bias.shape[:2] + (cn,), np.float32), band_bias], axis=2))
    return pl.pallas_call(
        functools.partial(_swa_kernel, t=t, cn=cn, layer=l),
        grid=(b, n_blocks),
        in_specs=[
            pl.BlockSpec(memory_space=pltpu.SMEM),
            pl.BlockSpec((1, SWA_BLOCK, B_HEADS * B_HD), lambda bb, n: (bb, n, 0)),
            pl.BlockSpec((1, s, B_KV_HEADS * B_HD), lambda bb, n: (bb, 0, 0)),
            pl.BlockSpec((1, s, SEG_BV[1]), lambda bb, n: (bb, 0, SEG_BV[0] // SEG_BV[1])),
            pl.BlockSpec((1,) + bias.shape[1:],
                         lambda bb, n: (jnp.where(n == 0, 0, jnp.where(n >= nlat - 1, 2, 1)), 0, 0)),
        ],
        out_specs=pl.BlockSpec((1, SWA_BLOCK, B_HEADS * B_HD), lambda bb, n: (bb, n, 0)),
        out_shape=jax.ShapeDtypeStruct((b, n_blocks * SWA_BLOCK, B_HEADS * B_HD), BF16),
        scratch_shapes=[pltpu.VMEM((B_KV_HEADS, rows, bias.shape[2]), F32),
                        pltpu.VMEM((B_KV_HEADS, rows, bias.shape[2]), BF16),
                        pltpu.VMEM((B_KV_HEADS, rows, LANE), F32)],
        compiler_params=_cparams(("parallel", "parallel")),
        name="swa",
    )(sink, q, k, p, bias)


GLA_BLOCK = 256
GLA_SAFE_DECAY = 60.0


def _log_decay(lr, w, bias):
    z = _dot(lr, w) + bias
    return (jnp.minimum(z, 0.0) - jnp.log(1.0 + jnp.exp(-jnp.abs(z)))) / C_DECAY_TEMP


def _cum_decay(la, tri):
    hi = la.astype(BF16)
    lo = (la - hi.astype(F32)).astype(BF16)
    return _dot(tri, hi) + _dot(tri, lo)


def _gla_kernel(qf_ref, kf_ref, vf_ref, lrf_ref, qb_ref, kb_ref, vb_ref, lrb_ref,
                wdec_ref, bdec_ref, sel_ref, of_ref, ob_ref, st_ref, e_ref):
    n = GLA_BLOCK
    pair = LANE // C_DK
    npair = C_HEADS // pair
    qscale = C_DK ** -0.5
    refs = ((qf_ref, kf_ref, vf_ref, lrf_ref), (qb_ref, kb_ref, vb_ref, lrb_ref))
    outs = (of_ref, ob_ref)

    @pl.when(pl.program_id(1) == 0)
    def _():
        st_ref[...] = jnp.zeros_like(st_ref)

    r_i = lax.broadcasted_iota(jnp.int32, (n, n), 0)
    c_i = lax.broadcasted_iota(jnp.int32, (n, n), 1)
    keep = (c_i <= r_i, c_i >= r_i)
    lane = lax.broadcasted_iota(jnp.int32, (1, LANE), 1)
    head_mask = [(lane // C_DK) == hh for hh in range(pair)]

    cums, tots = [], []
    for d in range(2):
        la = _log_decay(refs[d][3][0], wdec_ref[d], bdec_ref[d])
        cum = _cum_decay(la, keep[d].astype(BF16))
        cums.append(cum)
        tots.append(cum[n - 1:n] if d == 0 else cum[0:1])
    safe = jnp.maximum(jnp.max(-tots[0]), jnp.max(-tots[1])) <= GLA_SAFE_DECAY

    @pl.when(safe)
    def _():
        for d in range(2):
            q_ref, k_ref, v_ref, _ = refs[d]
            cum, tot = cums[d], tots[d]
            half = 0.5 * tot
            a = cum - half
            eh = jnp.exp(half)
            qr = q_ref[0].astype(F32) * qscale * jnp.exp(a)
            kr = k_ref[0].astype(F32) * jnp.exp(-a)
            kr_b, kd_b = kr.astype(BF16), (kr * eh).astype(BF16)
            qe = qr * eh
            v, st = v_ref[0], st_ref[d]
            o_parts, u_parts = [], []
            for p in range(npair):
                ls = slice(p * LANE, (p + 1) * LANE)
                st2 = st[:, ls].astype(BF16)
                upd = []
                for hh in range(pair):
                    h = p * pair + hh
                    vh = v[:, h * C_DV:(h + 1) * C_DV]
                    qr_h = jnp.where(head_mask[hh], qr[:, ls], 0.0).astype(BF16)
                    qe_h = jnp.where(head_mask[hh], qe[:, ls], 0.0).astype(BF16)
                    pm = jnp.where(keep[d], _dot_nt(qr_h, kr_b[:, ls]), 0.0).astype(BF16)
                    o_parts.append(_dot(pm, vh) + _dot_nt(qe_h, st2))
                    upd.append(_dot_tn(vh, kd_b[:, ls]))
                u_parts.append(jnp.where(head_mask[0], upd[0], upd[1]))
            outs[d][0] = jnp.concatenate(o_parts, axis=1)
            st_ref[d] = st * jnp.exp(tot) + jnp.concatenate(u_parts, axis=1)

    @pl.when(jnp.logical_not(safe))
    def _():
        c = C_CHUNK
        tile = 16
        nsub = n // c
        r64 = lax.broadcasted_iota(jnp.int32, (c, c), 0)
        c64 = lax.broadcasted_iota(jnp.int32, (c, c), 1)
        tri64 = ((c64 <= r64).astype(BF16), (c64 >= r64).astype(BF16))
        for d in range(2):
            q_ref, k_ref, v_ref, lr_ref = refs[d]
            e_ref[...] = jnp.zeros_like(e_ref)

            def sub(i, carry, d=d, q_ref=q_ref, k_ref=k_ref, v_ref=v_ref, lr_ref=lr_ref):
                rows = pl.ds(pl.multiple_of((i if d == 0 else nsub - 1 - i) * c, c), c)
                q = q_ref[0, rows, :].astype(F32) * qscale
                k = k_ref[0, rows, :].astype(F32)
                v = v_ref[0, rows, :]
                cum = _cum_decay(_log_decay(lr_ref[0, rows, :], wdec_ref[d], bdec_ref[d]), tri64[d])
                tot = cum[c - 1:c] if d == 0 else cum[0:1]
                qe = (q * jnp.exp(cum)).astype(BF16)
                kd = (k * jnp.exp(tot - cum)).astype(BF16)
                for p in range(npair):
                    ls = slice(p * LANE, (p + 1) * LANE)
                    q2, k2, c2 = q[:, ls], k[:, ls], cum[:, ls]
                    for s in range(c):
                        bt = (s // tile) * tile
                        t0, t1 = (bt, c) if d == 0 else (0, bt + tile)
                        e = q2[t0:t1] * k2[s:s + 1] * jnp.exp(c2[t0:t1] - c2[s:s + 1])
                        ridx = lax.broadcasted_iota(jnp.int32, (t1 - t0, LANE), 0) + t0
                        causal = (ridx >= s) if d == 0 else (ridx <= s)
                        e_ref[p * c + t0:p * c + t1, s * LANE:(s + 1) * LANE] = (
                            jnp.where(causal, e, 0.0).astype(BF16))
                pm = _dot(e_ref[...], sel_ref[...])
                st = st_ref[d]
                o_parts, u_parts = [], []
                for h in range(C_HEADS):
                    p, hh = divmod(h, pair)
                    ph = pm[p * c:(p + 1) * c, hh * c:(hh + 1) * c].astype(BF16)
                    vh = v[:, h * C_DV:(h + 1) * C_DV]
                    hs = slice(h * C_DK, (h + 1) * C_DK)
                    o_parts.append(_dot(ph, vh) + _dot_nt(qe[:, hs], st[:, hs].astype(BF16)))
                    u_parts.append(_dot_tn(vh, kd[:, hs]))
                outs[d][0, rows, :] = jnp.concatenate(o_parts, axis=1)
                st_ref[d] = st * jnp.exp(tot) + jnp.concatenate(u_parts, axis=1)
                return carry

            lax.fori_loop(0, nsub, sub, 0)


def _gla_sel():
    pair = LANE // C_DK
    sel = np.zeros((C_CHUNK, pair, C_DK, pair, C_CHUNK), np.float32)
    for s in range(C_CHUNK):
        for h in range(pair):
            sel[s, h, :, h, s] = 1.0
    return jnp.asarray(sel.reshape(C_CHUNK * LANE, pair * C_CHUNK), BF16)


def _gla(p, wdec, bdec, sel, l, *, t, cn):
    b, s, _ = p.shape
    n = GLA_BLOCK
    nbl, nbc = t // n, cn // n

    def fwd(j):
        return jnp.where(j < nbc, nbl + j, j - nbc)

    def bwd(j):
        return nbl + nbc - 1 - j

    def specs(order):
        return [pl.BlockSpec((1, n, sg[1]), lambda bb, j, sg=sg: (bb, order(j), sg[0] // sg[1]))
                for sg in (SEG_CQ, SEG_CK, SEG_CV, SEG_CLR)]

    width = C_HEADS * C_DV
    return pl.pallas_call(
        _gla_kernel,
        grid=(b, nbl + nbc),
        in_specs=specs(fwd) + specs(bwd) + [
            _layer_resident(wdec, l), _layer_resident(bdec, l), _resident(sel.shape),
        ],
        out_specs=[pl.BlockSpec((1, n, width), lambda bb, j: (bb, fwd(j), 0)),
                   pl.BlockSpec((1, n, width), lambda bb, j: (bb, bwd(j), 0))],
        out_shape=[jax.ShapeDtypeStruct((b, s, width), F32)] * 2,
        scratch_shapes=[pltpu.VMEM((2, C_DV, C_HEADS * C_DK), F32),
                        pltpu.VMEM(((C_HEADS * C_DK // LANE) * C_CHUNK, C_CHUNK * LANE), BF16)],
        compiler_params=_cparams(("parallel", "arbitrary")),
        name="gla",
    )(*([p] * 8), wdec, bdec, sel)


def _merge_mlp_kernel(*refs, d, nsrc, nlat_blocks, final):
    x_refs, ya_refs, yd_refs = (refs[i * nsrc:(i + 1) * nsrc] for i in range(3))
    (mod_ref, yb_ref, of_ref, ob_ref, r_ref, g_ref, gnw_ref, wba_ref, wbb_ref, wbc_ref, wbd_ref, wo_ref,
     nw2_ref, w1_ref, w2_ref, fnw_ref, o_ref) = refs[3 * nsrc:]
    mod = mod_ref[0]
    o = of_ref[0] + ob_ref[0]
    r = r_ref[0].astype(F32)
    yg = jnp.concatenate([_rms(o[:, h * C_DV:(h + 1) * C_DV], gnw_ref[...]) for h in range(C_HEADS)], axis=1)
    yg = (yg * (r * jax.nn.sigmoid(r))).astype(BF16)
    branches = ((_token_rows(ya_refs, nlat_blocks), wba_ref), (yb_ref[0], wbb_ref), (yg, wbc_ref),
                (_token_rows(yd_refs, nlat_blocks), wbd_ref))
    mixed = []
    for c in range(d // MERGE_TN):
        cs = slice(c * MERGE_TN, (c + 1) * MERGE_TN)
        acc_z = acc_t = None
        for n, (y, w_ref) in enumerate(branches):
            z = _dot(y, w_ref[:, cs])
            tz = jnp.tanh(g_ref[0, :, n * d + c * MERGE_TN:n * d + (c + 1) * MERGE_TN].astype(F32)) * z
            acc_z = z if acc_z is None else acc_z + z
            acc_t = tz if acc_t is None else acc_t + tz
        mixed.append((0.5 * (acc_z + acc_t)).astype(BF16))
    x1 = _token_rows(x_refs, nlat_blocks) + mod[:, 2 * d:3 * d] * _dot(jnp.concatenate(mixed, axis=1), wo_ref[...])
    h = _norm_modulate(x1, nw2_ref[...], [mod], 3, d)
    a = jnp.maximum(_dot(h.astype(BF16), w1_ref[...]), 0.0)
    y = x1 + mod[:, 5 * d:6 * d] * _dot((a * a).astype(BF16), w2_ref[...])
    o_ref[0] = _rms(y, fnw_ref[...]) if final else y


def _merge_mlp(xs, modsel, ya, yb, o_f, o_b, p, yd, gnw, w_branch, wb_b, wo, nw2, w1, w2, fnw, l, *,
               tm, n_blocks, nlat_blocks, final):
    assert isinstance(xs, tuple) == isinstance(ya, tuple) == isinstance(yd, tuple)
    streams = [s if isinstance(s, tuple) else (s,) for s in (xs, ya, yd)]
    srcs = [a for s in streams for a in s]
    b, d = srcs[0].shape[0], srcs[0].shape[2]

    def tok(w, col=0):
        return pl.BlockSpec((1, tm, w), lambda bb, i: (bb, i, col))

    w_specs = [_layer_resident(w_branch, l, 0), _layer_resident(wb_b, l),
               _layer_resident(w_branch, l, 2), _layer_resident(w_branch, l, 3), _layer_resident(wo, l),
               _layer_resident(nw2, l), _layer_resident(w1, l), _layer_resident(w2, l), _resident(fnw.shape)]
    return pl.pallas_call(
        functools.partial(_merge_mlp_kernel, d=d, nsrc=len(streams[0]), nlat_blocks=nlat_blocks, final=final),
        grid=(b, n_blocks),
        in_specs=[sp for s in (xs, ya, yd) for sp in _token_specs(s, tm, nlat_blocks)] + [
            _mod_spec(modsel, l, nlat_blocks),
            tok(yb.shape[2]), tok(o_f.shape[2]), tok(o_b.shape[2]),
            tok(SEG_CR[1], SEG_CR[0] // SEG_CR[1]),
            tok(SEG_GATE[1], 0),
            _layer_resident(gnw, l), *w_specs,
        ],
        out_specs=tok(d),
        out_shape=jax.ShapeDtypeStruct((b, n_blocks * tm, d), F32),
        compiler_params=_cparams(("parallel", "parallel")),
        name="merge_mlp",
    )(*srcs, modsel, yb, o_f, o_b, p, p, gnw, w_branch, wb_b, w_branch, w_branch, wo, nw2, w1, w2, fnw)


def _pad_cols(w, n):
    return jnp.concatenate([w, jnp.zeros(w.shape[:-1] + (n,), w.dtype)], axis=-1)


def _layout_w_in(w):
    a0 = 0
    b0 = a0 + A_Q_RANK + A_KV_RANK + A_ROPE
    c0 = b0 + (B_HEADS + 2 * B_KV_HEADS) * B_HD
    d0 = c0 + 2 * C_HEADS * C_DK + 2 * C_HEADS * C_DV + 2 * C_DECAY_RANK
    g0 = d0 + (D_HEADS + 2 * D_KV_HEADS) * D_HD

    def cols(lo, n):
        return w[..., lo:lo + n]

    cq = c0
    ck = cq + C_HEADS * C_DK
    cv = ck + C_HEADS * C_DK
    cr = cv + C_HEADS * C_DV
    clr = cr + C_HEADS * C_DV
    parts = [
        cols(g0, SEG_GATE[1]),
        cols(d0, SEG_DQ[1] + SEG_DK[1] + SEG_DV[1]),
        *[cols(b0 + h * B_HD, B_HD) for h in _swa_head_order()],
        cols(b0 + SEG_BQ[1], SEG_BK[1] + SEG_BV[1]),
        cols(a0, A_Q_RANK + A_KV_RANK),
        _pad_cols(cols(a0 + A_Q_RANK + A_KV_RANK, A_ROPE), SEG_AKR[1] - A_ROPE),
        cols(cq, SEG_CQ[1]), cols(cv, SEG_CV[1]), cols(cr, SEG_CR[1]), cols(ck, SEG_CK[1]),
        _pad_cols(cols(clr, 2 * C_DECAY_RANK), P_COLS - SEG_CLR[0] - 2 * C_DECAY_RANK),
    ]
    return jnp.concatenate(parts, axis=-1).astype(BF16)


def _layout_mla(w_uq, w_ukv):
    zq = jnp.zeros((A_Q_RANK, A_HEAD_PAD - A_NOPE - A_ROPE), w_uq.dtype)
    hq = A_NOPE + A_ROPE
    wq = jnp.concatenate([jnp.concatenate([w_uq[:, h * hq:(h + 1) * hq], zq], axis=1)
                          for h in range(A_HEADS)], axis=1)
    hk = A_NOPE + A_V
    zk = jnp.zeros((A_KV_RANK, A_HEAD_PAD - A_NOPE), w_ukv.dtype)
    wk_top = jnp.concatenate([jnp.concatenate([w_ukv[:, h * hk:h * hk + A_NOPE], zk], axis=1)
                              for h in range(A_HEADS)], axis=1)
    place = np.zeros((SEG_AKR[1], A_HEADS * A_HEAD_PAD), np.float32)
    for h in range(A_HEADS):
        place[np.arange(A_ROPE), h * A_HEAD_PAD + A_NOPE + np.arange(A_ROPE)] = 1.0
    wk = jnp.concatenate([wk_top, jnp.asarray(place)], axis=0)
    wv = jnp.concatenate([w_ukv[:, h * hk + A_NOPE:(h + 1) * hk] for h in range(A_HEADS)], axis=1)
    lanes = np.arange(A_HEADS * A_HEAD_PAD)
    rot = ((lanes % A_HEAD_PAD) >= A_NOPE) & ((lanes % A_HEAD_PAD) < A_NOPE + A_ROPE)
    idx = _partner_lanes(lanes.size, A_ROPE // 4)

    def with_partner(w):
        return jnp.concatenate([w, jnp.where(jnp.asarray(rot)[None, :], w[:, idx], 0.0)], axis=1)

    return with_partner(wq).astype(BF16), with_partner(wk).astype(BF16), wv.astype(BF16)


def _partner_lanes(width, blk):
    j = np.arange(width)
    return np.where((j % (2 * blk)) < blk, j + blk, j - blk)


def _perm_matrix(width, blk):
    m = np.zeros((width, width), np.float32)
    m[_partner_lanes(width, blk), np.arange(width)] = 1.0
    return m


def _swa_head_order():
    grp = B_HEADS // B_KV_HEADS
    return [h for g in range(grp) for h in (g, grp + g)]


def _layout_w_branch_b(w):
    return jnp.concatenate([w[h * B_HD:(h + 1) * B_HD] for h in _swa_head_order()], axis=0)


def _layout_decay(w_decay):
    z = jnp.zeros((C_DECAY_RANK, w_decay.shape[-1]), w_decay.dtype)
    tail = jnp.zeros((SEG_CLR[1] - 2 * C_DECAY_RANK, w_decay.shape[-1]), w_decay.dtype)
    return jnp.stack([jnp.concatenate([w_decay[0], z, tail], axis=0),
                      jnp.concatenate([z, w_decay[1], tail], axis=0)]).astype(BF16)


def kernel(x, c, ctx, c_ctx, ada_w, ada_b, norm1_w, norm2_w, w_in, mla_q_norm_w, mla_w_uq, mla_kv_norm_w, mla_w_ukv, swa_sink, gla_w_decay, gla_b_decay, gla_norm_w, gqa_q_norm_w, gqa_k_norm_w, w_branch, w_o, mlp_w1, mlp_w2, final_norm_w):
    b, t, d = x.shape
    cn = ctx.shape[1]
    depth = ada_w.shape[0]
    tm = TOKEN_TILE
    assert t % tm == 0 and cn % tm == 0 and t % GRID_W == 0 and t >= SWA_BLOCK + 2 * WINDOW
    assert t % ATTN_TQ == 0 and t % cn == 0 and t % GLA_BLOCK == 0 and cn % GLA_BLOCK == 0
    nlat, nall = t // tm, (t + cn) // tm

    rows = 16
    cs = jnp.concatenate([c, c_ctx[None], jnp.zeros((rows - b - 1, d), F32)], axis=0)
    mod = _ada(cs, ada_w, ada_b)
    modsel = jnp.stack([mod[:, :b], jnp.broadcast_to(mod[:, b:b + 1], (depth, b, 6 * d))], axis=2)
    modsel = modsel.reshape(depth, b * 2, 1, 6 * d)

    tabs = (_rope_tables(t, cn, A_HEAD_PAD, A_NOPE, A_ROPE)
            + tuple(jnp.tile(tb, (1, LANE // B_HD)) for tb in _rope_tables(t, cn, B_HD, 0, B_HD))
            + _rope_tables(t, cn, D_HD, 0, D_HD))
    sel = _gla_sel()
    perms = (jnp.asarray(_perm_matrix(B_HEADS * B_HD, B_HD // 4), BF16),
             jnp.asarray(_perm_matrix(B_KV_HEADS * B_HD, B_HD // 4), BF16),
             jnp.asarray(_perm_matrix(D_HEADS * D_HD, D_HD // 4), BF16),
             jnp.asarray(_perm_matrix(D_KV_HEADS * D_HD, D_HD // 4), BF16))
    d_partner = _partner_lanes(D_HD, D_HD // 4)

    w_p = _layout_w_in(w_in)
    wbr, wo = w_branch.astype(BF16), w_o.astype(BF16)
    wb_b = jax.vmap(_layout_w_branch_b)(w_branch[:, 1]).astype(BF16)
    w1, w2 = mlp_w1.astype(BF16), mlp_w2.astype(BF16)
    wq, wk, wv = jax.vmap(_layout_mla)(mla_w_uq, mla_w_ukv)
    wdec, bdec = jax.vmap(_layout_decay)(gla_w_decay), gla_b_decay[:, :, None, :]

    def rows(w):
        return w[:, None, :]

    prep_wts = ([(rows(mla_q_norm_w), True), (rows(mla_kv_norm_w), True), (wq, True), (wk, True), (wv, True)]
                + [(pm, False) for pm in perms]
                + [(rows(gqa_q_norm_w), True), (rows(gqa_q_norm_w[:, d_partner]), True),
                   (rows(gqa_k_norm_w), True), (rows(gqa_k_norm_w[:, d_partner]), True)])
    n1, n2, gnw = rows(norm1_w), rows(norm2_w), rows(gla_norm_w)

    xs = (x, ctx)
    for l in range(depth):
        last = l == depth - 1
        p, qa, ka, va, qb, kb, qd, kd = _inproj(xs, modsel, n1, w_p, tabs, prep_wts, l, nlat)
        ya = _attention(qa, ka, va, heads=A_HEADS, kv_group=1, dk=A_HEAD_PAD, dv=A_V, vcol0=0,
                        t=t, cn=cn, with_ctx=not last, name="attn_a")
        yd = _attention(qd, kd, p, heads=D_HEADS, kv_group=D_HEADS // D_KV_HEADS, dk=D_HD, dv=D_HD,
                        vcol0=SEG_DV[0] // SEG_DV[1], t=t, cn=cn, with_ctx=not last, name="attn_d")
        yb = _swa(swa_sink, qb, kb, p, l, t=t, cn=cn,
                  n_blocks=(t if last else t + cn) // SWA_BLOCK)
        o_f, o_b = _gla(p, wdec, bdec, sel, l, t=t, cn=cn)
        if last and isinstance(xs, tuple):
            xs = xs[0]
        tile = dict(tm=LAST_TILE, n_blocks=t // LAST_TILE, nlat_blocks=t // LAST_TILE) if last else dict(
            tm=tm, n_blocks=nall, nlat_blocks=nlat)
        xs = _merge_mlp(xs, modsel, ya, yb, o_f, o_b, p, yd, gnw, wbr, wb_b, wo, n2, w1, w2,
                        final_norm_w[None], l, final=last, **tile)
    return xs
```

```python
import functools

import numpy as np
import jax
import jax.numpy as jnp
from jax import lax
from jax.experimental import pallas as pl
from jax.experimental.pallas import tpu as pltpu

F32 = jnp.float32
BF16 = jnp.bfloat16

GRID_W = 64
EPS = 1e-6
ROPE_BASE = 10000.0
NEG_INF = -1e30

A_HEADS, A_NOPE, A_ROPE, A_V, A_Q_RANK, A_KV_RANK = 4, 128, 64, 128, 256, 128
A_HEAD_PAD = 256
B_HEADS, B_KV_HEADS, B_HD, WINDOW = 8, 2, 64, 128
SWA_BLOCK = 256
SWA_ROW_TILE = 32
C_HEADS, C_DK, C_DV, C_DECAY_RANK, C_DECAY_TEMP, C_CHUNK = 4, 64, 128, 16, 16.0, 64
D_HEADS, D_KV_HEADS, D_HD = 4, 2, 128
N_BRANCH, BRANCH_W = 4, 512

LANE = 128
TOKEN_TILE = 256
LAST_TILE = 512
ATTN_TQ = 1024
ATTN_CHAIN_ROWS = 512
ATTN_HEADS_PER_STEP = 4
INPROJ_TN = 2048
MERGE_TN = 256
VMEM_LIMIT = 56 * 1024 * 1024

P_COLS = 8192
P_OUT_COLS = 6144
SEG_GATE = (0, 4096)
SEG_CV, SEG_CR, SEG_CQ, SEG_CK = (4096, 512), (4608, 512), (5120, 256), (5376, 256)
SEG_DV, SEG_BV, SEG_CLR = (5632, 256), (5888, 128), (6016, 128)
SEG_DQ, SEG_DK = (6144, 512), (6656, 256)
SEG_BQ, SEG_BK = (6912, 512), (7424, 128)
SEG_ACQ, SEG_ACKV, SEG_AKR = (7552, 256), (7808, 128), (7936, 128)


def _cparams(sem):
    return pltpu.CompilerParams(dimension_semantics=sem, vmem_limit_bytes=VMEM_LIMIT)


def _rms(xf, w):
    return xf * lax.rsqrt(jnp.mean(xf * xf, axis=-1, keepdims=True) + EPS) * w


def _dot(a, b):
    return jnp.dot(a, b, preferred_element_type=F32)


def _dot_nt(a, b):
    return lax.dot_general(a, b, (((1,), (1,)), ((), ())), preferred_element_type=F32)


def _dot_tn(a, b):
    return lax.dot_general(a, b, (((0,), (0,)), ((), ())), preferred_element_type=F32)


def _rope_tables(t, cn, width, offset, rot):
    half = rot // 2
    q = half // 2
    inv = ROPE_BASE ** (-np.arange(q, dtype=np.float64) / q)
    pos = np.arange(t)
    cos = np.ones((t + cn, width), np.float64)
    sin = np.zeros((t + cn, width), np.float64)
    for part, p in enumerate((pos // GRID_W, pos % GRID_W)):
        ang = p[:, None] * inv[None, :]
        lo = offset + part * half
        cos[:t, lo:lo + half] = np.concatenate([np.cos(ang), np.cos(ang)], axis=1)
        sin[:t, lo:lo + half] = np.concatenate([-np.sin(ang), np.sin(ang)], axis=1)
    return jnp.asarray(cos, F32), jnp.asarray(sin, F32)


def _ada_kernel(cs_ref, w_ref, b_ref, o_ref):
    cs = cs_ref[...]
    s = cs * jax.nn.sigmoid(cs)
    o_ref[0] = _dot(s.astype(BF16), w_ref[0].astype(BF16)) + b_ref[0]


def _ada(cs, ada_w, ada_b):
    nl, d, n = ada_w.shape
    tn = 1536
    rows = cs.shape[0]
    return pl.pallas_call(
        _ada_kernel,
        grid=(nl, n // tn),
        in_specs=[
            pl.BlockSpec((rows, d), lambda l, j: (0, 0)),
            pl.BlockSpec((1, d, tn), lambda l, j: (l, 0, j)),
            pl.BlockSpec((1, 1, tn), lambda l, j: (l, 0, j)),
        ],
        out_specs=pl.BlockSpec((1, rows, tn), lambda l, j: (l, 0, j)),
        out_shape=jax.ShapeDtypeStruct((nl, rows, n), F32),
        compiler_params=_cparams(("parallel", "parallel")),
        name="ada",
    )(cs, ada_w, ada_b.reshape(nl, 1, n))


def _token_specs(xs, tm, nlat_blocks):
    if not isinstance(xs, tuple):
        return [pl.BlockSpec((1, tm, xs.shape[2]), lambda bb, i: (bb, i, 0))]
    x, ctx = xs
    return [pl.BlockSpec((1, tm, x.shape[2]), lambda bb, i: (bb, jnp.minimum(i, nlat_blocks - 1), 0)),
            pl.BlockSpec((1, tm, ctx.shape[2]), lambda bb, i: (bb, jnp.maximum(i - nlat_blocks, 0), 0))]


def _token_rows(x_refs, nlat_blocks):
    if len(x_refs) == 1:
        return x_refs[0][0]
    return jnp.where(pl.program_id(1) >= nlat_blocks, x_refs[1][0], x_refs[0][0])


def _norm_modulate(x, nw, mods, k, d):
    rows = x.shape[0] // len(mods)
    parts = [_rms(x[j * rows:(j + 1) * rows], nw) * (1.0 + m[:, (k + 1) * d:(k + 2) * d]) + m[:, k * d:(k + 1) * d]
             for j, m in enumerate(mods)]
    return parts[0] if len(parts) == 1 else jnp.concatenate(parts, axis=0)


N_PREP_TABLES, N_PREP_WEIGHTS, N_PREP_OUTS = 6, 13, 7


def _inproj_kernel(*refs, d, nsrc, nlat_blocks):
    mod_ref, nw_ref, w_ref = refs[nsrc:nsrc + 3]
    prep_refs = refs[nsrc + 3:nsrc + 3 + N_PREP_TABLES + N_PREP_WEIGHTS]
    o_ref = refs[nsrc + 3 + N_PREP_TABLES + N_PREP_WEIGHTS]
    prep_outs = refs[-N_PREP_OUTS:]
    x = _token_rows(refs[:nsrc], nlat_blocks)
    h = _norm_modulate(x, nw_ref[...], [mod_ref[0]], 0, d).astype(BF16)
    prep_in = _dot(h, w_ref[:, P_OUT_COLS:]).astype(BF16)

    def seg(sg):
        return prep_in[:, sg[0] - P_OUT_COLS:sg[0] - P_OUT_COLS + sg[1]]

    _prep_body(*[seg(sg) for sg in (SEG_ACQ, SEG_ACKV, SEG_AKR, SEG_BQ, SEG_BK, SEG_DQ, SEG_DK)],
               *prep_refs, *prep_outs)
    for j in range(P_OUT_COLS // INPROJ_TN):
        cs = slice(j * INPROJ_TN, (j + 1) * INPROJ_TN)
        res = _dot(h, w_ref[:, cs])
        if cs.stop <= SEG_GATE[0] + SEG_GATE[1]:
            res = res * 0.5
        o_ref[0, :, cs] = res.astype(BF16)


def _resident(shape):
    return pl.BlockSpec(shape, lambda *_: (0,) * len(shape), pipeline_mode=pl.Buffered(1))


def _layer_resident(arr, *lead):
    rest = arr.shape[len(lead):]
    return pl.BlockSpec((None,) * len(lead) + rest, lambda *_: tuple(lead) + (0,) * len(rest),
                        pipeline_mode=pl.Buffered(1))


def _mod_spec(modsel, l, nlat_blocks):
    return pl.BlockSpec((None, 1) + modsel.shape[2:],
                        lambda bb, i: (l, bb * 2 + jnp.where(i >= nlat_blocks, 1, 0), 0, 0))


def _inproj(xs, modsel, nw, w_p, tabs, wts, l, nlat_blocks):
    srcs = xs if isinstance(xs, tuple) else (xs,)
    b, d = srcs[0].shape[0], srcs[0].shape[2]
    s = sum(a.shape[1] for a in srcs)
    tm = TOKEN_TILE
    assert len(tabs) == N_PREP_TABLES and len(wts) == N_PREP_WEIGHTS
    tab_specs = [pl.BlockSpec((tm, tb.shape[1]), lambda bb, i: (i, 0)) for tb in tabs]
    wt_specs = [_layer_resident(w, l) if stacked else _resident(w.shape) for w, stacked in wts]
    widths = [P_OUT_COLS, A_HEADS * A_HEAD_PAD, A_HEADS * A_HEAD_PAD, A_HEADS * A_V,
              B_HEADS * B_HD, B_KV_HEADS * B_HD, D_HEADS * D_HD, D_KV_HEADS * D_HD]
    return pl.pallas_call(
        functools.partial(_inproj_kernel, d=d, nsrc=len(srcs), nlat_blocks=nlat_blocks),
        grid=(b, s // tm),
        in_specs=(_token_specs(xs, tm, nlat_blocks)
                  + [_mod_spec(modsel, l, nlat_blocks), _layer_resident(nw, l), _layer_resident(w_p, l)]
                  + tab_specs + wt_specs),
        out_specs=[pl.BlockSpec((1, tm, w), lambda bb, i: (bb, i, 0)) for w in widths],
        out_shape=[jax.ShapeDtypeStruct((b, s, w), BF16) for w in widths],
        compiler_params=_cparams(("parallel", "parallel")),
        name="inproj",
    )(*srcs, modsel, nw, w_p, *tabs, *[w for w, _ in wts])


def _prep_body(acq, ackv, akr, bq, bk, dq, dk,
               cosa_ref, sina_ref, cosb_ref, sinb_ref, cosd_ref, sind_ref,
               aqn_ref, akvn_ref, wq_ref, wk_ref, wv_ref, permbq_ref, permbk_ref,
               permdq_ref, permdk_ref, dqn_ref, dqnp_ref, dkn_ref, dknp_ref,
               qa_ref, ka_ref, va_ref, qb_ref, kb_ref, qd_ref, kd_ref):
    na = A_HEADS * A_HEAD_PAD
    cos_r, sin_r = cosa_ref[:, A_NOPE:], sina_ref[:, A_NOPE:]

    def rope_heads(x, w_ref, o_ref, scale):
        for h in range(A_HEADS):
            c0 = h * A_HEAD_PAD
            main = _dot(x, w_ref[:, c0:c0 + A_HEAD_PAD])
            part = _dot(x, w_ref[:, na + c0 + A_NOPE:na + c0 + A_HEAD_PAD])
            out = jnp.concatenate([main[:, :A_NOPE], main[:, A_NOPE:] * cos_r + part * sin_r], axis=1)
            o_ref[0, :, c0:c0 + A_HEAD_PAD] = (out * scale).astype(BF16)

    cq = _rms(acq.astype(F32), aqn_ref[...]).astype(BF16)
    rope_heads(cq, wq_ref, qa_ref, (A_NOPE + A_ROPE) ** -0.5)
    ckv = _rms(ackv.astype(F32), akvn_ref[...]).astype(BF16)
    rope_heads(jnp.concatenate([ckv, akr], axis=1), wk_ref, ka_ref, 1.0)
    va_ref[0] = _dot(ckv, wv_ref[...]).astype(BF16)

    nrep = B_HEADS * B_HD // LANE
    cosb, sinb = cosb_ref[...], sinb_ref[...]
    qb = (bq.astype(F32) * jnp.concatenate([cosb] * nrep, axis=1)
          + _dot(bq, permbq_ref[...]) * jnp.concatenate([sinb] * nrep, axis=1))
    qb_ref[0] = (qb * (B_HD ** -0.5)).astype(BF16)
    kb_ref[0] =(bk.astype(F32) * cosb + _dot(bk, permbk_ref[...]) * sinb).astype(BF16)

    cosd, sind = cosd_ref[...], sind_ref[...]

    def norm_rope(x, perm_ref, nw_ref, nwp_ref, heads):
        xp = _dot(x, perm_ref[...])
        xf = x.astype(F32)
        wc, ws = nw_ref[...] * cosd, nwp_ref[...] * sind
        parts = []
        for h in range(heads):
            hs = slice(h * D_HD, (h + 1) * D_HD)
            xh = xf[:, hs]
            inv = lax.rsqrt(jnp.mean(xh * xh, axis=-1, keepdims=True) + EPS)
            parts.append(inv * (xh * wc + xp[:, hs] * ws))
        return jnp.concatenate(parts, axis=1)

    qd_ref[0] = (norm_rope(dq, permdq_ref, dqn_ref, dqnp_ref, D_HEADS) * (D_HD ** -0.5)).astype(BF16)
    kd_ref[0] = norm_rope(dk, permdk_ref, dkn_ref, dknp_ref, D_KV_HEADS).astype(BF16)


def _attn_kernel(q_ref, k_ref, v_ref, o_ref, *, dk, dv, kv_group):
    rows = min(ATTN_CHAIN_ROWS, q_ref.shape[1])
    for c in range(ATTN_HEADS_PER_STEP):
        kc = c // kv_group
        s_all = _dot_nt(q_ref[0, :, c * dk:(c + 1) * dk], k_ref[0, :, kc * dk:(kc + 1) * dk])
        for r in range(q_ref.shape[1] // rows):
            rs = slice(r * rows, (r + 1) * rows)
            s = s_all[rs]
            m = jnp.max(s, axis=-1, keepdims=True)
            p = jnp.exp(s - m)
            l = jnp.sum(p, axis=-1, keepdims=True)
            o = _dot(p.astype(BF16), v_ref[0, :, kc * dv:(kc + 1) * dv])
            o_ref[0, rs, c * dv:(c + 1) * dv] = (o / l).astype(o_ref.dtype)


def _attention(q, k, v, *, heads, kv_group, dk, dv, vcol0, t, cn, with_ctx, name):
    b, s, _ = q.shape
    hps = ATTN_HEADS_PER_STEP
    nkv = hps // kv_group
    kern = functools.partial(_attn_kernel, dk=dk, dv=dv, kv_group=kv_group)
    sem = _cparams(("parallel", "parallel", "parallel"))

    def call(tq, nq, row0, krows, krow0, nm):
        return pl.pallas_call(
            kern,
            grid=(b, heads // hps, nq // tq),
            in_specs=[
                pl.BlockSpec((1, tq, hps * dk), lambda bb, h, i: (bb, row0 + i, h)),
                pl.BlockSpec((1, krows, nkv * dk), lambda bb, h, i: (bb, krow0, h)),
                pl.BlockSpec((1, krows, nkv * dv), lambda bb, h, i: (bb, krow0, vcol0 + h)),
            ],
            out_specs=pl.BlockSpec((1, tq, hps * dv), lambda bb, h, i: (bb, i, h)),
            out_shape=jax.ShapeDtypeStruct((b, nq, heads * dv), BF16),
            compiler_params=sem,
            name=nm,
        )(q, k, v)

    y = call(ATTN_TQ, t, 0, s, 0, name)
    if with_ctx:
        return y, call(cn, cn, t // cn, cn, t // cn, name + "_ctx")
    return y


def _swa_bias():
    band = SWA_BLOCK + 2 * WINDOW
    r = np.arange(SWA_BLOCK)[:, None]
    j = np.arange(band)[None, :]
    out = []
    for off in (0, WINDOW, band - SWA_BLOCK):
        out.append(np.where(np.abs(off + r - j) <= WINDOW, 0.0, NEG_INF))
    return np.stack(out).astype(np.float32)


def _swa_kernel(sink_ref, q_ref, k_ref, v_ref, bias_ref, o_ref, s_ref, p_ref, e_ref, *, t, cn, layer):
    n = pl.program_id(1)
    blk = SWA_BLOCK
    nlat = t // blk
    grp = B_HEADS // B_KV_HEADS
    band = blk + 2 * WINDOW
    rt = SWA_ROW_TILE

    def run(is_ctx):
        q = q_ref[0]
        k, v = k_ref[0, t:t + cn, :], v_ref[0, t:t + cn, :]
        if not is_ctx:
            start = pl.multiple_of(jnp.clip(n * blk - WINDOW, 0, t - band), WINDOW)
            k = jnp.concatenate([k, k_ref[0, pl.ds(start, band), :]], axis=0)
            v = jnp.concatenate([v, v_ref[0, pl.ds(start, band), :]], axis=0)
        nk = k.shape[0]
        halves = []
        qlane = lax.broadcasted_iota(jnp.int32, (blk, LANE), 1)
        for hk in range(B_KV_HEADS):
            mine = ((qlane // B_HD) == hk).astype(BF16)
            q4 = jnp.concatenate([q[:, gi * LANE:(gi + 1) * LANE] * mine for gi in range(grp)], axis=0)
            s_ref[hk, :, :nk] = _dot_nt(q4, k)

            for i in range(grp * blk // rt):
                rs = slice(i * rt, (i + 1) * rt)
                sink = sink_ref[layer, hk * grp + i // (blk // rt)]
                s = s_ref[hk, rs, :nk]
                if not is_ctx:
                    br = (i % (blk // rt)) * rt
                    s = s + bias_ref[0, br:br + rt, :]
                m = jnp.maximum(jnp.max(s, axis=-1, keepdims=True), sink)
                p = jnp.exp(s - m)
                l = jnp.sum(p, axis=-1, keepdims=True) + jnp.exp(sink - m)
                p_ref[hk, rs, :nk] = p.astype(BF16)
                e_ref[hk, rs, :] = jnp.broadcast_to(l, (rt, LANE))
            halves.append(_dot(p_ref[hk, :, :nk], v) / e_ref[hk])
        lane = lax.broadcasted_iota(jnp.int32, (blk, LANE), 1)
        outs = [jnp.where(lane < B_HD, halves[0][gi * blk:(gi + 1) * blk], halves[1][gi * blk:(gi + 1) * blk])
                for gi in range(grp)]
        o_ref[0] = jnp.concatenate(outs, axis=1).astype(o_ref.dtype)

    @pl.when(n < nlat)
    def _():
        run(False)

    @pl.when(n >= nlat)
    def _():
        run(True)


def _swa(sink, q, k, p, l, *, t, cn, n_blocks):
    b, s, _ = q.shape
    nlat = t // SWA_BLOCK
    rows = (B_HEADS // B_KV_HEADS) * SWA_BLOCK
    band_bias = _swa_bias()
    bias = jnp.asarray(np.concatenate([np.zeros(band_bias.shape[:2] + (cn,), np.float32), band_bias], axis=2))
    return pl.pallas_call(
        functools.partial(_swa_kernel, t=t, cn=cn, layer=l),
        grid=(b, n_blocks),
        in_specs=[
            pl.BlockSpec(memory_space=pltpu.SMEM),
            pl.BlockSpec((1, SWA_BLOCK, B_HEADS * B_HD), lambda bb, n: (bb, n, 0)),
            pl.BlockSpec((1, s, B_KV_HEADS * B_HD), lambda bb, n: (bb, 0, 0)),
            pl.BlockSpec((1, s, SEG_BV[1]), lambda bb, n: (bb, 0, SEG_BV[0] // SEG_BV[1])),
            pl.BlockSpec((1,) + bias.shape[1:],
                         lambda bb, n: (jnp.where(n == 0, 0, jnp.where(n >= nlat - 1, 2, 1)), 0, 0)),
        ],
        out_specs=pl.BlockSpec((1, SWA_BLOCK, B_HEADS * B_HD), lambda bb, n: (bb, n, 0)),
        out_shape=jax.ShapeDtypeStruct((b, n_blocks * SWA_BLOCK, B_HEADS * B_HD), BF16),
        scratch_shapes=[pltpu.VMEM((B_KV_HEADS, rows, bias.shape[2]), F32),
                        pltpu.VMEM((B_KV_HEADS, rows, bias.shape[2]), BF16),
                        pltpu.VMEM((B_KV_HEADS, rows, LANE), F32)],
        compiler_params=_cparams(("parallel", "parallel")),
        name="swa",
    )(sink, q, k, p, bias)


GLA_BLOCK = 256
GLA_SAFE_DECAY = 60.0


def _log_decay(lr, w, bias):
    z = _dot(lr, w) + bias
    return (jnp.minimum(z, 0.0) - jnp.log(1.0 + jnp.exp(-jnp.abs(z)))) / C_DECAY_TEMP


def _cum_decay(la, tri):
    hi = la.astype(BF16)
    lo = (la - hi.astype(F32)).astype(BF16)
    return _dot(tri, hi) + _dot(tri, lo)


def _gla_kernel(qf_ref, kf_ref, vf_ref, lrf_ref, qb_ref, kb_ref, vb_ref, lrb_ref,
                wdec_ref, bdec_ref, sel_ref, of_ref, ob_ref, st_ref, e_ref):
    n = GLA_BLOCK
    pair = LANE // C_DK
    npair = C_HEADS // pair
    qscale = C_DK ** -0.5
    refs = ((qf_ref, kf_ref, vf_ref, lrf_ref), (qb_ref, kb_ref, vb_ref, lrb_ref))
    outs = (of_ref, ob_ref)

    @pl.when(pl.program_id(1) == 0)
    def _():
        st_ref[...] = jnp.zeros_like(st_ref)

    r_i = lax.broadcasted_iota(jnp.int32, (n, n), 0)
    c_i = lax.broadcasted_iota(jnp.int32, (n, n), 1)
    keep = (c_i <= r_i, c_i >= r_i)
    lane = lax.broadcasted_iota(jnp.int32, (1, LANE), 1)
    head_mask = [(lane // C_DK) == hh for hh in range(pair)]

    cums, tots = [], []
    for d in range(2):
        la = _log_decay(refs[d][3][0], wdec_ref[d], bdec_ref[d])
        cum = _cum_decay(la, keep[d].astype(BF16))
        cums.append(cum)
        tots.append(cum[n - 1:n] if d == 0 else cum[0:1])
    safe = jnp.maximum(jnp.max(-tots[0]), jnp.max(-tots[1])) <= GLA_SAFE_DECAY

    @pl.when(safe)
    def _():
        for d in range(2):
            q_ref, k_ref, v_ref, _ = refs[d]
            cum, tot = cums[d], tots[d]
            half = 0.5 * tot
            a = cum - half
            eh = jnp.exp(half)
            qr = q_ref[0].astype(F32) * qscale * jnp.exp(a)
            kr = k_ref[0].astype(F32) * jnp.exp(-a)
            kr_b, kd_b = kr.astype(BF16), (kr * eh).astype(BF16)
            qe = qr * eh
            v, st = v_ref[0], st_ref[d]
            o_parts, u_parts = [], []
            for p in range(npair):
                ls = slice(p * LANE, (p + 1) * LANE)
                st2 = st[:, ls].astype(BF16)
                upd = []
                for hh in range(pair):
                    h = p * pair + hh
                    vh = v[:, h * C_DV:(h + 1) * C_DV]
                    qr_h = jnp.where(head_mask[hh], qr[:, ls], 0.0).astype(BF16)
                    qe_h = jnp.where(head_mask[hh], qe[:, ls], 0.0).astype(BF16)
                    pm = jnp.where(keep[d], _dot_nt(qr_h, kr_b[:, ls]), 0.0).astype(BF16)
                    o_parts.append(_dot(pm, vh) + _dot_nt(qe_h, st2))
                    upd.append(_dot_tn(vh, kd_b[:, ls]))
                u_parts.append(jnp.where(head_mask[0], upd[0], upd[1]))
            outs[d][0] = jnp.concatenate(o_parts, axis=1)
            st_ref[d] = st * jnp.exp(tot) + jnp.concatenate(u_parts, axis=1)

    @pl.when(jnp.logical_not(safe))
    def _():
        c = C_CHUNK
        tile = 16
        nsub = n // c
        r64 = lax.broadcasted_iota(jnp.int32, (c, c), 0)
        c64 = lax.broadcasted_iota(jnp.int32, (c, c), 1)
        tri64 = ((c64 <= r64).astype(BF16), (c64 >= r64).astype(BF16))
        for d in range(2):
            q_ref, k_ref, v_ref, lr_ref = refs[d]
            e_ref[...] = jnp.zeros_like(e_ref)

            def sub(i, carry, d=d, q_ref=q_ref, k_ref=k_ref, v_ref=v_ref, lr_ref=lr_ref):
                rows = pl.ds(pl.multiple_of((i if d == 0 else nsub - 1 - i) * c, c), c)
                q = q_ref[0, rows, :].astype(F32) * qscale
                k = k_ref[0, rows, :].astype(F32)
                v = v_ref[0, rows, :]
                cum = _cum_decay(_log_decay(lr_ref[0, rows, :], wdec_ref[d], bdec_ref[d]), tri64[d])
                tot = cum[c - 1:c] if d == 0 else cum[0:1]
                qe = (q * jnp.exp(cum)).astype(BF16)
                kd = (k * jnp.exp(tot - cum)).astype(BF16)
                for p in range(npair):
                    ls = slice(p * LANE, (p + 1) * LANE)
                    q2, k2, c2 = q[:, ls], k[:, ls], cum[:, ls]
                    for s in range(c):
                        bt = (s // tile) * tile
                        t0, t1 = (bt, c) if d == 0 else (0, bt + tile)
                        e = q2[t0:t1] * k2[s:s + 1] * jnp.exp(c2[t0:t1] - c2[s:s + 1])
                        ridx = lax.broadcasted_iota(jnp.int32, (t1 - t0, LANE), 0) + t0
                        causal = (ridx >= s) if d == 0 else (ridx <= s)
                        e_ref[p * c + t0:p * c + t1, s * LANE:(s + 1) * LANE] = (
                            jnp.where(causal, e, 0.0).astype(BF16))
                pm = _dot(e_ref[...], sel_ref[...])
                st = st_ref[d]
                o_parts, u_parts = [], []
                for h in range(C_HEADS):
                    p, hh = divmod(h, pair)
                    ph = pm[p * c:(p + 1) * c, hh * c:(hh + 1) * c].astype(BF16)
                    vh = v[:, h * C_DV:(h + 1) * C_DV]
                    hs = slice(h * C_DK, (h + 1) * C_DK)
                    o_parts.append(_dot(ph, vh) + _dot_nt(qe[:, hs], st[:, hs].astype(BF16)))
                    u_parts.append(_dot_tn(vh, kd[:, hs]))
                outs[d][0, rows, :] = jnp.concatenate(o_parts, axis=1)
                st_ref[d] = st * jnp.exp(tot) + jnp.concatenate(u_parts, axis=1)
                return carry

            lax.fori_loop(0, nsub, sub, 0)


def _gla_sel():
    pair = LANE // C_DK
    sel = np.zeros((C_CHUNK, pair, C_DK, pair, C_CHUNK), np.float32)
    for s in range(C_CHUNK):
        for h in range(pair):
            sel[s, h, :, h, s] = 1.0
    return jnp.asarray(sel.reshape(C_CHUNK * LANE, pair * C_CHUNK), BF16)


def _gla(p, wdec, bdec, sel, l, *, t, cn):
    b, s, _ = p.shape
    n = GLA_BLOCK
    nbl, nbc = t // n, cn // n

    def fwd(j):
        return jnp.where(j < nbc, nbl + j, j - nbc)

    def bwd(j):
        return nbl + nbc - 1 - j

    def specs(order):
        return [pl.BlockSpec((1, n, sg[1]), lambda bb, j, sg=sg: (bb, order(j), sg[0] // sg[1]))
                for sg in (SEG_CQ, SEG_CK, SEG_CV, SEG_CLR)]

    width = C_HEADS * C_DV
    return pl.pallas_call(
        _gla_kernel,
        grid=(b, nbl + nbc),
        in_specs=specs(fwd) + specs(bwd) + [
            _layer_resident(wdec, l), _layer_resident(bdec, l), _resident(sel.shape),
        ],
        out_specs=[pl.BlockSpec((1, n, width), lambda bb, j: (bb, fwd(j), 0)),
                   pl.BlockSpec((1, n, width), lambda bb, j: (bb, bwd(j), 0))],
        out_shape=[jax.ShapeDtypeStruct((b, s, width), F32)] * 2,
        scratch_shapes=[pltpu.VMEM((2, C_DV, C_HEADS * C_DK), F32),
                        pltpu.VMEM(((C_HEADS * C_DK // LANE) * C_CHUNK, C_CHUNK * LANE), BF16)],
        compiler_params=_cparams(("parallel", "arbitrary")),
        name="gla",
    )(*([p] * 8), wdec, bdec, sel)


def _merge_mlp_kernel(*refs, d, nsrc, nlat_blocks, final):
    x_refs, ya_refs, yd_refs = (refs[i * nsrc:(i + 1) * nsrc] for i in range(3))
    (mod_ref, yb_ref, of_ref, ob_ref, r_ref, g_ref, gnw_ref, wba_ref, wbb_ref, wbc_ref, wbd_ref, wo_ref,
     nw2_ref, w1_ref, w2_ref, fnw_ref, o_ref) = refs[3 * nsrc:]
    mod = mod_ref[0]
    o = of_ref[0] + ob_ref[0]
    r = r_ref[0].astype(F32)
    yg = jnp.concatenate([_rms(o[:, h * C_DV:(h + 1) * C_DV], gnw_ref[...]) for h in range(C_HEADS)], axis=1)
    yg = (yg * (r * jax.nn.sigmoid(r))).astype(BF16)
    branches = ((_token_rows(ya_refs, nlat_blocks), wba_ref), (yb_ref[0], wbb_ref), (yg, wbc_ref),
                (_token_rows(yd_refs, nlat_blocks), wbd_ref))
    mixed = []
    for c in range(d // MERGE_TN):
        cs = slice(c * MERGE_TN, (c + 1) * MERGE_TN)
        acc_z = acc_t = None
        for n, (y, w_ref) in enumerate(branches):
            z = _dot(y, w_ref[:, cs])
            tz = jnp.tanh(g_ref[0, :, n * d + c * MERGE_TN:n * d + (c + 1) * MERGE_TN].astype(F32)) * z
            acc_z = z if acc_z is None else acc_z + z
            acc_t = tz if acc_t is None else acc_t + tz
        mixed.append((0.5 * (acc_z + acc_t)).astype(BF16))
    x1 = _token_rows(x_refs, nlat_blocks) + mod[:, 2 * d:3 * d] * _dot(jnp.concatenate(mixed, axis=1), wo_ref[...])
    h = _norm_modulate(x1, nw2_ref[...], [mod], 3, d)
    a = jnp.maximum(_dot(h.astype(BF16), w1_ref[...]), 0.0)
    y = x1 + mod[:, 5 * d:6 * d] * _dot((a * a).astype(BF16), w2_ref[...])
    o_ref[0] = _rms(y, fnw_ref[...]) if final else y


def _merge_mlp(xs, modsel, ya, yb, o_f, o_b, p, yd, gnw, w_branch, wb_b, wo, nw2, w1, w2, fnw, l, *,
               tm, n_blocks, nlat_blocks, final):
    assert isinstance(xs, tuple) == isinstance(ya, tuple) == isinstance(yd, tuple)
    streams = [s if isinstance(s, tuple) else (s,) for s in (xs, ya, yd)]
    srcs = [a for s in streams for a in s]
    b, d = srcs[0].shape[0], srcs[0].shape[2]

    def tok(w, col=0):
        return pl.BlockSpec((1, tm, w), lambda bb, i: (bb, i, col))

    w_specs = [_layer_resident(w_branch, l, 0), _layer_resident(wb_b, l),
               _layer_resident(w_branch, l, 2), _layer_resident(w_branch, l, 3), _layer_resident(wo, l),
               _layer_resident(nw2, l), _layer_resident(w1, l), _layer_resident(w2, l), _resident(fnw.shape)]
    return pl.pallas_call(
        functools.partial(_merge_mlp_kernel, d=d, nsrc=len(streams[0]), nlat_blocks=nlat_blocks, final=final),
        grid=(b, n_blocks),
        in_specs=[sp for s in (xs, ya, yd) for sp in _token_specs(s, tm, nlat_blocks)] + [
            _mod_spec(modsel, l, nlat_blocks),
            tok(yb.shape[2]), tok(o_f.shape[2]), tok(o_b.shape[2]),
            tok(SEG_CR[1], SEG_CR[0] // SEG_CR[1]),
            tok(SEG_GATE[1], 0),
            _layer_resident(gnw, l), *w_specs,
        ],
        out_specs=tok(d),
        out_shape=jax.ShapeDtypeStruct((b, n_blocks * tm, d), F32),
        compiler_params=_cparams(("parallel", "parallel")),
        name="merge_mlp",
    )(*srcs, modsel, yb, o_f, o_b, p, p, gnw, w_branch, wb_b, w_branch, w_branch, wo, nw2, w1, w2, fnw)


def _pad_cols(w, n):
    return jnp.concatenate([w, jnp.zeros(w.shape[:-1] + (n,), w.dtype)], axis=-1)


def _layout_w_in(w):
    a0 = 0
    b0 = a0 + A_Q_RANK + A_KV_RANK + A_ROPE
    c0 = b0 + (B_HEADS + 2 * B_KV_HEADS) * B_HD
    d0 = c0 + 2 * C_HEADS * C_DK + 2 * C_HEADS * C_DV + 2 * C_DECAY_RANK
    g0 = d0 + (D_HEADS + 2 * D_KV_HEADS) * D_HD

    def cols(lo, n):
        return w[..., lo:lo + n]

    cq = c0
    ck = cq + C_HEADS * C_DK
    cv = ck + C_HEADS * C_DK
    cr = cv + C_HEADS * C_DV
    clr = cr + C_HEADS * C_DV
    dk_, dv_ = d0 + SEG_DQ[1], d0 + SEG_DQ[1] + SEG_DK[1]
    bk_, bv_ = b0 + SEG_BQ[1], b0 + SEG_BQ[1] + SEG_BK[1]
    parts = [
        cols(g0, SEG_GATE[1]),
        cols(cv, SEG_CV[1]), cols(cr, SEG_CR[1]), cols(cq, SEG_CQ[1]), cols(ck, SEG_CK[1]),
        cols(dv_, SEG_DV[1]), cols(bv_, SEG_BV[1]),
        _pad_cols(cols(clr, 2 * C_DECAY_RANK), SEG_CLR[1] - 2 * C_DECAY_RANK),
        cols(d0, SEG_DQ[1]), cols(dk_, SEG_DK[1]),
        *[cols(b0 + h * B_HD, B_HD) for h in _swa_head_order()],
        cols(bk_, SEG_BK[1]),
        cols(a0, A_Q_RANK), cols(a0 + A_Q_RANK, A_KV_RANK),
        _pad_cols(cols(a0 + A_Q_RANK + A_KV_RANK, A_ROPE), P_COLS - SEG_AKR[0] - A_ROPE),
    ]
    out = jnp.concatenate(parts, axis=-1).astype(BF16)
    assert out.shape[-1] == P_COLS
    return out


def _layout_mla(w_uq, w_ukv):
    zq = jnp.zeros((A_Q_RANK, A_HEAD_PAD - A_NOPE - A_ROPE), w_uq.dtype)
    hq = A_NOPE + A_ROPE
    wq = jnp.concatenate([jnp.concatenate([w_uq[:, h * hq:(h + 1) * hq], zq], axis=1)
                          for h in range(A_HEADS)], axis=1)
    hk = A_NOPE + A_V
    zk = jnp.zeros((A_KV_RANK, A_HEAD_PAD - A_NOPE), w_ukv.dtype)
    wk_top = jnp.concatenate([jnp.concatenate([w_ukv[:, h * hk:h * hk + A_NOPE], zk], axis=1)
                              for h in range(A_HEADS)], axis=1)
    place = np.zeros((SEG_AKR[1], A_HEADS * A_HEAD_PAD), np.float32)
    for h in range(A_HEADS):
        place[np.arange(A_ROPE), h * A_HEAD_PAD + A_NOPE + np.arange(A_ROPE)] = 1.0
    wk = jnp.concatenate([wk_top, jnp.asarray(place)], axis=0)
    wv = jnp.concatenate([w_ukv[:, h * hk + A_NOPE:(h + 1) * hk] for h in range(A_HEADS)], axis=1)
    lanes = np.arange(A_HEADS * A_HEAD_PAD)
    rot = ((lanes % A_HEAD_PAD) >= A_NOPE) & ((lanes % A_HEAD_PAD) < A_NOPE + A_ROPE)
    idx = _partner_lanes(lanes.size, A_ROPE // 4)

    def with_partner(w):
        return jnp.concatenate([w, jnp.where(jnp.asarray(rot)[None, :], w[:, idx], 0.0)], axis=1)

    return with_partner(wq).astype(BF16), with_partner(wk).astype(BF16), wv.astype(BF16)


def _partner_lanes(width, blk):
    j = np.arange(width)
    return np.where((j % (2 * blk)) < blk, j + blk, j - blk)


def _perm_matrix(width, blk):
    m = np.zeros((width, width), np.float32)
    m[_partner_lanes(width, blk), np.arange(width)] = 1.0
    return m


def _swa_head_order():
    grp = B_HEADS // B_KV_HEADS
    return [h for g in range(grp) for h in (g, grp + g)]


def _layout_w_branch_b(w):
    return jnp.concatenate([w[h * B_HD:(h + 1) * B_HD] for h in _swa_head_order()], axis=0)


def _layout_decay(w_decay):
    z = jnp.zeros((C_DECAY_RANK, w_decay.shape[-1]), w_decay.dtype)
    tail = jnp.zeros((SEG_CLR[1] - 2 * C_DECAY_RANK, w_decay.shape[-1]), w_decay.dtype)
    return jnp.stack([jnp.concatenate([w_decay[0], z, tail], axis=0),
                      jnp.concatenate([z, w_decay[1], tail], axis=0)]).astype(BF16)


def kernel(x, c, ctx, c_ctx, ada_w, ada_b, norm1_w, norm2_w, w_in, mla_q_norm_w, mla_w_uq, mla_kv_norm_w, mla_w_ukv, swa_sink, gla_w_decay, gla_b_decay, gla_norm_w, gqa_q_norm_w, gqa_k_norm_w, w_branch, w_o, mlp_w1, mlp_w2, final_norm_w):
    b, t, d = x.shape
    cn = ctx.shape[1]
    depth = ada_w.shape[0]
    tm = TOKEN_TILE
    assert t % tm == 0 and cn % tm == 0 and t % GRID_W == 0 and t >= SWA_BLOCK + 2 * WINDOW
    assert t % ATTN_TQ == 0 and t % cn == 0 and t % GLA_BLOCK == 0 and cn % GLA_BLOCK == 0
    nlat, nall = t // tm, (t + cn) // tm

    rows = 16
    cs = jnp.concatenate([c, c_ctx[None], jnp.zeros((rows - b - 1, d), F32)], axis=0)
    mod = _ada(cs, ada_w, ada_b)
    modsel = jnp.stack([mod[:, :b], jnp.broadcast_to(mod[:, b:b + 1], (depth, b, 6 * d))], axis=2)
    modsel = modsel.reshape(depth, b * 2, 1, 6 * d)

    tabs = (_rope_tables(t, cn, A_HEAD_PAD, A_NOPE, A_ROPE)
            + tuple(jnp.tile(tb, (1, LANE // B_HD)) for tb in _rope_tables(t, cn, B_HD, 0, B_HD))
            + _rope_tables(t, cn, D_HD, 0, D_HD))
    sel = _gla_sel()
    perms = (jnp.asarray(_perm_matrix(B_HEADS * B_HD, B_HD // 4), BF16),
             jnp.asarray(_perm_matrix(B_KV_HEADS * B_HD, B_HD // 4), BF16),
             jnp.asarray(_perm_matrix(D_HEADS * D_HD, D_HD // 4), BF16),
             jnp.asarray(_perm_matrix(D_KV_HEADS * D_HD, D_HD // 4), BF16))
    d_partner = _partner_lanes(D_HD, D_HD // 4)

    w_p = _layout_w_in(w_in)
    wbr, wo = w_branch.astype(BF16), w_o.astype(BF16)
    wb_b = jax.vmap(_layout_w_branch_b)(w_branch[:, 1]).astype(BF16)
    w1, w2 = mlp_w1.astype(BF16), mlp_w2.astype(BF16)
    wq, wk, wv = jax.vmap(_layout_mla)(mla_w_uq, mla_w_ukv)
    wdec, bdec = jax.vmap(_layout_decay)(gla_w_decay), gla_b_decay[:, :, None, :]

    def rows(w):
        return w[:, None, :]

    prep_wts = ([(rows(mla_q_norm_w), True), (rows(mla_kv_norm_w), True), (wq, True), (wk, True), (wv, True)]
                + [(pm, False) for pm in perms]
                + [(rows(gqa_q_norm_w), True), (rows(gqa_q_norm_w[:, d_partner]), True),
                   (rows(gqa_k_norm_w), True), (rows(gqa_k_norm_w[:, d_partner]), True)])
    n1, n2, gnw = rows(norm1_w), rows(norm2_w), rows(gla_norm_w)

    xs = (x, ctx)
    for l in range(depth):
        last = l == depth - 1
        p, qa, ka, va, qb, kb, qd, kd = _inproj(xs, modsel, n1, w_p, tabs, prep_wts, l, nlat)
        ya = _attention(qa, ka, va, heads=A_HEADS, kv_group=1, dk=A_HEAD_PAD, dv=A_V, vcol0=0,
                        t=t, cn=cn, with_ctx=not last, name="attn_a")
        yd = _attention(qd, kd, p, heads=D_HEADS, kv_group=D_HEADS // D_KV_HEADS, dk=D_HD, dv=D_HD,
                        vcol0=SEG_DV[0] // SEG_DV[1], t=t, cn=cn, with_ctx=not last, name="attn_d")
        yb = _swa(swa_sink, qb, kb, p, l, t=t, cn=cn,
                  n_blocks=(t if last else t + cn) // SWA_BLOCK)
        o_f, o_b = _gla(p, wdec, bdec, sel, l, t=t, cn=cn)
        if last and isinstance(xs, tuple):
            xs = xs[0]
        tile = dict(tm=LAST_TILE, n_blocks=t // LAST_TILE, nlat_blocks=t // LAST_TILE) if last else dict(
            tm=tm, n_blocks=nall, nlat_blocks=nlat)
        xs = _merge_mlp(xs, modsel, ya, yb, o_f, o_b, p, yd, gnw, wbr, wb_b, wo, n2, w1, w2,
                        final_norm_w[None], l, final=last, **tile)
    return xs
```

```python
import functools

import numpy as np
import jax
import jax.numpy as jnp
from jax import lax
from jax.experimental import pallas as pl
from jax.experimental.pallas import tpu as pltpu

F32 = jnp.float32
BF16 = jnp.bfloat16

GRID_W = 64
EPS = 1e-6
ROPE_BASE = 10000.0
NEG_INF = -1e30

A_HEADS, A_NOPE, A_ROPE, A_V, A_Q_RANK, A_KV_RANK = 4, 128, 64, 128, 256, 128
A_HEAD_PAD = 256
B_HEADS, B_KV_HEADS, B_HD, WINDOW = 8, 2, 64, 128
SWA_BLOCK = 256
SWA_ROW_TILE = 32
C_HEADS, C_DK, C_DV, C_DECAY_RANK, C_DECAY_TEMP, C_CHUNK = 4, 64, 128, 16, 16.0, 64
D_HEADS, D_KV_HEADS, D_HD = 4, 2, 128
N_BRANCH, BRANCH_W = 4, 512

LANE = 128
TOKEN_TILE = 256
LAST_TILE = 512
ATTN_TQ = 1024
ATTN_CHAIN_ROWS = 512
ATTN_HEADS_PER_STEP = 4
INPROJ_TN = 2048
MERGE_TN = 256
VMEM_LIMIT = 56 * 1024 * 1024

P_COLS = 8192
SEG_GATE = (0, 4096)
SEG_DQ, SEG_DK, SEG_DV = (4096, 512), (4608, 256), (4864, 256)
SEG_BQ, SEG_BK, SEG_BV = (5120, 512), (5632, 128), (5760, 128)
SEG_ACQ, SEG_ACKV, SEG_AKR = (5888, 256), (6144, 128), (6272, 128)
SEG_CQ, SEG_CV, SEG_CR, SEG_CK, SEG_CLR = (6400, 256), (6656, 512), (7168, 512), (7680, 256), (7936, 128)


def _cparams(sem):
    return pltpu.CompilerParams(dimension_semantics=sem, vmem_limit_bytes=VMEM_LIMIT)


def _rms(xf, w):
    return xf * lax.rsqrt(jnp.mean(xf * xf, axis=-1, keepdims=True) + EPS) * w


def _dot(a, b):
    return jnp.dot(a, b, preferred_element_type=F32)


def _dot_nt(a, b):
    return lax.dot_general(a, b, (((1,), (1,)), ((), ())), preferred_element_type=F32)


def _dot_tn(a, b):
    return lax.dot_general(a, b, (((0,), (0,)), ((), ())), preferred_element_type=F32)


def _rope_tables(t, cn, width, offset, rot):
    half = rot // 2
    q = half // 2
    inv = ROPE_BASE ** (-np.arange(q, dtype=np.float64) / q)
    pos = np.arange(t)
    cos = np.ones((t + cn, width), np.float64)
    sin = np.zeros((t + cn, width), np.float64)
    for part, p in enumerate((pos // GRID_W, pos % GRID_W)):
        ang = p[:, None] * inv[None, :]
        lo = offset + part * half
        cos[:t, lo:lo + half] = np.concatenate([np.cos(ang), np.cos(ang)], axis=1)
        sin[:t, lo:lo + half] = np.concatenate([-np.sin(ang), np.sin(ang)], axis=1)
    return jnp.asarray(cos, F32), jnp.asarray(sin, F32)


def _ada_kernel(cs_ref, w_ref, b_ref, o_ref):
    cs = cs_ref[...]
    s = cs * jax.nn.sigmoid(cs)
    o_ref[0] = _dot(s.astype(BF16), w_ref[0].astype(BF16)) + b_ref[0]


def _ada(cs, ada_w, ada_b):
    nl, d, n = ada_w.shape
    tn = 1536
    rows = cs.shape[0]
    return pl.pallas_call(
        _ada_kernel,
        grid=(nl, n // tn),
        in_specs=[
            pl.BlockSpec((rows, d), lambda l, j: (0, 0)),
            pl.BlockSpec((1, d, tn), lambda l, j: (l, 0, j)),
            pl.BlockSpec((1, 1, tn), lambda l, j: (l, 0, j)),
        ],
        out_specs=pl.BlockSpec((1, rows, tn), lambda l, j: (l, 0, j)),
        out_shape=jax.ShapeDtypeStruct((nl, rows, n), F32),
        compiler_params=_cparams(("parallel", "parallel")),
        name="ada",
    )(cs, ada_w, ada_b.reshape(nl, 1, n))


def _token_specs(xs, tm, nlat_blocks):
    if not isinstance(xs, tuple):
        return [pl.BlockSpec((1, tm, xs.shape[2]), lambda bb, i: (bb, i, 0))]
    x, ctx = xs
    return [pl.BlockSpec((1, tm, x.shape[2]), lambda bb, i: (bb, jnp.minimum(i, nlat_blocks - 1), 0)),
            pl.BlockSpec((1, tm, ctx.shape[2]), lambda bb, i: (bb, jnp.maximum(i - nlat_blocks, 0), 0))]


def _token_rows(x_refs, nlat_blocks):
    if len(x_refs) == 1:
        return x_refs[0][0]
    return jnp.where(pl.program_id(1) >= nlat_blocks, x_refs[1][0], x_refs[0][0])


def _norm_modulate(x, nw, mods, k, d):
    rows = x.shape[0] // len(mods)
    parts = [_rms(x[j * rows:(j + 1) * rows], nw) * (1.0 + m[:, (k + 1) * d:(k + 2) * d]) + m[:, k * d:(k + 1) * d]
             for j, m in enumerate(mods)]
    return parts[0] if len(parts) == 1 else jnp.concatenate(parts, axis=0)


N_PREP_TABLES, N_PREP_WEIGHTS, N_PREP_OUTS = 6, 13, 7


def _inproj_kernel(*refs, d, nsrc, nlat_blocks):
    mod_ref, nw_ref, w_ref = refs[nsrc:nsrc + 3]
    prep_refs = refs[nsrc + 3:nsrc + 3 + N_PREP_TABLES + N_PREP_WEIGHTS]
    o_ref = refs[nsrc + 3 + N_PREP_TABLES + N_PREP_WEIGHTS]
    prep_outs = refs[-N_PREP_OUTS:]
    x = _token_rows(refs[:nsrc], nlat_blocks)
    h = _norm_modulate(x, nw_ref[...], [mod_ref[0]], 0, d).astype(BF16)
    chunks = []
    for j in range(P_COLS // INPROJ_TN):
        cs = slice(j * INPROJ_TN, (j + 1) * INPROJ_TN)
        res = _dot(h, w_ref[:, cs])
        if cs.stop <= SEG_GATE[0] + SEG_GATE[1]:
            res = res * 0.5
        chunks.append(res.astype(BF16))
        o_ref[0, :, cs] = chunks[-1]

    def seg(sg):
        j, lo = divmod(sg[0], INPROJ_TN)
        assert lo + sg[1] <= INPROJ_TN
        return chunks[j][:, lo:lo + sg[1]]

    _prep_body(*[seg(sg) for sg in (SEG_ACQ, SEG_ACKV, SEG_AKR, SEG_BQ, SEG_BK, SEG_DQ, SEG_DK)],
               *prep_refs, *prep_outs)


def _resident(shape):
    return pl.BlockSpec(shape, lambda *_: (0,) * len(shape), pipeline_mode=pl.Buffered(1))


def _layer_resident(arr, *lead):
    rest = arr.shape[len(lead):]
    return pl.BlockSpec((None,) * len(lead) + rest, lambda *_: tuple(lead) + (0,) * len(rest),
                        pipeline_mode=pl.Buffered(1))


def _mod_spec(modsel, l, nlat_blocks):
    return pl.BlockSpec((None, 1) + modsel.shape[2:],
                        lambda bb, i: (l, bb * 2 + jnp.where(i >= nlat_blocks, 1, 0), 0, 0))


def _inproj(xs, modsel, nw, w_p, tabs, wts, l, nlat_blocks):
    srcs = xs if isinstance(xs, tuple) else (xs,)
    b, d = srcs[0].shape[0], srcs[0].shape[2]
    s = sum(a.shape[1] for a in srcs)
    tm = TOKEN_TILE
    assert len(tabs) == N_PREP_TABLES and len(wts) == N_PREP_WEIGHTS
    tab_specs = [pl.BlockSpec((tm, tb.shape[1]), lambda bb, i: (i, 0)) for tb in tabs]
    wt_specs = [_layer_resident(w, l) if stacked else _resident(w.shape) for w, stacked in wts]
    widths = [P_COLS, A_HEADS * A_HEAD_PAD, A_HEADS * A_HEAD_PAD, A_HEADS * A_V,
              B_HEADS * B_HD, B_KV_HEADS * B_HD, D_HEADS * D_HD, D_KV_HEADS * D_HD]
    return pl.pallas_call(
        functools.partial(_inproj_kernel, d=d, nsrc=len(srcs), nlat_blocks=nlat_blocks),
        grid=(b, s // tm),
        in_specs=(_token_specs(xs, tm, nlat_blocks)
                  + [_mod_spec(modsel, l, nlat_blocks), _layer_resident(nw, l), _layer_resident(w_p, l)]
                  + tab_specs + wt_specs),
        out_specs=[pl.BlockSpec((1, tm, w), lambda bb, i: (bb, i, 0)) for w in widths],
        out_shape=[jax.ShapeDtypeStruct((b, s, w), BF16) for w in widths],
        compiler_params=_cparams(("parallel", "parallel")),
        name="inproj",
    )(*srcs, modsel, nw, w_p, *tabs, *[w for w, _ in wts])


def _prep_body(acq, ackv, akr, bq, bk, dq, dk,
               cosa_ref, sina_ref, cosb_ref, sinb_ref, cosd_ref, sind_ref,
               aqn_ref, akvn_ref, wq_ref, wk_ref, wv_ref, permbq_ref, permbk_ref,
               permdq_ref, permdk_ref, dqn_ref, dqnp_ref, dkn_ref, dknp_ref,
               qa_ref, ka_ref, va_ref, qb_ref, kb_ref, qd_ref, kd_ref):
    na = A_HEADS * A_HEAD_PAD
    cos_r, sin_r = cosa_ref[:, A_NOPE:], sina_ref[:, A_NOPE:]

    def rope_heads(x, w_ref, o_ref, scale):
        for h in range(A_HEADS):
            c0 = h * A_HEAD_PAD
            main = _dot(x, w_ref[:, c0:c0 + A_HEAD_PAD])
            part = _dot(x, w_ref[:, na + c0 + A_NOPE:na + c0 + A_HEAD_PAD])
            out = jnp.concatenate([main[:, :A_NOPE], main[:, A_NOPE:] * cos_r + part * sin_r], axis=1)
            o_ref[0, :, c0:c0 + A_HEAD_PAD] = (out * scale).astype(BF16)

    cq = _rms(acq.astype(F32), aqn_ref[...]).astype(BF16)
    rope_heads(cq, wq_ref, qa_ref, (A_NOPE + A_ROPE) ** -0.5)
    ckv = _rms(ackv.astype(F32), akvn_ref[...]).astype(BF16)
    rope_heads(jnp.concatenate([ckv, akr], axis=1), wk_ref, ka_ref, 1.0)
    va_ref[0] = _dot(ckv, wv_ref[...]).astype(BF16)

    nrep = B_HEADS * B_HD // LANE
    cosb, sinb = cosb_ref[...], sinb_ref[...]
    qb = (bq.astype(F32) * jnp.concatenate([cosb] * nrep, axis=1)
          + _dot(bq, permbq_ref[...]) * jnp.concatenate([sinb] * nrep, axis=1))
    qb_ref[0] = (qb * (B_HD ** -0.5)).astype(BF16)
    kb_ref[0] =(bk.astype(F32) * cosb + _dot(bk, permbk_ref[...]) * sinb).astype(BF16)

    cosd, sind = cosd_ref[...], sind_ref[...]

    def norm_rope(x, perm_ref, nw_ref, nwp_ref, heads):
        xp = _dot(x, perm_ref[...])
        xf = x.astype(F32)
        wc, ws = nw_ref[...] * cosd, nwp_ref[...] * sind
        parts = []
        for h in range(heads):
            hs = slice(h * D_HD, (h + 1) * D_HD)
            xh = xf[:, hs]
            inv = lax.rsqrt(jnp.mean(xh * xh, axis=-1, keepdims=True) + EPS)
            parts.append(inv * (xh * wc + xp[:, hs] * ws))
        return jnp.concatenate(parts, axis=1)

    qd_ref[0] = (norm_rope(dq, permdq_ref, dqn_ref, dqnp_ref, D_HEADS) * (D_HD ** -0.5)).astype(BF16)
    kd_ref[0] = norm_rope(dk, permdk_ref, dkn_ref, dknp_ref, D_KV_HEADS).astype(BF16)


def _attn_kernel(q_ref, k_ref, v_ref, o_ref, *, dk, dv, kv_group):
    rows = min(ATTN_CHAIN_ROWS, q_ref.shape[1])
    for c in range(ATTN_HEADS_PER_STEP):
        kc = c // kv_group
        s_all = _dot_nt(q_ref[0, :, c * dk:(c + 1) * dk], k_ref[0, :, kc * dk:(kc + 1) * dk])
        for r in range(q_ref.shape[1] // rows):
            rs = slice(r * rows, (r + 1) * rows)
            s = s_all[rs]
            m = jnp.max(s, axis=-1, keepdims=True)
            p = jnp.exp(s - m)
            l = jnp.sum(p, axis=-1, keepdims=True)
            o = _dot(p.astype(BF16), v_ref[0, :, kc * dv:(kc + 1) * dv])
            o_ref[0, rs, c * dv:(c + 1) * dv] = (o / l).astype(o_ref.dtype)


def _attention(q, k, v, *, heads, kv_group, dk, dv, vcol0, t, cn, with_ctx, name):
    b, s, _ = q.shape
    hps = ATTN_HEADS_PER_STEP
    nkv = hps // kv_group
    kern = functools.partial(_attn_kernel, dk=dk, dv=dv, kv_group=kv_group)
    sem = _cparams(("parallel", "parallel", "parallel"))

    def call(tq, nq, row0, krows, krow0, nm):
        return pl.pallas_call(
            kern,
            grid=(b, heads // hps, nq // tq),
            in_specs=[
                pl.BlockSpec((1, tq, hps * dk), lambda bb, h, i: (bb, row0 + i, h)),
                pl.BlockSpec((1, krows, nkv * dk), lambda bb, h, i: (bb, krow0, h)),
                pl.BlockSpec((1, krows, nkv * dv), lambda bb, h, i: (bb, krow0, vcol0 + h)),
            ],
            out_specs=pl.BlockSpec((1, tq, hps * dv), lambda bb, h, i: (bb, i, h)),
            out_shape=jax.ShapeDtypeStruct((b, nq, heads * dv), BF16),
            compiler_params=sem,
            name=nm,
        )(q, k, v)

    y = call(ATTN_TQ, t, 0, s, 0, name)
    if with_ctx:
        return y, call(cn, cn, t // cn, cn, t // cn, name + "_ctx")
    return y


def _swa_bias():
    band = SWA_BLOCK + 2 * WINDOW
    r = np.arange(SWA_BLOCK)[:, None]
    j = np.arange(band)[None, :]
    out = []
    for off in (0, WINDOW, band - SWA_BLOCK):
        out.append(np.where(np.abs(off + r - j) <= WINDOW, 0.0, NEG_INF))
    return np.stack(out).astype(np.float32)


def _swa_kernel(sink_ref, q_ref, k_ref, v_ref, bias_ref, o_ref, s_ref, p_ref, e_ref, *, t, cn, layer):
    n = pl.program_id(1)
    blk = SWA_BLOCK
    nlat = t // blk
    grp = B_HEADS // B_KV_HEADS
    band = blk + 2 * WINDOW
    rt = SWA_ROW_TILE

    def run(is_ctx):
        q = q_ref[0]
        k, v = k_ref[0, t:t + cn, :], v_ref[0, t:t + cn, :]
        if not is_ctx:
            start = pl.multiple_of(jnp.clip(n * blk - WINDOW, 0, t - band), WINDOW)
            k = jnp.concatenate([k, k_ref[0, pl.ds(start, band), :]], axis=0)
            v = jnp.concatenate([v, v_ref[0, pl.ds(start, band), :]], axis=0)
        nk = k.shape[0]
        halves = []
        qlane = lax.broadcasted_iota(jnp.int32, (blk, LANE), 1)
        for hk in range(B_KV_HEADS):
            mine = ((qlane // B_HD) == hk).astype(BF16)
            q4 = jnp.concatenate([q[:, gi * LANE:(gi + 1) * LANE] * mine for gi in range(grp)], axis=0)
            s_ref[hk, :, :nk] = _dot_nt(q4, k)

            for i in range(grp * blk // rt):
                rs = slice(i * rt, (i + 1) * rt)
                sink = sink_ref[layer, hk * grp + i // (blk // rt)]
                s = s_ref[hk, rs, :nk]
                if not is_ctx:
                    br = (i % (blk // rt)) * rt
                    s = s + bias_ref[0, br:br + rt, :]
                m = jnp.maximum(jnp.max(s, axis=-1, keepdims=True), sink)
                p = jnp.exp(s - m)
                l = jnp.sum(p, axis=-1, keepdims=True) + jnp.exp(sink - m)
                p_ref[hk, rs, :nk] = p.astype(BF16)
                e_ref[hk, rs, :] = jnp.broadcast_to(l, (rt, LANE))
            halves.append(_dot(p_ref[hk, :, :nk], v) / e_ref[hk])
        lane = lax.broadcasted_iota(jnp.int32, (blk, LANE), 1)
        outs = [jnp.where(lane < B_HD, halves[0][gi * blk:(gi + 1) * blk], halves[1][gi * blk:(gi + 1) * blk])
                for gi in range(grp)]
        o_ref[0] = jnp.concatenate(outs, axis=1).astype(o_ref.dtype)

    @pl.when(n < nlat)
    def _():
        run(False)

    @pl.when(n >= nlat)
    def _():
        run(True)


def _swa(sink, q, k, p, l, *, t, cn, n_blocks):
    b, s, _ = q.shape
    nlat = t // SWA_BLOCK
    rows = (B_HEADS // B_KV_HEADS) * SWA_BLOCK
    band_bias = _swa_bias()
    bias = jnp.asarray(np.concatenate([np.zeros(band_bias.shape[:2] + (cn,), np.float32), band_bias], axis=2))
    return pl.pallas_call(
        functools.partial(_swa_kernel, t=t, cn=cn, layer=l),
        grid=(b, n_blocks),
        in_specs=[
            pl.BlockSpec(memory_space=pltpu.SMEM),
            pl.BlockSpec((1, SWA_BLOCK, B_HEADS * B_HD), lambda bb, n: (bb, n, 0)),
            pl.BlockSpec((1, s, B_KV_HEADS * B_HD), lambda bb, n: (bb, 0, 0)),
            pl.BlockSpec((1, s, SEG_BV[1]), lambda bb, n: (bb, 0, SEG_BV[0] // SEG_BV[1])),
            pl.BlockSpec((1,) + bias.shape[1:],
                         lambda bb, n: (jnp.where(n == 0, 0, jnp.where(n >= nlat - 1, 2, 1)), 0, 0)),
        ],
        out_specs=pl.BlockSpec((1, SWA_BLOCK, B_HEADS * B_HD), lambda bb, n: (bb, n, 0)),
        out_shape=jax.ShapeDtypeStruct((b, n_blocks * SWA_BLOCK, B_HEADS * B_HD), BF16),
        scratch_shapes=[pltpu.VMEM((B_KV_HEADS, rows, bias.shape[2]), F32),
                        pltpu.VMEM((B_KV_HEADS, rows, bias.shape[2]), BF16),
                        pltpu.VMEM((B_KV_HEADS, rows, LANE), F32)],
        compiler_params=_cparams(("parallel", "parallel")),
        name="swa",
    )(sink, q, k, p, bias)


GLA_BLOCK = 256
GLA_SAFE_DECAY = 60.0


def _log_decay(lr, w, bias):
    z = _dot(lr, w) + bias
    return (jnp.minimum(z, 0.0) - jnp.log(1.0 + jnp.exp(-jnp.abs(z)))) / C_DECAY_TEMP


def _cum_decay(la, tri):
    hi = la.astype(BF16)
    lo = (la - hi.astype(F32)).astype(BF16)
    return _dot(tri, hi) + _dot(tri, lo)


def _gla_kernel(qf_ref, kf_ref, vf_ref, lrf_ref, qb_ref, kb_ref, vb_ref, lrb_ref,
                wdec_ref, bdec_ref, sel_ref, of_ref, ob_ref, st_ref, e_ref, st0_ref):
    n = GLA_BLOCK
    pair = LANE // C_DK
    npair = C_HEADS // pair
    qscale = C_DK ** -0.5
    refs = ((qf_ref, kf_ref, vf_ref, lrf_ref), (qb_ref, kb_ref, vb_ref, lrb_ref))
    outs = (of_ref, ob_ref)

    @pl.when(pl.program_id(1) == 0)
    def _():
        st_ref[...] = jnp.zeros_like(st_ref)

    r_i = lax.broadcasted_iota(jnp.int32, (n, n), 0)
    c_i = lax.broadcasted_iota(jnp.int32, (n, n), 1)
    keep = (c_i <= r_i, c_i >= r_i)
    lane = lax.broadcasted_iota(jnp.int32, (1, LANE), 1)
    head_mask = [(lane // C_DK) == hh for hh in range(pair)]

    cums, tots = [], []
    for d in range(2):
        la = _log_decay(refs[d][3][0], wdec_ref[d], bdec_ref[d])
        cum = _cum_decay(la, keep[d].astype(BF16))
        cums.append(cum)
        tots.append(cum[n - 1:n] if d == 0 else cum[0:1])
    safe = jnp.maximum(jnp.max(-tots[0]), jnp.max(-tots[1])) <= GLA_SAFE_DECAY

    st0_ref[...] = st_ref[...]

    def factored():
        for d in range(2):
            q_ref, k_ref, v_ref, _ = refs[d]
            cum, tot = cums[d], tots[d]
            half = 0.5 * tot
            a = cum - half
            eh = jnp.exp(half)
            qr = q_ref[0].astype(F32) * qscale * jnp.exp(a)
            kr = k_ref[0].astype(F32) * jnp.exp(-a)
            kr_b, kd_b = kr.astype(BF16), (kr * eh).astype(BF16)
            qe = qr * eh
            v, st = v_ref[0], st_ref[d]
            o_parts, u_parts = [], []
            for p in range(npair):
                ls = slice(p * LANE, (p + 1) * LANE)
                st2 = st[:, ls].astype(BF16)
                upd = []
                for hh in range(pair):
                    h = p * pair + hh
                    vh = v[:, h * C_DV:(h + 1) * C_DV]
                    qr_h = jnp.where(head_mask[hh], qr[:, ls], 0.0).astype(BF16)
                    qe_h = jnp.where(head_mask[hh], qe[:, ls], 0.0).astype(BF16)
                    pm = jnp.where(keep[d], _dot_nt(qr_h, kr_b[:, ls]), 0.0).astype(BF16)
                    o_parts.append(_dot(pm, vh) + _dot_nt(qe_h, st2))
                    upd.append(_dot_tn(vh, kd_b[:, ls]))
                u_parts.append(jnp.where(head_mask[0], upd[0], upd[1]))
            outs[d][0] = jnp.concatenate(o_parts, axis=1)
            st_ref[d] = st * jnp.exp(tot) + jnp.concatenate(u_parts, axis=1)

    factored()

    @pl.when(jnp.logical_not(safe))
    def _():
        st_ref[...] = st0_ref[...]
        c = C_CHUNK
        tile = 16
        nsub = n // c
        r64 = lax.broadcasted_iota(jnp.int32, (c, c), 0)
        c64 = lax.broadcasted_iota(jnp.int32, (c, c), 1)
        tri64 = ((c64 <= r64).astype(BF16), (c64 >= r64).astype(BF16))
        for d in range(2):
            q_ref, k_ref, v_ref, lr_ref = refs[d]
            e_ref[...] = jnp.zeros_like(e_ref)

            def sub(i, carry, d=d, q_ref=q_ref, k_ref=k_ref, v_ref=v_ref, lr_ref=lr_ref):
                rows = pl.ds(pl.multiple_of((i if d == 0 else nsub - 1 - i) * c, c), c)
                q = q_ref[0, rows, :].astype(F32) * qscale
                k = k_ref[0, rows, :].astype(F32)
                v = v_ref[0, rows, :]
                cum = _cum_decay(_log_decay(lr_ref[0, rows, :], wdec_ref[d], bdec_ref[d]), tri64[d])
                tot = cum[c - 1:c] if d == 0 else cum[0:1]
                qe = (q * jnp.exp(cum)).astype(BF16)
                kd = (k * jnp.exp(tot - cum)).astype(BF16)
                for p in range(npair):
                    ls = slice(p * LANE, (p + 1) * LANE)
                    q2, k2, c2 = q[:, ls], k[:, ls], cum[:, ls]
                    for s in range(c):
                        bt = (s // tile) * tile
                        t0, t1 = (bt, c) if d == 0 else (0, bt + tile)
                        e = q2[t0:t1] * k2[s:s + 1] * jnp.exp(c2[t0:t1] - c2[s:s + 1])
                        ridx = lax.broadcasted_iota(jnp.int32, (t1 - t0, LANE), 0) + t0
                        causal = (ridx >= s) if d == 0 else (ridx <= s)
                        e_ref[p * c + t0:p * c + t1, s * LANE:(s + 1) * LANE] = (
                            jnp.where(causal, e, 0.0).astype(BF16))
                pm = _dot(e_ref[...], sel_ref[...])
                st = st_ref[d]
                o_parts, u_parts = [], []
                for h in range(C_HEADS):
                    p, hh = divmod(h, pair)
                    ph = pm[p * c:(p + 1) * c, hh * c:(hh + 1) * c].astype(BF16)
                    vh = v[:, h * C_DV:(h + 1) * C_DV]
                    hs = slice(h * C_DK, (h + 1) * C_DK)
                    o_parts.append(_dot(ph, vh) + _dot_nt(qe[:, hs], st[:, hs].astype(BF16)))
                    u_parts.append(_dot_tn(vh, kd[:, hs]))
                outs[d][0, rows, :] = jnp.concatenate(o_parts, axis=1)
                st_ref[d] = st * jnp.exp(tot) + jnp.concatenate(u_parts, axis=1)
                return carry

            lax.fori_loop(0, nsub, sub, 0)


def _gla_sel():
    pair = LANE // C_DK
    sel = np.zeros((C_CHUNK, pair, C_DK, pair, C_CHUNK), np.float32)
    for s in range(C_CHUNK):
        for h in range(pair):
            sel[s, h, :, h, s] = 1.0
    return jnp.asarray(sel.reshape(C_CHUNK * LANE, pair * C_CHUNK), BF16)


def _gla(p, wdec, bdec, sel, l, *, t, cn):
    b, s, _ = p.shape
    n = GLA_BLOCK
    nbl, nbc = t // n, cn // n

    def fwd(j):
        return jnp.where(j < nbc, nbl + j, j - nbc)

    def bwd(j):
        return nbl + nbc - 1 - j

    def specs(order):
        return [pl.BlockSpec((1, n, sg[1]), lambda bb, j, sg=sg: (bb, order(j), sg[0] // sg[1]))
                for sg in (SEG_CQ, SEG_CK, SEG_CV, SEG_CLR)]

    width = C_HEADS * C_DV
    return pl.pallas_call(
        _gla_kernel,
        grid=(b, nbl + nbc),
        in_specs=specs(fwd) + specs(bwd) + [
            _layer_resident(wdec, l), _layer_resident(bdec, l), _resident(sel.shape),
        ],
        out_specs=[pl.BlockSpec((1, n, width), lambda bb, j: (bb, fwd(j), 0)),
                   pl.BlockSpec((1, n, width), lambda bb, j: (bb, bwd(j), 0))],
        out_shape=[jax.ShapeDtypeStruct((b, s, width), F32)] * 2,
        scratch_shapes=[pltpu.VMEM((2, C_DV, C_HEADS * C_DK), F32),
                        pltpu.VMEM(((C_HEADS * C_DK // LANE) * C_CHUNK, C_CHUNK * LANE), BF16),
                        pltpu.VMEM((2, C_DV, C_HEADS * C_DK), F32)],
        compiler_params=_cparams(("parallel", "arbitrary")),
        name="gla",
    )(*([p] * 8), wdec, bdec, sel)


def _merge_mlp_kernel(*refs, d, nsrc, nlat_blocks, final):
    x_refs, ya_refs, yd_refs = (refs[i * nsrc:(i + 1) * nsrc] for i in range(3))
    (mod_ref, yb_ref, of_ref, ob_ref, r_ref, g_ref, gnw_ref, wba_ref, wbb_ref, wbc_ref, wbd_ref, wo_ref,
     nw2_ref, w1_ref, w2_ref, fnw_ref, o_ref) = refs[3 * nsrc:]
    mod = mod_ref[0]
    o = of_ref[0] + ob_ref[0]
    r = r_ref[0].astype(F32)
    yg = jnp.concatenate([_rms(o[:, h * C_DV:(h + 1) * C_DV], gnw_ref[...]) for h in range(C_HEADS)], axis=1)
    yg = (yg * (r * jax.nn.sigmoid(r))).astype(BF16)
    branches = ((_token_rows(ya_refs, nlat_blocks), wba_ref), (yb_ref[0], wbb_ref), (yg, wbc_ref),
                (_token_rows(yd_refs, nlat_blocks), wbd_ref))
    mixed = []
    for c in range(d // MERGE_TN):
        cs = slice(c * MERGE_TN, (c + 1) * MERGE_TN)
        acc_z = acc_t = None
        for n, (y, w_ref) in enumerate(branches):
            z = _dot(y, w_ref[:, cs])
            tz = jnp.tanh(g_ref[0, :, n * d + c * MERGE_TN:n * d + (c + 1) * MERGE_TN].astype(F32)) * z
            acc_z = z if acc_z is None else acc_z + z
            acc_t = tz if acc_t is None else acc_t + tz
        mixed.append((0.5 * (acc_z + acc_t)).astype(BF16))
    x1 = _token_rows(x_refs, nlat_blocks) + mod[:, 2 * d:3 * d] * _dot(jnp.concatenate(mixed, axis=1), wo_ref[...])
    h = _norm_modulate(x1, nw2_ref[...], [mod], 3, d)
    a = jnp.maximum(_dot(h.astype(BF16), w1_ref[...]), 0.0)
    y = x1 + mod[:, 5 * d:6 * d] * _dot((a * a).astype(BF16), w2_ref[...])
    o_ref[0] = _rms(y, fnw_ref[...]) if final else y


def _merge_mlp(xs, modsel, ya, yb, o_f, o_b, p, yd, gnw, w_branch, wb_b, wo, nw2, w1, w2, fnw, l, *,
               tm, n_blocks, nlat_blocks, final):
    assert isinstance(xs, tuple) == isinstance(ya, tuple) == isinstance(yd, tuple)
    streams = [s if isinstance(s, tuple) else (s,) for s in (xs, ya, yd)]
    srcs = [a for s in streams for a in s]
    b, d = srcs[0].shape[0], srcs[0].shape[2]

    def tok(w, col=0):
        return pl.BlockSpec((1, tm, w), lambda bb, i: (bb, i, col))

    w_specs = [_layer_resident(w_branch, l, 0), _layer_resident(wb_b, l),
               _layer_resident(w_branch, l, 2), _layer_resident(w_branch, l, 3), _layer_resident(wo, l),
               _layer_resident(nw2, l), _layer_resident(w1, l), _layer_resident(w2, l), _resident(fnw.shape)]
    return pl.pallas_call(
        functools.partial(_merge_mlp_kernel, d=d, nsrc=len(streams[0]), nlat_blocks=nlat_blocks, final=final),
        grid=(b, n_blocks),
        in_specs=[sp for s in (xs, ya, yd) for sp in _token_specs(s, tm, nlat_blocks)] + [
            _mod_spec(modsel, l, nlat_blocks),
            tok(yb.shape[2]), tok(o_f.shape[2]), tok(o_b.shape[2]),
            tok(SEG_CR[1], SEG_CR[0] // SEG_CR[1]),
            tok(SEG_GATE[1], 0),
            _layer_resident(gnw, l), *w_specs,
        ],
        out_specs=tok(d),
        out_shape=jax.ShapeDtypeStruct((b, n_blocks * tm, d), F32),
        compiler_params=_cparams(("parallel", "parallel")),
        name="merge_mlp",
    )(*srcs, modsel, yb, o_f, o_b, p, p, gnw, w_branch, wb_b, w_branch, w_branch, wo, nw2, w1, w2, fnw)


def _pad_cols(w, n):
    return jnp.concatenate([w, jnp.zeros(w.shape[:-1] + (n,), w.dtype)], axis=-1)


def _layout_w_in(w):
    a0 = 0
    b0 = a0 + A_Q_RANK + A_KV_RANK + A_ROPE
    c0 = b0 + (B_HEADS + 2 * B_KV_HEADS) * B_HD
    d0 = c0 + 2 * C_HEADS * C_DK + 2 * C_HEADS * C_DV + 2 * C_DECAY_RANK
    g0 = d0 + (D_HEADS + 2 * D_KV_HEADS) * D_HD

    def cols(lo, n):
        return w[..., lo:lo + n]

    cq = c0
    ck = cq + C_HEADS * C_DK
    cv = ck + C_HEADS * C_DK
    cr = cv + C_HEADS * C_DV
    clr = cr + C_HEADS * C_DV
    parts = [
        cols(g0, SEG_GATE[1]),
        cols(d0, SEG_DQ[1] + SEG_DK[1] + SEG_DV[1]),
        *[cols(b0 + h * B_HD, B_HD) for h in _swa_head_order()],
        cols(b0 + SEG_BQ[1], SEG_BK[1] + SEG_BV[1]),
        cols(a0, A_Q_RANK + A_KV_RANK),
        _pad_cols(cols(a0 + A_Q_RANK + A_KV_RANK, A_ROPE), SEG_AKR[1] - A_ROPE),
        cols(cq, SEG_CQ[1]), cols(cv, SEG_CV[1]), cols(cr, SEG_CR[1]), cols(ck, SEG_CK[1]),
        _pad_cols(cols(clr, 2 * C_DECAY_RANK), P_COLS - SEG_CLR[0] - 2 * C_DECAY_RANK),
    ]
    return jnp.concatenate(parts, axis=-1).astype(BF16)


def _layout_mla(w_uq, w_ukv):
    zq = jnp.zeros((A_Q_RANK, A_HEAD_PAD - A_NOPE - A_ROPE), w_uq.dtype)
    hq = A_NOPE + A_ROPE
    wq = jnp.concatenate([jnp.concatenate([w_uq[:, h * hq:(h + 1) * hq], zq], axis=1)
                          for h in range(A_HEADS)], axis=1)
    hk = A_NOPE + A_V
    zk = jnp.zeros((A_KV_RANK, A_HEAD_PAD - A_NOPE), w_ukv.dtype)
    wk_top = jnp.concatenate([jnp.concatenate([w_ukv[:, h * hk:h * hk + A_NOPE], zk], axis=1)
                              for h in range(A_HEADS)], axis=1)
    place = np.zeros((SEG_AKR[1], A_HEADS * A_HEAD_PAD), np.float32)
    for h in range(A_HEADS):
        place[np.arange(A_ROPE), h * A_HEAD_PAD + A_NOPE + np.arange(A_ROPE)] = 1.0
    wk = jnp.concatenate([wk_top, jnp.asarray(place)], axis=0)
    wv = jnp.concatenate([w_ukv[:, h * hk + A_NOPE:(h + 1) * hk] for h in range(A_HEADS)], axis=1)
    lanes = np.arange(A_HEADS * A_HEAD_PAD)
    rot = ((lanes % A_HEAD_PAD) >= A_NOPE) & ((lanes % A_HEAD_PAD) < A_NOPE + A_ROPE)
    idx = _partner_lanes(lanes.size, A_ROPE // 4)

    def with_partner(w):
        return jnp.concatenate([w, jnp.where(jnp.asarray(rot)[None, :], w[:, idx], 0.0)], axis=1)

    return with_partner(wq).astype(BF16), with_partner(wk).astype(BF16), wv.astype(BF16)


def _partner_lanes(width, blk):
    j = np.arange(width)
    return np.where((j % (2 * blk)) < blk, j + blk, j - blk)


def _perm_matrix(width, blk):
    m = np.zeros((width, width), np.float32)
    m[_partner_lanes(width, blk), np.arange(width)] = 1.0
    return m


def _swa_head_order():
    grp = B_HEADS // B_KV_HEADS
    return [h for g in range(grp) for h in (g, grp + g)]


def _layout_w_branch_b(w):
    return jnp.concatenate([w[h * B_HD:(h + 1) * B_HD] for h in _swa_head_order()], axis=0)


def _layout_decay(w_decay):
    z = jnp.zeros((C_DECAY_RANK, w_decay.shape[-1]), w_decay.dtype)
    tail = jnp.zeros((SEG_CLR[1] - 2 * C_DECAY_RANK, w_decay.shape[-1]), w_decay.dtype)
    return jnp.stack([jnp.concatenate([w_decay[0], z, tail], axis=0),
                      jnp.concatenate([z, w_decay[1], tail], axis=0)]).astype(BF16)


def kernel(x, c, ctx, c_ctx, ada_w, ada_b, norm1_w, norm2_w, w_in, mla_q_norm_w, mla_w_uq, mla_kv_norm_w, mla_w_ukv, swa_sink, gla_w_decay, gla_b_decay, gla_norm_w, gqa_q_norm_w, gqa_k_norm_w, w_branch, w_o, mlp_w1, mlp_w2, final_norm_w):
    b, t, d = x.shape
    cn = ctx.shape[1]
    depth = ada_w.shape[0]
    tm = TOKEN_TILE
    assert t % tm == 0 and cn % tm == 0 and t % GRID_W == 0 and t >= SWA_BLOCK + 2 * WINDOW
    assert t % ATTN_TQ == 0 and t % cn == 0 and t % GLA_BLOCK == 0 and cn % GLA_BLOCK == 0
    nlat, nall = t // tm, (t + cn) // tm

    rows = 16
    cs = jnp.concatenate([c, c_ctx[None], jnp.zeros((rows - b - 1, d), F32)], axis=0)
    mod = _ada(cs, ada_w, ada_b)
    modsel = jnp.stack([mod[:, :b], jnp.broadcast_to(mod[:, b:b + 1], (depth, b, 6 * d))], axis=2)
    modsel = modsel.reshape(depth, b * 2, 1, 6 * d)

    tabs = (_rope_tables(t, cn, A_HEAD_PAD, A_NOPE, A_ROPE)
            + tuple(jnp.tile(tb, (1, LANE // B_HD)) for tb in _rope_tables(t, cn, B_HD, 0, B_HD))
            + _rope_tables(t, cn, D_HD, 0, D_HD))
    sel = _gla_sel()
    perms = (jnp.asarray(_perm_matrix(B_HEADS * B_HD, B_HD // 4), BF16),
             jnp.asarray(_perm_matrix(B_KV_HEADS * B_HD, B_HD // 4), BF16),
             jnp.asarray(_perm_matrix(D_HEADS * D_HD, D_HD // 4), BF16),
             jnp.asarray(_perm_matrix(D_KV_HEADS * D_HD, D_HD // 4), BF16))
    d_partner = _partner_lanes(D_HD, D_HD // 4)

    w_p = _layout_w_in(w_in)
    wbr, wo = w_branch.astype(BF16), w_o.astype(BF16)
    wb_b = jax.vmap(_layout_w_branch_b)(w_branch[:, 1]).astype(BF16)
    w1, w2 = mlp_w1.astype(BF16), mlp_w2.astype(BF16)
    wq, wk, wv = jax.vmap(_layout_mla)(mla_w_uq, mla_w_ukv)
    wdec, bdec = jax.vmap(_layout_decay)(gla_w_decay), gla_b_decay[:, :, None, :]

    def rows(w):
        return w[:, None, :]

    prep_wts = ([(rows(mla_q_norm_w), True), (rows(mla_kv_norm_w), True), (wq, True), (wk, True), (wv, True)]
                + [(pm, False) for pm in perms]
                + [(rows(gqa_q_norm_w), True), (rows(gqa_q_norm_w[:, d_partner]), True),
                   (rows(gqa_k_norm_w), True), (rows(gqa_k_norm_w[:, d_partner]), True)])
    n1, n2, gnw = rows(norm1_w), rows(norm2_w), rows(gla_norm_w)

    xs = (x, ctx)
    for l in range(depth):
        last = l == depth - 1
        p, qa, ka, va, qb, kb, qd, kd = _inproj(xs, modsel, n1, w_p, tabs, prep_wts, l, nlat)
        ya = _attention(qa, ka, va, heads=A_HEADS, kv_group=1, dk=A_HEAD_PAD, dv=A_V, vcol0=0,
                        t=t, cn=cn, with_ctx=not last, name="attn_a")
        yd = _attention(qd, kd, p, heads=D_HEADS, kv_group=D_HEADS // D_KV_HEADS, dk=D_HD, dv=D_HD,
                        vcol0=SEG_DV[0] // SEG_DV[1], t=t, cn=cn, with_ctx=not last, name="attn_d")
        yb = _swa(swa_sink, qb, kb, p, l, t=t, cn=cn,
                  n_blocks=(t if last else t + cn) // SWA_BLOCK)
        o_f, o_b = _gla(p, wdec, bdec, sel, l, t=t, cn=cn)
        if last and isinstance(xs, tuple):
            xs = xs[0]
        tile = dict(tm=LAST_TILE, n_blocks=t // LAST_TILE, nlat_blocks=t // LAST_TILE) if last else dict(
            tm=tm, n_blocks=nall, nlat_blocks=nlat)
        xs = _merge_mlp(xs, modsel, ya, yb, o_f, o_b, p, yd, gnw, wbr, wb_b, wo, n2, w1, w2,
                        final_norm_w[None], l, final=last, **tile)
    return xs
```
